```python
import math
import jax, jax.numpy as jnp
from jax import lax
import numpy as np

D_MODEL = 2048
BATCH = 4
SEQ = 2048
DEPTH = 1
DEC_BATCH = 128
DEC_SEQ = 4
PAST_LEN = 16384
PAGE_SIZE = 128

EXPAND = 2
D_MIX = EXPAND * D_MODEL
D_RET = D_MIX // 2
D_RWKV = D_MIX - D_RET
RET_HEADS = 8
RET_HEAD_DIM = D_RET // RET_HEADS
RWKV_HEAD_DIM = 64
RWKV_HEADS = D_RWKV // RWKV_HEAD_DIM
DECAY_LORA = max(32, int(round(1.8 * math.sqrt(D_RWKV) / 32)) * 32)
A_LORA = DECAY_LORA
RET_CHUNK = 128
ROPE_THETA = 10000.0
NORM_EPS = 1e-6
GN_EPS = 1e-5
RWKV_GN_EPS = 64e-5

RET_Q = 0
RET_K = RET_Q + D_RET
RET_V = RET_K + D_RET
RET_G = RET_V + D_RET
RW_R = RET_G + D_RET
RW_K = RW_R + D_RWKV
RW_V = RW_K + D_RWKV
RW_G = RW_V + D_RWKV
RW_WD = RW_G + D_RWKV
RW_AD = RW_WD + DECAY_LORA
N_IN = RW_AD + A_LORA
N_SHIFT = 3 * D_RWKV + DECAY_LORA + A_LORA

kernel_name = 'hymba_retention_rwkv7_adaln_step'


def rms_norm(x, w):
    xf = x.astype(jnp.float32)
    return xf * lax.rsqrt(jnp.mean(xf * xf, axis=-1, keepdims=True) + NORM_EPS) * w.astype(jnp.float32)


def group_norm(x, eps):
    mean = jnp.mean(x, axis=-1, keepdims=True)
    xc = x - mean
    var = jnp.mean(xc * xc, axis=-1, keepdims=True)
    return xc * lax.rsqrt(var + eps)


def rotary(x, pos):
    half = x.shape[-1] // 2
    inv_freq = ROPE_THETA ** (-jnp.arange(half, dtype=jnp.float32) / half)
    ang = pos[:, None] * inv_freq[None, :]
    cos = jnp.cos(ang)[None, :, None, :]
    sin = jnp.sin(ang)[None, :, None, :]
    x1, x2 = x[..., :half], x[..., half:]
    return jnp.concatenate([x1 * cos - x2 * sin, x1 * sin + x2 * cos], axis=-1)


def retention(q, k, v, s0):
    b, l, h, d = q.shape
    chunk = math.gcd(l, RET_CHUNK)
    n = l // chunk
    lg = jnp.log1p(-jnp.exp2(-5.0 - jnp.arange(h, dtype=jnp.float32)))
    idx = jnp.arange(chunk, dtype=jnp.float32)
    diff = idx[:, None] - idx[None, :]
    dmask = jnp.where(diff[None] >= 0, jnp.exp(jnp.maximum(diff, 0.0)[None] * lg[:, None, None]), 0.0)
    cross_decay = jnp.exp((idx[None, :] + 1.0) * lg[:, None])
    key_decay = jnp.exp((chunk - 1.0 - idx[None, :]) * lg[:, None])
    chunk_decay = jnp.exp(chunk * lg)

    def to_chunks(t):
        return t.reshape(b, n, chunk, h, d).transpose(1, 0, 3, 2, 4)

    def step(s, inp):
        qc, kc, vc = inp
        scores = jnp.einsum('bhid,bhjd->bhij', qc, kc) * dmask[None]
        inner = jnp.einsum('bhij,bhjd->bhid', scores, vc)
        cross = jnp.einsum('bhid,bhde->bhie', qc, s) * cross_decay[None, :, :, None]
        s = s * chunk_decay[None, :, None, None] + jnp.einsum('bhjd,bhje->bhde', kc * key_decay[None, :, :, None], vc)
        return s, inner + cross

    s, out = lax.scan(step, s0, (to_chunks(q), to_chunks(k), to_chunks(v)))
    out = out.transpose(1, 0, 3, 2, 4).reshape(b, l, h, d)
    return out, s


def rwkv7_recurrence(r, w, k, v, kk, a, s0):
    def step(s, inp):
        r_t, w_t, k_t, v_t, kk_t, a_t = inp
        sa = jnp.einsum('bhvk,bhk->bhv', s, -kk_t)
        s = (s * w_t[:, :, None, :] + sa[..., None] * (kk_t * a_t)[:, :, None, :]
             + v_t[..., None] * k_t[:, :, None, :])
        return s, jnp.einsum('bhvk,bhk->bhv', s, r_t)

    xs = (jnp.swapaxes(r, 0, 1), jnp.swapaxes(w, 0, 1), jnp.swapaxes(k, 0, 1),
          jnp.swapaxes(v, 0, 1), jnp.swapaxes(kk, 0, 1), jnp.swapaxes(a, 0, 1))
    s, out = lax.scan(step, s0, xs)
    return jnp.swapaxes(out, 0, 1), s


def shift_cols(t):
    return jnp.concatenate([t[..., RW_R:RW_G], t[..., RW_WD:N_IN]], axis=-1)


def hybrid_layer(x, c, s_ret, s_rwkv, h_prev, pos0, norm_w, w_ada, b_ada, w_in, mu_shift,
                 w0_decay, w2_decay, a0, a2, k_k, k_a, r_k, ln_x_w, ln_x_b, w_out):
    f32 = jnp.float32
    b, l, _ = x.shape
    mod = jax.nn.silu(c.astype(f32)) @ w_ada.astype(f32) + b_ada.astype(f32)
    shift, scale, gate = jnp.split(mod, 3, axis=-1)
    h = (rms_norm(x, norm_w) * (1.0 + scale[:, None]) + shift[:, None]).astype(x.dtype)
    proj = (h @ w_in).astype(f32)

    pos = pos0 + jnp.arange(l, dtype=f32)
    q = rotary(proj[..., RET_Q:RET_K].reshape(b, l, RET_HEADS, RET_HEAD_DIM), pos)
    kr = rotary(proj[..., RET_K:RET_V].reshape(b, l, RET_HEADS, RET_HEAD_DIM), pos) * (RET_HEAD_DIM ** -0.5)
    vr = proj[..., RET_V:RET_G].reshape(b, l, RET_HEADS, RET_HEAD_DIM)
    o_ret, s_ret_new = retention(q, kr, vr, s_ret.astype(f32))
    y_ret = group_norm(o_ret, GN_EPS).reshape(b, l, D_RET) * jax.nn.silu(proj[..., RET_G:RW_R])

    cur = shift_cols(proj)
    prev_first = shift_cols((h_prev.astype(x.dtype) @ w_in).astype(f32))[:, None]
    prev = jnp.concatenate([prev_first, cur[:, :-1]], axis=1)
    mix = cur + (prev - cur) * mu_shift.astype(f32)
    r = mix[..., 0:D_RWKV]
    kw = mix[..., D_RWKV:2 * D_RWKV]
    vw = mix[..., 2 * D_RWKV:3 * D_RWKV]
    wd = mix[..., 3 * D_RWKV:3 * D_RWKV + DECAY_LORA]
    ad = mix[..., 3 * D_RWKV + DECAY_LORA:]
    w_log = -jax.nn.softplus(-(w0_decay.astype(f32) + jnp.tanh(wd) @ w2_decay.astype(f32))) - 0.5
    decay = jnp.exp(-jnp.exp(w_log))
    a = jax.nn.sigmoid(a0.astype(f32) + ad @ a2.astype(f32))

    def heads(t):
        return t.reshape(b, l, RWKV_HEADS, RWKV_HEAD_DIM)

    kk = heads(kw * k_k.astype(f32))
    kk = kk / jnp.maximum(jnp.sqrt(jnp.sum(kk * kk, axis=-1, keepdims=True)), 1e-12)
    kw = kw * (1.0 + (a - 1.0) * k_a.astype(f32))
    r_h, k_h, v_h = heads(r), heads(kw), heads(vw)
    o_rw, s_rwkv_new = rwkv7_recurrence(r_h, heads(decay), k_h, v_h, kk, heads(a), s_rwkv.astype(f32))
    o_rw = group_norm(o_rw, RWKV_GN_EPS) * heads(ln_x_w.astype(f32))[0, 0] + heads(ln_x_b.astype(f32))[0, 0] \
        if False else group_norm(o_rw, RWKV_GN_EPS) * ln_x_w.astype(f32).reshape(RWKV_HEADS, RWKV_HEAD_DIM) \
        + ln_x_b.astype(f32).reshape(RWKV_HEADS, RWKV_HEAD_DIM)
    o_rw = o_rw + jnp.sum(r_h * k_h * r_k.astype(f32), axis=-1, keepdims=True) * v_h
    y_rw = o_rw.reshape(b, l, D_RWKV) * jax.nn.silu(proj[..., RW_G:RW_WD])

    y = (jnp.concatenate([y_ret, y_rw], axis=-1).astype(x.dtype) @ w_out).astype(f32)
    x_new = (x.astype(f32) + gate[:, None] * y).astype(x.dtype)
    return x_new, s_ret_new, s_rwkv_new, h[:, -1]


def setup_inputs(seed: int = 0) -> dict:
    key = jax.random.key(seed)
    ks = jax.random.split(key, 24)
    f32 = jnp.float32

    def nrm(k, shape, s):
        return jax.random.normal(k, shape, f32) * s

    ramp = (jnp.arange(D_RWKV, dtype=f32) / (D_RWKV - 1)) ** 0.9
    return {
        'x_prompt': nrm(ks[0], (BATCH, SEQ, D_MODEL), 1.0),
        'x_sample': nrm(ks[1], (DEC_BATCH, DEC_SEQ, D_MODEL), 1.0),
        'c_prompt': nrm(ks[2], (BATCH, D_MODEL), 1.0),
        'c_sample': nrm(ks[3], (DEC_BATCH, D_MODEL), 1.0),
        'state_ret': nrm(ks[4], (DEPTH, DEC_BATCH, RET_HEADS, RET_HEAD_DIM, RET_HEAD_DIM), 0.3),
        'state_rwkv': nrm(ks[5], (DEPTH, DEC_BATCH, RWKV_HEADS, RWKV_HEAD_DIM, RWKV_HEAD_DIM), 0.3),
        'state_shift': nrm(ks[6], (DEPTH, DEC_BATCH, D_MODEL), 1.0),
        'norm_w': 1.0 + nrm(ks[7], (DEPTH, D_MODEL), 0.02),
        'w_ada': nrm(ks[8], (DEPTH, D_MODEL, 3 * D_MODEL), 0.3 * D_MODEL ** -0.5),
        'b_ada': nrm(ks[9], (DEPTH, 3 * D_MODEL), 0.02),
        'w_in': nrm(ks[10], (DEPTH, D_MODEL, N_IN), D_MODEL ** -0.5),
        'mu_shift': jax.random.uniform(ks[11], (DEPTH, N_SHIFT), f32),
        'w0_decay': -5.5 + 5.0 * ramp + nrm(ks[12], (DEPTH, D_RWKV), 0.1),
        'w2_decay': nrm(ks[13], (DEPTH, DECAY_LORA, D_RWKV), 0.5 * DECAY_LORA ** -0.5),
        'a0': nrm(ks[14], (DEPTH, D_RWKV), 0.1),
        'a2': nrm(ks[15], (DEPTH, A_LORA, D_RWKV), A_LORA ** -0.5),
        'k_k': 0.85 + nrm(ks[16], (DEPTH, D_RWKV), 0.02),
        'k_a': 1.0 + nrm(ks[17], (DEPTH, D_RWKV), 0.02),
        'r_k': nrm(ks[18], (DEPTH, RWKV_HEADS, RWKV_HEAD_DIM), 0.1),
        'ln_x_w': 1.0 + nrm(ks[19], (DEPTH, D_RWKV), 0.02),
        'ln_x_b': nrm(ks[20], (DEPTH, D_RWKV), 0.02),
        'w_out': nrm(ks[21], (DEPTH, D_MIX, D_MODEL), D_MIX ** -0.5),
        'final_norm_w': 1.0 + nrm(ks[22], (D_MODEL,), 0.02),
    }


def reference(x_prompt, x_sample, c_prompt, c_sample, state_ret, state_rwkv, state_shift,
              norm_w, w_ada, b_ada, w_in, mu_shift, w0_decay, w2_decay, a0, a2, k_k, k_a,
              r_k, ln_x_w, ln_x_b, w_out, final_norm_w):
    bp = x_prompt.shape[0]
    zero_ret = jnp.zeros((bp, RET_HEADS, RET_HEAD_DIM, RET_HEAD_DIM), jnp.float32)
    zero_rwkv = jnp.zeros((bp, RWKV_HEADS, RWKV_HEAD_DIM, RWKV_HEAD_DIM), jnp.float32)
    zero_shift = jnp.zeros((bp, D_MODEL), x_prompt.dtype)
    hp, hs = x_prompt, x_sample
    ret_p, rwkv_p, shift_p, ret_s, rwkv_s, shift_s = [], [], [], [], [], []
    for layer in range(DEPTH):
        params = (norm_w[layer], w_ada[layer], b_ada[layer], w_in[layer], mu_shift[layer],
                  w0_decay[layer], w2_decay[layer], a0[layer], a2[layer], k_k[layer], k_a[layer],
                  r_k[layer], ln_x_w[layer], ln_x_b[layer], w_out[layer])
        hp, sr, sw, sh = hybrid_layer(hp, c_prompt, zero_ret, zero_rwkv, zero_shift, 0.0, *params)
        ret_p.append(sr.astype(state_ret.dtype))
        rwkv_p.append(sw.astype(state_rwkv.dtype))
        shift_p.append(sh.astype(state_shift.dtype))
        hs, sr, sw, sh = hybrid_layer(hs, c_sample, state_ret[layer], state_rwkv[layer],
                                      state_shift[layer], float(PAST_LEN), *params)
        ret_s.append(sr.astype(state_ret.dtype))
        rwkv_s.append(sw.astype(state_rwkv.dtype))
        shift_s.append(sh.astype(state_shift.dtype))
    y_prompt = rms_norm(hp, final_norm_w).astype(x_prompt.dtype)
    y_sample = rms_norm(hs, final_norm_w).astype(x_sample.dtype)
    return (y_prompt, y_sample, jnp.stack(ret_p), jnp.stack(rwkv_p), jnp.stack(shift_p),
            jnp.stack(ret_s), jnp.stack(rwkv_s), jnp.stack(shift_s))
```

```python
import functools
import math

import jax
import jax.numpy as jnp
from jax import lax
from jax.experimental import pallas as pl
from jax.experimental.pallas import tpu as pltpu

F32 = jnp.float32
BF16 = jnp.bfloat16

RET_HEAD_DIM = 256
RWKV_HEAD_DIM = 64
RET_CHUNK = 128
PAST_LEN = 16384
ROPE_THETA = 10000.0
NORM_EPS = 1e-6
GN_EPS = 1e-5
RWKV_GN_EPS = 64e-5
SAMPLE_SLOTS = 8
LANES = 128
SUBLANES = 8
VMEM_LIMIT_BYTES = 56 * 1024 * 1024


def _params(sem):
    return pltpu.CompilerParams(dimension_semantics=sem, vmem_limit_bytes=VMEM_LIMIT_BYTES)


def _tile(n, cap, align):
    if n <= cap:
        return n
    t = (cap // align) * align
    while t >= align:
        if n % t == 0:
            return t
        t -= align
    return n


def _silu(x):
    return x * jax.nn.sigmoid(x)


def _adaln_body(c_ref, w_ref, b_ref, o_ref):
    s = _silu(c_ref[...]).astype(BF16)
    o_ref[...] = jnp.dot(s, w_ref[...].astype(BF16), preferred_element_type=F32) + b_ref[...]


def _adaln(c, w_ada, b_ada):
    rows, d = c.shape
    n = w_ada.shape[1]
    tn = _tile(n, 768, LANES)
    return pl.pallas_call(
        _adaln_body,
        grid=(n // tn,),
        in_specs=[pl.BlockSpec((rows, d), lambda j: (0, 0)),
                  pl.BlockSpec((d, tn), lambda j: (0, j)),
                  pl.BlockSpec((1, tn), lambda j: (0, j))],
        out_specs=pl.BlockSpec((rows, tn), lambda j: (0, j)),
        out_shape=jax.ShapeDtypeStruct((rows, n), F32),
        compiler_params=_params(("parallel",)),
        name="adaln",
    )(c, w_ada, b_ada.reshape(1, n))


def _modulated(x, nw, shift, scale):
    ms = jnp.mean(x * x, axis=-1, keepdims=True)
    return x * lax.rsqrt(ms + NORM_EPS) * nw * (1.0 + scale) + shift


def _mod_prompt_body(x_ref, mod_ref, nw_ref, h_ref, last_ref):
    m = mod_ref[0]
    h = _modulated(x_ref[0], nw_ref[...], m[0:1], m[1:2])
    h_ref[...] = h.astype(BF16)
    tl = h.shape[0]
    last_ref[0] = h[tl - SUBLANES:tl]


def _mod_prompt(x, mod3, nw):
    b, l, d = x.shape
    tl = _tile(l, 256, SUBLANES)
    return pl.pallas_call(
        _mod_prompt_body,
        grid=(b, l // tl),
        in_specs=[pl.BlockSpec((1, tl, d), lambda bi, i: (bi, i, 0)),
                  pl.BlockSpec((1, 3, d), lambda bi, i: (bi, 0, 0)),
                  pl.BlockSpec((1, d), lambda bi, i: (0, 0))],
        out_specs=[pl.BlockSpec((tl, d), lambda bi, i: (i, bi)),
                   pl.BlockSpec((1, SUBLANES, d), lambda bi, i: (bi, 0, 0))],
        out_shape=[jax.ShapeDtypeStruct((l, b * d), BF16),
                   jax.ShapeDtypeStruct((b, SUBLANES, d), F32)],
        compiler_params=_params(("parallel", "arbitrary")),
        name="modulate_prompt",
    )(x, mod3, nw.reshape(1, d))


def _mod_sample_body(x_ref, prev_ref, mod_ref, nw_ref, h_ref, new_ref):
    x = x_ref[...]
    m = mod_ref[...]
    h = _modulated(x, nw_ref[...], m[:, 0:1, :], m[:, 1:2, :])
    slot = lax.broadcasted_iota(jnp.int32, h.shape, 1)
    hp = prev_ref[...][:, None, :]
    full = jnp.where(slot == 0, hp, h)
    bs, _, d = x.shape
    h_ref[...] = full.reshape(bs * SAMPLE_SLOTS, d).astype(BF16)
    new_ref[...] = h[:, 4, :]


def _mod_sample(x_slots, prev, mod3, nw):
    bsz, _, d = x_slots.shape
    bs = _tile(bsz, 16, SUBLANES)
    return pl.pallas_call(
        _mod_sample_body,
        grid=(bsz // bs,),
        in_specs=[pl.BlockSpec((bs, SAMPLE_SLOTS, d), lambda i: (i, 0, 0)),
                  pl.BlockSpec((bs, d), lambda i: (i, 0)),
                  pl.BlockSpec((bs, 3, d), lambda i: (i, 0, 0)),
                  pl.BlockSpec((1, d), lambda i: (0, 0))],
        out_specs=[pl.BlockSpec((bs * SAMPLE_SLOTS, d), lambda i: (i, 0)),
                   pl.BlockSpec((bs, d), lambda i: (i, 0))],
        out_shape=[jax.ShapeDtypeStruct((bsz * SAMPLE_SLOTS, d), BF16),
                   jax.ShapeDtypeStruct((bsz, d), F32)],
        compiler_params=_params(("parallel",)),
        name="modulate_sample",
    )(x_slots, prev, mod3, nw.reshape(1, d))


def _matmul_body(x_ref, w_ref, o_ref):
    o_ref[...] = jnp.dot(x_ref[...], w_ref[...], preferred_element_type=F32)


def _matmul(x, w, tm_cap=512, tn_cap=1024, name="in_proj"):
    m, k = x.shape
    n = w.shape[1]
    tm = _tile(m, tm_cap, 16)
    tn = _tile(n, tn_cap, LANES)
    return pl.pallas_call(
        _matmul_body,
        grid=(n // tn, m // tm),
        in_specs=[pl.BlockSpec((tm, k), lambda j, i: (i, 0)),
                  pl.BlockSpec((k, tn), lambda j, i: (0, j))],
        out_specs=pl.BlockSpec((tm, tn), lambda j, i: (i, j)),
        out_shape=jax.ShapeDtypeStruct((m, n), F32),
        compiler_params=_params(("parallel", "parallel")),
        name=name,
    )(x, w)


def _rwkv_prep_body(pr_ref, pk_ref, pv_ref, pl_ref, mu_r, mu_k, mu_v, mu_l, w0_ref, w2_ref,
                    a0_ref, a2_ref, kk_ref, ka_ref,
                    r_out, w_out, k_out, v_out, kk_out, a_out,
                    c_r, c_k, c_v, c_l):
    i = pl.program_id(1)

    @pl.when(i == 0)
    def _():
        c_r[...] = jnp.zeros_like(c_r)
        c_k[...] = jnp.zeros_like(c_k)
        c_v[...] = jnp.zeros_like(c_v)
        c_l[...] = jnp.zeros_like(c_l)

    def mixed(cur_ref, carry_ref, mu_ref):
        cur = cur_ref[...]
        tl = cur.shape[0]
        rolled = pltpu.roll(cur, 1, 0)
        row = lax.broadcasted_iota(jnp.int32, cur.shape, 0)
        prev = jnp.where(row == 0, carry_ref[0:1, :], rolled)
        carry_ref[0:1, :] = cur[tl - 1:tl, :]
        return cur + (prev - cur) * mu_ref[...]

    r = mixed(pr_ref, c_r, mu_r)
    kw = mixed(pk_ref, c_k, mu_k)
    vw = mixed(pv_ref, c_v, mu_v)
    lo = mixed(pl_ref, c_l, mu_l)

    dec = jnp.dot(jnp.tanh(lo).astype(BF16), w2_ref[...].astype(BF16), preferred_element_type=F32)
    z = -(w0_ref[...] + dec)
    softplus = jnp.maximum(z, 0.0) + jnp.log1p(jnp.exp(-jnp.abs(z)))
    w_log = -softplus - 0.5
    decay = jnp.exp(-jnp.exp(w_log))
    aa = jnp.dot(lo.astype(BF16), a2_ref[...].astype(BF16), preferred_element_type=F32)
    a = jax.nn.sigmoid(a0_ref[...] + aa)

    r_out[...] = r
    w_out[...] = decay
    k_out[...] = kw * (1.0 + (a - 1.0) * ka_ref[...])
    v_out[...] = vw
    kk_out[...] = kw * kk_ref[...]
    a_out[...] = a


def _rwkv_prep(proj, lora, nb, dr, col0, mu_r, mu_k, mu_v, mu_l, w0, w2p, a0, a2p, k_k, k_a):
    rows = proj.shape[0]
    lp = lora.shape[1] // nb
    sec = proj.shape[1] // nb // dr
    tl = _tile(rows, 128, SUBLANES)
    vec = lambda: pl.BlockSpec((1, dr), lambda b, i: (0, 0))
    sect = lambda s: pl.BlockSpec((tl, dr), lambda b, i, s=s: (i, b * sec + col0 + s))
    out = pl.BlockSpec((tl, dr), lambda b, i: (i, b))
    shape = jax.ShapeDtypeStruct((rows, nb * dr), F32)
    return pl.pallas_call(
        _rwkv_prep_body,
        grid=(nb, rows // tl),
        in_specs=[sect(0), sect(1), sect(2),
                  pl.BlockSpec((tl, lp), lambda b, i: (i, b)),
                  vec(), vec(), vec(),
                  pl.BlockSpec((1, lp), lambda b, i: (0, 0)),
                  vec(),
                  pl.BlockSpec((lp, dr), lambda b, i: (0, 0)),
                  vec(),
                  pl.BlockSpec((lp, dr), lambda b, i: (0, 0)),
                  vec(), vec()],
        out_specs=[out] * 6,
        out_shape=[shape] * 6,
        scratch_shapes=[pltpu.VMEM((SUBLANES, dr), F32), pltpu.VMEM((SUBLANES, dr), F32),
                        pltpu.VMEM((SUBLANES, dr), F32), pltpu.VMEM((SUBLANES, lp), F32)],
        compiler_params=_params(("parallel", "arbitrary")),
        name="rwkv_prep",
    )(proj, proj, proj, lora, mu_r, mu_k, mu_v, mu_l, w0, w2p, a0, a2p, k_k, k_a)


def _rwkv_seq_body(r_ref, w_ref, k_ref, v_ref, kk_ref, a_ref, lnw_ref, lnb_ref, rk_ref, s0_ref,
                   o_ref, so_ref, s_scr, o_scr):
    ti = pl.program_id(1)
    tt_n = r_ref.shape[0]
    hd = r_ref.shape[1]

    @pl.when(ti == 0)
    def _():
        s_scr[...] = s0_ref[...]

    def token(tt, carry):
        rv = r_ref[tt]
        wv = w_ref[tt]
        kv = k_ref[tt]
        kkr = kk_ref[tt]
        nrm = jnp.sqrt(jnp.sum(kkr * kkr, axis=0, keepdims=True))
        kkn = kkr / jnp.maximum(nrm, 1e-12)
        bv = kkn * a_ref[tt]
        nkk = -kkn

        def vblock(vb, c):
            base = pl.multiple_of(vb * SUBLANES, SUBLANES)
            vt = v_ref[tt, pl.ds(base, SUBLANES), :]
            sub = lax.broadcasted_iota(jnp.int32, vt.shape, 0)
            tile = jnp.zeros_like(vt)
            for j in range(SUBLANES):
                sv = s_scr[base + j]
                sa = jnp.sum(sv * nkk, axis=0, keepdims=True)
                sn = sv * wv + sa * bv + vt[j:j + 1, :] * kv
                s_scr[base + j] = sn
                tile = jnp.where(sub == j, jnp.sum(sn * rv, axis=0, keepdims=True), tile)
            o_scr[pl.ds(base, SUBLANES), :] = tile
            return c

        lax.fori_loop(0, hd // SUBLANES, vblock, 0)
        o = o_scr[...]
        mean = jnp.mean(o, axis=0, keepdims=True)
        oc = o - mean
        var = jnp.mean(oc * oc, axis=0, keepdims=True)
        og = oc * lax.rsqrt(var + RWKV_GN_EPS) * lnw_ref[...] + lnb_ref[...]
        bonus = jnp.sum(rv * kv * rk_ref[...], axis=0, keepdims=True) * v_ref[tt]
        o_ref[tt] = og + bonus
        return carry

    lax.fori_loop(0, tt_n, token, 0)

    @pl.when(ti == pl.num_programs(1) - 1)
    def _():
        so_ref[...] = s_scr[...]


def _rwkv_seq(r, w, k, v, kk, a, lnw, lnb, rk, s0):
    t, hd, nc = r.shape
    tt = _tile(t, 32, 1)
    seq = pl.BlockSpec((tt, hd, LANES), lambda c, i: (i, 0, c))
    par = pl.BlockSpec((hd, LANES), lambda c, i: (0, c))
    st = pl.BlockSpec((hd, hd, LANES), lambda c, i: (0, 0, c))
    return pl.pallas_call(
        _rwkv_seq_body,
        grid=(nc // LANES, t // tt),
        in_specs=[seq] * 6 + [par] * 3 + [st],
        out_specs=[seq, st],
        out_shape=[jax.ShapeDtypeStruct((t, hd, nc), F32),
                   jax.ShapeDtypeStruct((hd, hd, nc), F32)],
        scratch_shapes=[pltpu.VMEM((hd, hd, LANES), F32), pltpu.VMEM((hd, LANES), F32)],
        compiler_params=_params(("parallel", "arbitrary")),
        name="rwkv_recurrence",
    )(r, w, k, v, kk, a, lnw, lnb, rk, s0)


def _gate_body(o_ref, g_ref, y_ref):
    y_ref[...] = (o_ref[...] * _silu(g_ref[...])).astype(BF16)


def _gate(o, proj, nb, dr, gcol):
    rows = o.shape[0]
    sec = proj.shape[1] // nb // dr
    tl = _tile(rows, 256, 16)
    return pl.pallas_call(
        _gate_body,
        grid=(nb, rows // tl),
        in_specs=[pl.BlockSpec((tl, dr), lambda b, i: (i, b)),
                  pl.BlockSpec((tl, dr), lambda b, i: (i, b * sec + gcol))],
        out_specs=pl.BlockSpec((tl, dr), lambda b, i: (i, b)),
        out_shape=jax.ShapeDtypeStruct((rows, nb * dr), BF16),
        compiler_params=_params(("parallel", "parallel")),
        name="rwkv_gate",
    )(o, proj)


def _rotary(x, cos, sin):
    half = x.shape[-1] // 2
    x1 = x[:, :half]
    x2 = x[:, half:]
    return jnp.concatenate([x1 * cos - x2 * sin, x1 * sin + x2 * cos], axis=-1)


def _group_norm_rows(o, eps):
    mean = jnp.mean(o, axis=-1, keepdims=True)
    oc = o - mean
    var = jnp.mean(oc * oc, axis=-1, keepdims=True)
    return oc * lax.rsqrt(var + eps)


def _dot_nt(a, b):
    return lax.dot_general(a, b, (((1,), (1,)), ((), ())), preferred_element_type=F32)


def _ret_prompt_body(lg_ref, q_ref, k_ref, v_ref, g_ref, cos_ref, sin_ref, y_ref, s_ref):
    h = pl.program_id(1)
    n = pl.program_id(2)
    c = q_ref.shape[0]
    hd = q_ref.shape[1]
    lg = jnp.full((1, 1), lg_ref[h], F32)

    @pl.when(n == 0)
    def _():
        s_ref[...] = jnp.zeros_like(s_ref)

    cos = cos_ref[...]
    sin = sin_ref[...]
    q = _rotary(q_ref[...], cos, sin)
    k = _rotary(k_ref[...], cos, sin) * (hd ** -0.5)
    vb = v_ref[...].astype(BF16)

    ii = lax.broadcasted_iota(jnp.int32, (c, c), 0)
    jj = lax.broadcasted_iota(jnp.int32, (c, c), 1)
    diff = (ii - jj).astype(F32)
    dmask = jnp.where(diff >= 0.0, jnp.exp(jnp.maximum(diff, 0.0) * lg), 0.0)
    ic = lax.broadcasted_iota(jnp.int32, (c, 1), 0).astype(F32)
    cross_decay = jnp.exp((ic + 1.0) * lg)
    key_decay = jnp.exp((c - 1.0 - ic) * lg)
    chunk_decay = jnp.exp(float(c) * lg)

    s = s_ref[0, 0]
    qb = q.astype(BF16)
    scores = _dot_nt(qb, k.astype(BF16)) * dmask
    inner = jnp.dot(scores.astype(BF16), vb, preferred_element_type=F32)
    cross = jnp.dot(qb, s.astype(BF16), preferred_element_type=F32) * cross_decay
    kd_t = (k * key_decay).T.astype(BF16)
    s_ref[0, 0] = s * chunk_decay + jnp.dot(kd_t, vb, preferred_element_type=F32)

    o = inner + cross
    y_ref[...] = (_group_norm_rows(o, GN_EPS) * _silu(g_ref[...])).astype(BF16)


def _ret_prompt(proj, lg, cos, sin, nb, nh):
    l = proj.shape[0]
    hd = RET_HEAD_DIM
    c = math.gcd(l, RET_CHUNK)
    per_b = proj.shape[1] // nb // hd
    sect = lambda s: pl.BlockSpec((c, hd), lambda b, h, n, s=s: (n, b * per_b + s * nh + h))
    tab = pl.BlockSpec((c, hd // 2), lambda b, h, n: (n, 0))
    return pl.pallas_call(
        _ret_prompt_body,
        grid=(nb, nh, l // c),
        in_specs=[pl.BlockSpec(memory_space=pltpu.SMEM), sect(0), sect(1), sect(2), sect(3), tab, tab],
        out_specs=[pl.BlockSpec((c, hd), lambda b, h, n: (n, b * nh + h)),
                   pl.BlockSpec((1, 1, hd, hd), lambda b, h, n: (b, h, 0, 0))],
        out_shape=[jax.ShapeDtypeStruct((l, nb * nh * hd), BF16),
                   jax.ShapeDtypeStruct((nb, nh, hd, hd), F32)],
        compiler_params=_params(("parallel", "parallel", "arbitrary")),
        name="retention_prompt",
    )(lg, proj, proj, proj, proj, cos, sin)


def _ret_sample_body(lg_ref, q_ref, k_ref, v_ref, g_ref, cos_ref, sin_ref, s0_ref, y_ref, s_ref, *, nh, n_tok):
    hd = RET_HEAD_DIM
    bs = s0_ref.shape[0]
    sl = SAMPLE_SLOTS
    cos = cos_ref[...]
    sin = sin_ref[...]
    slot_r = lax.broadcasted_iota(jnp.int32, (sl, 1), 0)
    tok_r = (slot_r - 1).astype(F32)
    valid_r = (slot_r >= 1) & (slot_r <= n_tok)
    ii = lax.broadcasted_iota(jnp.int32, (sl, sl), 0)
    jj = lax.broadcasted_iota(jnp.int32, (sl, sl), 1)
    diff = (ii - jj).astype(F32)
    pair_ok = (diff >= 0.0) & (jj >= 1) & (jj <= n_tok)
    for h in range(nh):
        lg = jnp.full((1, 1), lg_ref[h], F32)
        dmask = jnp.where(pair_ok, jnp.exp(jnp.maximum(diff, 0.0) * lg), 0.0)
        cross_decay = jnp.exp((tok_r + 1.0) * lg)
        key_decay = jnp.where(valid_r, jnp.exp((n_tok - 1.0 - tok_r) * lg), 0.0)
        chunk_decay = jnp.exp(float(n_tok) * lg)
        for b in range(bs):
            rows = slice(b * sl, (b + 1) * sl)
            cols = slice(h * hd, (h + 1) * hd)
            q = _rotary(q_ref[rows, cols], cos, sin)
            k = _rotary(k_ref[rows, cols], cos, sin) * (hd ** -0.5)
            vb = v_ref[rows, cols].astype(BF16)
            s = s0_ref[b, h]
            qb = q.astype(BF16)
            scores = _dot_nt(qb, k.astype(BF16)) * dmask
            inner = jnp.dot(scores.astype(BF16), vb, preferred_element_type=F32)
            cross = jnp.dot(qb, s.astype(BF16), preferred_element_type=F32) * cross_decay
            kd_t = (k * key_decay).T.astype(BF16)
            s_ref[b, h] = s * chunk_decay + jnp.dot(kd_t, vb, preferred_element_type=F32)
            o = inner + cross
            y_ref[rows, cols] = (_group_norm_rows(o, GN_EPS) * _silu(g_ref[rows, cols])).astype(BF16)


def _ret_sample(proj, lg, cos, sin, s0, nh, n_tok):
    bsz = s0.shape[0]
    hd = RET_HEAD_DIM
    dr = nh * hd
    bs = 2
    rows = bs * SAMPLE_SLOTS
    sect = lambda s: pl.BlockSpec((rows, dr), lambda i, s=s: (i, s))
    tab = pl.BlockSpec((SAMPLE_SLOTS, hd // 2), lambda i: (0, 0))
    st = pl.BlockSpec((bs, nh, hd, hd), lambda i: (i, 0, 0, 0))
    return pl.pallas_call(
        functools.partial(_ret_sample_body, nh=nh, n_tok=n_tok),
        grid=(bsz // bs,),
        in_specs=[pl.BlockSpec(memory_space=pltpu.SMEM), sect(0), sect(1), sect(2), sect(3), tab, tab, st],
        out_specs=[pl.BlockSpec((rows, dr), lambda i: (i, 0)), st],
        out_shape=[jax.ShapeDtypeStruct((bsz * SAMPLE_SLOTS, dr), BF16),
                   jax.ShapeDtypeStruct(s0.shape, F32)],
        compiler_params=_params(("parallel",)),
        name="retention_sample",
    )(lg, proj, proj, proj, proj, cos, sin, s0)


def _out_core(yr_ref, yw_ref, w1_ref, w2_ref, x, gate, fw):
    y = (jnp.dot(yr_ref[...], w1_ref[...], preferred_element_type=F32)
         + jnp.dot(yw_ref[...], w2_ref[...], preferred_element_type=F32))
    xn = x + gate * y
    ms = jnp.mean(xn * xn, axis=-1, keepdims=True)
    return xn * lax.rsqrt(ms + NORM_EPS) * fw


def _out_prompt_body(yr_ref, yw_ref, w1_ref, w2_ref, x_ref, mod_ref, fw_ref, o_ref):
    o_ref[0] = _out_core(yr_ref, yw_ref, w1_ref, w2_ref, x_ref[0], mod_ref[0][2:3], fw_ref[...])


def _out_prompt(y_ret, y_rw, w1, w2, x, mod3, fw):
    b, l, d = x.shape
    dr = w1.shape[0]
    tl = _tile(l, 256, 16)
    wspec = pl.BlockSpec((dr, d), lambda bi, i: (0, 0))
    yspec = pl.BlockSpec((tl, dr), lambda bi, i: (i, bi))
    return pl.pallas_call(
        _out_prompt_body,
        grid=(b, l // tl),
        in_specs=[yspec, yspec, wspec, wspec,
                  pl.BlockSpec((1, tl, d), lambda bi, i: (bi, i, 0)),
                  pl.BlockSpec((1, 3, d), lambda bi, i: (bi, 0, 0)),
                  pl.BlockSpec((1, d), lambda bi, i: (0, 0))],
        out_specs=pl.BlockSpec((1, tl, d), lambda bi, i: (bi, i, 0)),
        out_shape=jax.ShapeDtypeStruct((b, l, d), F32),
        compiler_params=_params(("parallel", "parallel")),
        name="out_proj_prompt",
    )(y_ret, y_rw, w1, w2, x, mod3, fw.reshape(1, d))


def _out_sample_body(yr_ref, yw_ref, w1_ref, w2_ref, x_ref, mod_ref, fw_ref, o_ref):
    bs, sl, d = x_ref.shape
    gate = jnp.broadcast_to(mod_ref[...][:, 2:3, :], (bs, sl, d)).reshape(bs * sl, d)
    x = x_ref[...].reshape(bs * sl, d)
    o = _out_core(yr_ref, yw_ref, w1_ref, w2_ref, x, gate, fw_ref[...])
    o_ref[...] = o.reshape(bs, sl, d)


def _out_sample(y_ret, y_rw, w1, w2, x_slots, mod3, fw):
    bsz, sl, d = x_slots.shape
    dr = w1.shape[0]
    bs = _tile(bsz, 32, SUBLANES)
    wspec = pl.BlockSpec((dr, d), lambda i: (0, 0))
    yspec = pl.BlockSpec((bs * sl, dr), lambda i: (i, 0))
    xspec = pl.BlockSpec((bs, sl, d), lambda i: (i, 0, 0))
    return pl.pallas_call(
        _out_sample_body,
        grid=(bsz // bs,),
        in_specs=[yspec, yspec, wspec, wspec, xspec,
                  pl.BlockSpec((bs, 3, d), lambda i: (i, 0, 0)),
                  pl.BlockSpec((1, d), lambda i: (0, 0))],
        out_specs=xspec,
        out_shape=jax.ShapeDtypeStruct((bsz, sl, d), F32),
        compiler_params=_params(("parallel",)),
        name="out_proj_sample",
    )(y_ret, y_rw, w1, w2, x_slots, mod3, fw.reshape(1, d))


def _pad_chains(x, nc_pad):
    nc = x.shape[-1]
    if nc == nc_pad:
        return x
    return jnp.pad(x, [(0, 0)] * (x.ndim - 1) + [(0, nc_pad - nc)])


def _chain_params(p, nh_rw, nb, nc_pad):
    t = p.reshape(nh_rw, RWKV_HEAD_DIM).T
    return _pad_chains(jnp.tile(t, (1, nb)), nc_pad)


def _rope_tables(pos, half):
    inv_freq = ROPE_THETA ** (-jnp.arange(half, dtype=F32) / half)
    ang = pos[:, None] * inv_freq[None, :]
    return jnp.cos(ang), jnp.sin(ang)


def kernel(x_prompt, x_sample, c_prompt, c_sample, state_ret, state_rwkv, state_shift, norm_w, w_ada,
           b_ada, w_in, mu_shift, w0_decay, w2_decay, a0, a2, k_k, k_a, r_k, ln_x_w, ln_x_b, w_out,
           final_norm_w):
    depth = w_in.shape[0]
    assert depth == 1, "single-layer trunk"
    bp, lp, d = x_prompt.shape
    bsz, ls, _ = x_sample.shape
    assert ls == 4, "sample path packs 4 tokens into slots 1..4"
    dr = d
    nh_ret = dr // RET_HEAD_DIM
    nh_rw = dr // RWKV_HEAD_DIM
    n_main = 8 * dr
    lora = w2_decay.shape[1]
    lora_pad = -(-2 * lora // LANES) * LANES

    w_in0 = w_in[0]
    w_main = w_in0[:, :n_main].astype(BF16)
    w_lora = jnp.pad(w_in0[:, n_main:], ((0, 0), (0, lora_pad - 2 * lora))).astype(BF16)
    w_o = w_out[0].astype(BF16)
    w_o_ret, w_o_rw = w_o[:dr], w_o[dr:]
    mu = mu_shift[0]
    row = lambda p: p.reshape(1, -1)
    mu_r, mu_k, mu_v = row(mu[0:dr]), row(mu[dr:2 * dr]), row(mu[2 * dr:3 * dr])
    mu_l = jnp.pad(mu[3 * dr:], (0, lora_pad - 2 * lora)).reshape(1, lora_pad)
    w2p = jnp.pad(w2_decay[0], ((0, lora_pad - lora), (0, 0)))
    a2p = jnp.pad(a2[0], ((lora, lora_pad - 2 * lora), (0, 0)))
    lg = jnp.log1p(-jnp.exp2(-5.0 - jnp.arange(nh_ret, dtype=F32)))

    n_c = bp + bsz
    n_c_pad = -(-n_c // SUBLANES) * SUBLANES
    c_all = jnp.pad(jnp.concatenate([c_prompt, c_sample], axis=0), ((0, n_c_pad - n_c), (0, 0)))
    mod3 = _adaln(c_all, w_ada[0], b_ada[0]).reshape(n_c_pad, 3, d)
    mod_p, mod_s = mod3[:bp], mod3[bp:bp + bsz]

    def rwkv_branch(proj, lora_proj, nb, to_chain, from_chain, s0_chain, nc):
        nc_pad = -(-nc // LANES) * LANES
        prep = _rwkv_prep(proj, lora_proj, nb, dr, 4, mu_r, mu_k, mu_v, mu_l, row(w0_decay[0]), w2p,
                          row(a0[0]), a2p, row(k_k[0]), row(k_a[0]))
        chain_in = [_pad_chains(to_chain(t), nc_pad) for t in prep]
        reps = nc // nh_rw
        lnw = _chain_params(ln_x_w[0], nh_rw, reps, nc_pad)
        lnb = _chain_params(ln_x_b[0], nh_rw, reps, nc_pad)
        rk = _chain_params(r_k[0].reshape(-1), nh_rw, reps, nc_pad)
        o_c, s_c = _rwkv_seq(*chain_in, lnw, lnb, rk, _pad_chains(s0_chain, nc_pad))
        o_nat = from_chain(o_c[..., :nc])
        return _gate(o_nat, proj, nb, dr, 7), s_c[..., :nc]

    h_p, last_p = _mod_prompt(x_prompt, mod_p, norm_w[0])
    new_shift_p = last_p[:, SUBLANES - 1, :]
    hp2 = h_p.reshape(lp * bp, d)
    proj_p = _matmul(hp2, w_main).reshape(lp, bp * n_main)
    lora_p = _matmul(hp2, w_lora, name="in_proj_lora").reshape(lp, bp * lora_pad)

    cos_p, sin_p = _rope_tables(jnp.arange(lp, dtype=F32), RET_HEAD_DIM // 2)
    y_ret_p, s_ret_p = _ret_prompt(proj_p, lg, cos_p, sin_p, bp, nh_ret)

    nc_p = bp * nh_rw
    to_chain_p = lambda t: jnp.swapaxes(t.reshape(lp, nc_p, RWKV_HEAD_DIM), 1, 2)
    from_chain_p = lambda o: jnp.swapaxes(o, 1, 2).reshape(lp, bp * dr)
    s0_p = jnp.zeros((RWKV_HEAD_DIM, RWKV_HEAD_DIM, nc_p), F32)
    y_rw_p, s_rw_p = rwkv_branch(proj_p, lora_p, bp, to_chain_p, from_chain_p, s0_p, nc_p)
    s_rw_p = jnp.transpose(s_rw_p, (2, 0, 1)).reshape(bp, nh_rw, RWKV_HEAD_DIM, RWKV_HEAD_DIM)

    y_prompt = _out_prompt(y_ret_p, y_rw_p, w_o_ret, w_o_rw, x_prompt, mod_p, final_norm_w)

    sl = SAMPLE_SLOTS
    x_slots = jnp.pad(x_sample, ((0, 0), (1, sl - 1 - ls), (0, 0)))
    h_s, new_shift_s = _mod_sample(x_slots, state_shift[0], mod_s, norm_w[0])
    proj_s = _matmul(h_s, w_main, tm_cap=1024)
    lora_s = _matmul(h_s, w_lora, tm_cap=1024, name="in_proj_lora")

    slot_pos = jnp.arange(sl, dtype=F32) - 1.0
    pos_s = jnp.where((slot_pos >= 0) & (slot_pos < ls), float(PAST_LEN) + slot_pos, 0.0)
    cos_s, sin_s = _rope_tables(pos_s, RET_HEAD_DIM // 2)
    y_ret_s, s_ret_s = _ret_sample(proj_s, lg, cos_s, sin_s, state_ret[0], nh_ret, ls)

    nc_s = bsz * nh_rw

    def to_chain_s(t):
        t4 = t.reshape(bsz, sl, nh_rw, RWKV_HEAD_DIM)[:, 1:1 + ls]
        return jnp.transpose(t4, (1, 3, 0, 2)).reshape(ls, RWKV_HEAD_DIM, nc_s)

    def from_chain_s(o):
        o4 = jnp.transpose(o.reshape(ls, RWKV_HEAD_DIM, bsz, nh_rw), (2, 0, 3, 1)).reshape(bsz, ls, dr)
        return jnp.pad(o4, ((0, 0), (1, sl - 1 - ls), (0, 0))).reshape(bsz * sl, dr)

    s0_s = jnp.transpose(state_rwkv[0], (2, 3, 0, 1)).reshape(RWKV_HEAD_DIM, RWKV_HEAD_DIM, nc_s)
    y_rw_s, s_rw_s = rwkv_branch(proj_s, lora_s, 1, to_chain_s, from_chain_s, s0_s, nc_s)
    s_rw_s = jnp.transpose(s_rw_s.reshape(RWKV_HEAD_DIM, RWKV_HEAD_DIM, bsz, nh_rw), (2, 3, 0, 1))

    y_slots = _out_sample(y_ret_s, y_rw_s, w_o_ret, w_o_rw, x_slots, mod_s, final_norm_w)
    y_sample = y_slots[:, 1:1 + ls, :]

    return (y_prompt, y_sample, s_ret_p[None], s_rw_p[None], new_shift_p[None],
            s_ret_s[None], s_rw_s[None], new_shift_s[None])
```

```python
import functools
import math

import jax
import jax.numpy as jnp
from jax import lax
from jax.experimental import pallas as pl
from jax.experimental.pallas import tpu as pltpu

F32 = jnp.float32
BF16 = jnp.bfloat16

RET_HEAD_DIM = 256
RWKV_HEAD_DIM = 64
RET_CHUNK = 128
RWKV_CHUNK = 64
RWKV_GROUP = 4
PAST_LEN = 16384
ROPE_THETA = 10000.0
NORM_EPS = 1e-6
GN_EPS = 1e-5
RWKV_GN_EPS = 64e-5
SAMPLE_SLOTS = 8
LANES = 128
SUBLANES = 8
VMEM_LIMIT_BYTES = 56 * 1024 * 1024


def _params(sem):
    return pltpu.CompilerParams(dimension_semantics=sem, vmem_limit_bytes=VMEM_LIMIT_BYTES)


def _tile(n, cap, align):
    if n <= cap:
        return n
    t = (cap // align) * align
    while t >= align:
        if n % t == 0:
            return t
        t -= align
    return n


def _silu(x):
    return x * jax.nn.sigmoid(x)


def _dot(a, b):
    return jnp.dot(a, b, preferred_element_type=F32)


def _dot_nt(a, b):
    return lax.dot_general(a, b, (((1,), (1,)), ((), ())), preferred_element_type=F32)


def _dot_tn(a, b):
    return lax.dot_general(a, b, (((0,), (0,)), ((), ())), preferred_element_type=F32)


def _adaln_body(c_ref, w_ref, b_ref, o_ref):
    s = _silu(c_ref[...]).astype(BF16)
    o_ref[...] = _dot(s, w_ref[...].astype(BF16)) + b_ref[...]


def _adaln(c, w_ada, b_ada):
    rows, d = c.shape
    n = w_ada.shape[1]
    tn = _tile(n, 768, LANES)
    return pl.pallas_call(
        _adaln_body,
        grid=(n // tn,),
        in_specs=[pl.BlockSpec((rows, d), lambda j: (0, 0)),
                  pl.BlockSpec((d, tn), lambda j: (0, j)),
                  pl.BlockSpec((1, tn), lambda j: (0, j))],
        out_specs=pl.BlockSpec((rows, tn), lambda j: (0, j)),
        out_shape=jax.ShapeDtypeStruct((rows, n), F32),
        compiler_params=_params(("parallel",)),
        name="adaln",
    )(c, w_ada, b_ada.reshape(1, n))


def _modulated(x, nw, shift, scale):
    ms = jnp.mean(x * x, axis=-1, keepdims=True)
    return x * lax.rsqrt(ms + NORM_EPS) * nw * (1.0 + scale) + shift


def _mod_prompt_body(x_ref, mod_ref, nw_ref, h_ref, last_ref):
    m = mod_ref[0]
    h = _modulated(x_ref[0], nw_ref[...], m[0:1], m[1:2])
    h_ref[...] = h.astype(BF16)
    tl = h.shape[0]
    last_ref[0] = h[tl - SUBLANES:tl]


def _mod_prompt(x, mod3, nw):
    b, l, d = x.shape
    tl = _tile(l, 256, 16)
    nt = l // tl
    return pl.pallas_call(
        _mod_prompt_body,
        grid=(b, nt),
        in_specs=[pl.BlockSpec((1, tl, d), lambda bi, i: (bi, i, 0)),
                  pl.BlockSpec((1, 3, d), lambda bi, i: (bi, 0, 0)),
                  pl.BlockSpec((1, d), lambda bi, i: (0, 0))],
        out_specs=[pl.BlockSpec((tl, d), lambda bi, i: (bi * nt + i, 0)),
                   pl.BlockSpec((1, SUBLANES, d), lambda bi, i: (bi, 0, 0))],
        out_shape=[jax.ShapeDtypeStruct((b * l, d), BF16),
                   jax.ShapeDtypeStruct((b, SUBLANES, d), F32)],
        compiler_params=_params(("parallel", "arbitrary")),
        name="modulate_prompt",
    )(x, mod3, nw.reshape(1, d))


def _mod_sample_body(x_ref, prev_ref, mod_ref, nw_ref, h_ref, new_ref):
    x = x_ref[...]
    m = mod_ref[...]
    h = _modulated(x, nw_ref[...], m[:, 0:1, :], m[:, 1:2, :])
    slot = lax.broadcasted_iota(jnp.int32, h.shape, 1)
    hp = prev_ref[...][:, None, :]
    full = jnp.where(slot == 0, hp, h)
    bs, _, d = x.shape
    h_ref[...] = full.reshape(bs * SAMPLE_SLOTS, d).astype(BF16)
    new_ref[...] = h[:, 4, :]


def _mod_sample(x_slots, prev, mod3, nw):
    bsz, _, d = x_slots.shape
    bs = _tile(bsz, 16, SUBLANES)
    return pl.pallas_call(
        _mod_sample_body,
        grid=(bsz // bs,),
        in_specs=[pl.BlockSpec((bs, SAMPLE_SLOTS, d), lambda i: (i, 0, 0)),
                  pl.BlockSpec((bs, d), lambda i: (i, 0)),
                  pl.BlockSpec((bs, 3, d), lambda i: (i, 0, 0)),
                  pl.BlockSpec((1, d), lambda i: (0, 0))],
        out_specs=[pl.BlockSpec((bs * SAMPLE_SLOTS, d), lambda i: (i, 0)),
                   pl.BlockSpec((bs, d), lambda i: (i, 0))],
        out_shape=[jax.ShapeDtypeStruct((bsz * SAMPLE_SLOTS, d), BF16),
                   jax.ShapeDtypeStruct((bsz, d), F32)],
        compiler_params=_params(("parallel",)),
        name="modulate_sample",
    )(x_slots, prev, mod3, nw.reshape(1, d))


def _matmul_body(x_ref, w_ref, o_ref):
    o_ref[...] = _dot(x_ref[...], w_ref[...])


def _matmul(x, w, tm_cap=512, tn_cap=1024, name="in_proj"):
    m, k = x.shape
    n = w.shape[1]
    tm = _tile(m, tm_cap, 16)
    tn = _tile(n, tn_cap, LANES)
    return pl.pallas_call(
        _matmul_body,
        grid=(n // tn, m // tm),
        in_specs=[pl.BlockSpec((tm, k), lambda j, i: (i, 0)),
                  pl.BlockSpec((k, tn), lambda j, i: (0, j))],
        out_specs=pl.BlockSpec((tm, tn), lambda j, i: (i, j)),
        out_shape=jax.ShapeDtypeStruct((m, n), F32),
        compiler_params=_params(("parallel", "parallel")),
        name=name,
    )(x, w)


def _token_shift(cur_ref, carry_ref, mu_ref):
    cur = cur_ref[...]
    tl = cur.shape[0]
    rolled = pltpu.roll(cur, 1, 0)
    row = lax.broadcasted_iota(jnp.int32, cur.shape, 0)
    prev = jnp.where(row == 0, carry_ref[0:1, :], rolled)
    carry_ref[0:1, :] = cur[tl - 1:tl, :]
    return cur + (prev - cur) * mu_ref[...]


def _decay_and_rate(lo, w0_ref, w2_ref, a0_ref, a2_ref):
    dec = _dot(jnp.tanh(lo).astype(BF16), w2_ref[...])
    z = -(w0_ref[...] + dec)
    softplus = jnp.maximum(z, 0.0) + jnp.log1p(jnp.exp(-jnp.abs(z)))
    w_log = -softplus - 0.5
    a = jax.nn.sigmoid(a0_ref[...] + _dot(lo.astype(BF16), a2_ref[...]))
    return -jnp.exp(w_log), a


def _zero_refs(*refs):
    for r in refs:
        r[...] = jnp.zeros_like(r)


def _rwkv_prep_body(pr_ref, pk_ref, pv_ref, pl_ref, mu_r, mu_k, mu_v, mu_l, w0_ref, w2_ref,
                    a0_ref, a2_ref, kk_ref, ka_ref,
                    r_out, w_out, k_out, v_out, kk_out, a_out,
                    c_r, c_k, c_v, c_l):
    @pl.when(pl.program_id(0) == 0)
    def _():
        _zero_refs(c_r, c_k, c_v, c_l)

    r = _token_shift(pr_ref, c_r, mu_r)
    kw = _token_shift(pk_ref, c_k, mu_k)
    vw = _token_shift(pv_ref, c_v, mu_v)
    lo = _token_shift(pl_ref, c_l, mu_l)
    logw, a = _decay_and_rate(lo, w0_ref, w2_ref, a0_ref, a2_ref)
    r_out[...] = r
    w_out[...] = jnp.exp(logw)
    k_out[...] = kw * (1.0 + (a - 1.0) * ka_ref[...])
    v_out[...] = vw
    kk_out[...] = kw * kk_ref[...]
    a_out[...] = a


def _rwkv_prep(proj, lora, dr, col0, mu_r, mu_k, mu_v, mu_l, w0, w2p, a0, a2p, k_k, k_a):
    rows = proj.shape[0]
    lp = lora.shape[1]
    tl = _tile(rows, 128, SUBLANES)
    vec = lambda: pl.BlockSpec((1, dr), lambda i: (0, 0))
    sect = lambda s: pl.BlockSpec((tl, dr), lambda i, s=s: (i, col0 + s))
    out = pl.BlockSpec((tl, dr), lambda i: (i, 0))
    shape = jax.ShapeDtypeStruct((rows, dr), F32)
    return pl.pallas_call(
        _rwkv_prep_body,
        grid=(rows // tl,),
        in_specs=[sect(0), sect(1), sect(2),
                  pl.BlockSpec((tl, lp), lambda i: (i, 0)),
                  vec(), vec(), vec(),
                  pl.BlockSpec((1, lp), lambda i: (0, 0)),
                  vec(),
                  pl.BlockSpec((lp, dr), lambda i: (0, 0)),
                  vec(),
                  pl.BlockSpec((lp, dr), lambda i: (0, 0)),
                  vec(), vec()],
        out_specs=[out] * 6,
        out_shape=[shape] * 6,
        scratch_shapes=[pltpu.VMEM((SUBLANES, dr), F32), pltpu.VMEM((SUBLANES, dr), F32),
                        pltpu.VMEM((SUBLANES, dr), F32), pltpu.VMEM((SUBLANES, lp), F32)],
        compiler_params=_params(("arbitrary",)),
        name="rwkv_prep",
    )(proj, proj, proj, lora, mu_r, mu_k, mu_v, mu_l, w0, w2p, a0, a2p, k_k, k_a)


def _rwkv_seq_body(r_ref, w_ref, k_ref, v_ref, kk_ref, a_ref, lnw_ref, lnb_ref, rk_ref, s0_ref,
                   o_ref, so_ref, s_scr, o_scr):
    ti = pl.program_id(1)
    tt_n = r_ref.shape[0]
    hd = r_ref.shape[1]

    @pl.when(ti == 0)
    def _():
        s_scr[...] = s0_ref[...]

    def token(tt, carry):
        rv = r_ref[tt]
        wv = w_ref[tt]
        kv = k_ref[tt]
        kkr = kk_ref[tt]
        nrm = jnp.sqrt(jnp.sum(kkr * kkr, axis=0, keepdims=True))
        kkn = kkr / jnp.maximum(nrm, 1e-12)
        bv = kkn * a_ref[tt]
        nkk = -kkn

        def vblock(vb, c):
            base = pl.multiple_of(vb * SUBLANES, SUBLANES)
            vt = v_ref[tt, pl.ds(base, SUBLANES), :]
            sub = lax.broadcasted_iota(jnp.int32, vt.shape, 0)
            tile = jnp.zeros_like(vt)
            for j in range(SUBLANES):
                sv = s_scr[base + j]
                sa = jnp.sum(sv * nkk, axis=0, keepdims=True)
                sn = sv * wv + sa * bv + vt[j:j + 1, :] * kv
                s_scr[base + j] = sn
                tile = jnp.where(sub == j, jnp.sum(sn * rv, axis=0, keepdims=True), tile)
            o_scr[pl.ds(base, SUBLANES), :] = tile
            return c

        lax.fori_loop(0, hd // SUBLANES, vblock, 0)
        o = o_scr[...]
        mean = jnp.mean(o, axis=0, keepdims=True)
        oc = o - mean
        var = jnp.mean(oc * oc, axis=0, keepdims=True)
        og = oc * lax.rsqrt(var + RWKV_GN_EPS) * lnw_ref[...] + lnb_ref[...]
        bonus = jnp.sum(rv * kv * rk_ref[...], axis=0, keepdims=True) * v_ref[tt]
        o_ref[tt] = og + bonus
        return carry

    lax.fori_loop(0, tt_n, token, 0)

    @pl.when(ti == pl.num_programs(1) - 1)
    def _():
        so_ref[...] = s_scr[...]


def _rwkv_seq(r, w, k, v, kk, a, lnw, lnb, rk, s0):
    t, hd, nc = r.shape
    tt = _tile(t, 32, 1)
    seq = pl.BlockSpec((tt, hd, LANES), lambda c, i: (i, 0, c))
    par = pl.BlockSpec((hd, LANES), lambda c, i: (0, c))
    st = pl.BlockSpec((hd, hd, LANES), lambda c, i: (0, 0, c))
    return pl.pallas_call(
        _rwkv_seq_body,
        grid=(nc // LANES, t // tt),
        in_specs=[seq] * 6 + [par] * 3 + [st],
        out_specs=[seq, st],
        out_shape=[jax.ShapeDtypeStruct((t, hd, nc), F32),
                   jax.ShapeDtypeStruct((hd, hd, nc), F32)],
        scratch_shapes=[pltpu.VMEM((hd, hd, LANES), F32), pltpu.VMEM((hd, LANES), F32)],
        compiler_params=_params(("parallel", "arbitrary")),
        name="rwkv_recurrence",
    )(r, w, k, v, kk, a, lnw, lnb, rk, s0)


def _gate_body(o_ref, g_ref, y_ref):
    y_ref[...] = (o_ref[...] * _silu(g_ref[...])).astype(BF16)


def _gate(o, proj, dr, gcol):
    rows = o.shape[0]
    tl = _tile(rows, 256, 16)
    return pl.pallas_call(
        _gate_body,
        grid=(rows // tl,),
        in_specs=[pl.BlockSpec((tl, dr), lambda i: (i, 0)),
                  pl.BlockSpec((tl, dr), lambda i: (i, gcol))],
        out_specs=pl.BlockSpec((tl, dr), lambda i: (i, 0)),
        out_shape=jax.ShapeDtypeStruct((rows, dr), BF16),
        compiler_params=_params(("parallel",)),
        name="rwkv_gate",
    )(o, proj)


def _split3(x):
    hi = x.astype(BF16)
    r1 = x - hi.astype(F32)
    mid = r1.astype(BF16)
    lo = (r1 - mid.astype(F32)).astype(BF16)
    return hi, mid, lo


def _rwkv_chunk_body(pr_ref, pk_ref, pv_ref, pg_ref, pl_ref, mu_r, mu_k, mu_v, mu_l, w0_ref, w2_ref,
                     a0_ref, a2_ref, kk_ref, ka_ref, rk_ref, lnw_ref, lnb_ref,
                     y_ref, so_ref, c_r, c_k, c_v, c_l, s_scr):
    n = pl.program_id(1)
    c = pr_ref.shape[0]
    gw = RWKV_GROUP * RWKV_HEAD_DIM
    n_groups = pr_ref.shape[1] // gw
    hd = RWKV_HEAD_DIM

    @pl.when(n == 0)
    def _():
        _zero_refs(c_r, c_k, c_v, c_l, s_scr)

    r_all = _token_shift(pr_ref, c_r, mu_r)
    kw_all = _token_shift(pk_ref, c_k, mu_k)
    v_all = _token_shift(pv_ref, c_v, mu_v)
    lo = _token_shift(pl_ref, c_l, mu_l)
    logw_all, a_all = _decay_and_rate(lo, w0_ref, w2_ref, a0_ref, a2_ref)
    k_all = kw_all * (1.0 + (a_all - 1.0) * ka_ref[...])
    kkr_all = kw_all * kk_ref[...]

    ti = lax.broadcasted_iota(jnp.int32, (c, c), 0)
    tj = lax.broadcasted_iota(jnp.int32, (c, c), 1)
    ltri = (ti >= tj).astype(BF16)
    hi, mid, low = _split3(logw_all)
    cum_all = _dot(ltri, hi) + _dot(ltri, mid) + _dot(ltri, low)

    hr = lax.broadcasted_iota(jnp.int32, (gw, gw), 0) // hd
    hc = lax.broadcasted_iota(jnp.int32, (gw, gw), 1) // hd
    headmask = hr == hc
    ones_bd = headmask.astype(BF16)
    tok = lax.broadcasted_iota(jnp.int32, (c, gw), 0)
    src = lax.broadcasted_iota(jnp.int32, (c, gw), 1) % hd
    strict = tok > src
    incl = tok >= src

    def bd(x):
        t = jnp.concatenate([x] * RWKV_GROUP, axis=0)
        return jnp.where(headmask, t, 0.0).astype(BF16)

    def head_sum(x):
        xh = x.astype(BF16)
        xl = (x - xh.astype(F32)).astype(BF16)
        return _dot(xh, ones_bd) + _dot(xl, ones_bd)

    groups = range(n_groups)
    sls = [slice(g * gw, (g + 1) * gw) for g in groups]
    r = [r_all[:, sl] for sl in sls]
    k = [k_all[:, sl] for sl in sls]
    v = [v_all[:, sl] for sl in sls]
    cum = [cum_all[:, sl] for sl in sls]
    cum_end = [cm[c - 1:c, :] for cm in cum]

    kk_sq = [head_sum(jnp.square(kkr_all[:, sl])) for sl in sls]
    rk_sum = [head_sum(r[g] * k[g] * rk_ref[:, sls[g]]) for g in groups]
    kkn = [kkr_all[:, sls[g]] / jnp.maximum(jnp.sqrt(kk_sq[g]), 1e-12) for g in groups]
    b = [kkn[g] * a_all[:, sls[g]] for g in groups]
    e_inv = [jnp.exp(-cum[g]) for g in groups]
    ar = [jnp.concatenate([-kkn[g] * jnp.exp(cum[g] - logw_all[:, sls[g]]), r[g] * jnp.exp(cum[g])],
                          axis=0).astype(BF16) for g in groups]
    s0 = [s_scr[g] for g in groups]
    from_state = [_dot_nt(ar[g], s0[g].astype(BF16)) for g in groups]
    s_b = [_dot_nt(ar[g], bd(b[g] * e_inv[g])) for g in groups]
    s_k = [_dot_nt(ar[g], bd(k[g] * e_inv[g])) for g in groups]
    a_ab = [jnp.where(strict, x[:c], 0.0) for x in s_b]
    a_rb = [jnp.where(incl, x[c:], 0.0) for x in s_b]
    a_ak = [jnp.where(strict, x[:c], 0.0) for x in s_k]
    a_rk = [jnp.where(incl, x[c:], 0.0) for x in s_k]

    steps = max(1, int(math.ceil(math.log2(c))))
    q = a_ab
    pm = a_ab
    q_next = [_dot(q[g].astype(BF16), bd(q[g])) for g in groups]
    for s in range(1, steps):
        q = q_next
        if s < steps - 1:
            res = [_dot(jnp.concatenate([q[g], pm[g]], axis=0).astype(BF16), bd(q[g])) for g in groups]
            q_next = [x[:c] for x in res]
            pm = [pm[g] + q[g] + res[g][c:] for g in groups]
        else:
            pm = [pm[g] + q[g] + _dot(pm[g].astype(BF16), bd(q[g])) for g in groups]

    bd_v = [bd(x) for x in v]
    w_rhs = [from_state[g][:c] + _dot(a_ak[g].astype(BF16), bd_v[g]) for g in groups]
    u = [w_rhs[g] + _dot(pm[g].astype(BF16), bd(w_rhs[g])) for g in groups]
    o = [from_state[g][c:] + _dot(jnp.concatenate([a_rb[g], a_rk[g]], axis=1).astype(BF16),
                                  jnp.concatenate([bd(u[g]), bd_v[g]], axis=0)) for g in groups]
    for g in groups:
        e_end = jnp.exp(cum_end[g] - cum[g])
        uv = jnp.concatenate([u[g], v[g]], axis=0).astype(BF16)
        bk = jnp.concatenate([b[g] * e_end, k[g] * e_end], axis=0).astype(BF16)
        s_new = s0[g] * jnp.exp(cum_end[g]) + _dot_tn(uv, bk)
        s_scr[g] = jnp.where(headmask, s_new, 0.0)

    mean = [head_sum(x) * (1.0 / hd) for x in o]
    oc = [o[g] - mean[g] for g in groups]
    var = [head_sum(jnp.square(x)) * (1.0 / hd) for x in oc]
    for g in groups:
        sl = sls[g]
        og = oc[g] * lax.rsqrt(var[g] + RWKV_GN_EPS) * lnw_ref[:, sl] + lnb_ref[:, sl]
        y_ref[:, sl] = ((og + rk_sum[g] * v[g]) * _silu(pg_ref[:, sl])).astype(BF16)

    @pl.when(n == pl.num_programs(1) - 1)
    def _():
        for g in range(n_groups):
            sg = s_scr[g]
            for j in range(RWKV_GROUP):
                so_ref[0, g * RWKV_GROUP + j] = sg[j * hd:(j + 1) * hd, j * hd:(j + 1) * hd]


def _rwkv_chunk(proj, lora, nb, l, dr, mu_r, mu_k, mu_v, mu_l, w0, w2p, a0, a2p, k_k, k_a, r_k, ln_w, ln_b):
    c = RWKV_CHUNK
    assert l % c == 0 and c == RWKV_HEAD_DIM
    nchunk = l // c
    lp = lora.shape[1]
    nh = dr // RWKV_HEAD_DIM
    gw = RWKV_GROUP * RWKV_HEAD_DIM
    vec = lambda: pl.BlockSpec((1, dr), lambda b, n: (0, 0))
    sect = lambda s: pl.BlockSpec((c, dr), lambda b, n, s=s: (b * nchunk + n, 4 + s))
    return pl.pallas_call(
        _rwkv_chunk_body,
        grid=(nb, nchunk),
        in_specs=[sect(0), sect(1), sect(2), sect(3),
                  pl.BlockSpec((c, lp), lambda b, n: (b * nchunk + n, 0)),
                  vec(), vec(), vec(),
                  pl.BlockSpec((1, lp), lambda b, n: (0, 0)),
                  vec(),
                  pl.BlockSpec((lp, dr), lambda b, n: (0, 0)),
                  vec(),
                  pl.BlockSpec((lp, dr), lambda b, n: (0, 0)),
                  vec(), vec(), vec(), vec(), vec()],
        out_specs=[pl.BlockSpec((c, dr), lambda b, n: (b * nchunk + n, 0)),
                   pl.BlockSpec((1, nh, RWKV_HEAD_DIM, RWKV_HEAD_DIM), lambda b, n: (b, 0, 0, 0))],
        out_shape=[jax.ShapeDtypeStruct((nb * l, dr), BF16),
                   jax.ShapeDtypeStruct((nb, nh, RWKV_HEAD_DIM, RWKV_HEAD_DIM), F32)],
        scratch_shapes=[pltpu.VMEM((SUBLANES, dr), F32), pltpu.VMEM((SUBLANES, dr), F32),
                        pltpu.VMEM((SUBLANES, dr), F32), pltpu.VMEM((SUBLANES, lp), F32),
                        pltpu.VMEM((dr // gw, gw, gw), F32)],
        compiler_params=_params(("parallel", "arbitrary")),
        name="rwkv_chunked",
    )(proj, proj, proj, proj, lora, mu_r, mu_k, mu_v, mu_l, w0, w2p, a0, a2p, k_k, k_a, r_k, ln_w, ln_b)


def _rotary(x, cos, sin):
    half = x.shape[-1] // 2
    x1 = x[:, :half]
    x2 = x[:, half:]
    return jnp.concatenate([x1 * cos - x2 * sin, x1 * sin + x2 * cos], axis=-1)


def _group_norm_rows(o, eps):
    mean = jnp.mean(o, axis=-1, keepdims=True)
    oc = o - mean
    var = jnp.mean(oc * oc, axis=-1, keepdims=True)
    return oc * lax.rsqrt(var + eps)


def _ret_prompt_body(lg_ref, q_ref, k_ref, v_ref, g_ref, cos_ref, sin_ref, y_ref, s_ref):
    h = pl.program_id(1)
    n = pl.program_id(2)
    c = q_ref.shape[0]
    hd = q_ref.shape[1]
    lg = jnp.full((1, 1), lg_ref[h], F32)

    @pl.when(n == 0)
    def _():
        s_ref[...] = jnp.zeros_like(s_ref)

    cos = cos_ref[...]
    sin = sin_ref[...]
    q = _rotary(q_ref[...], cos, sin)
    k = _rotary(k_ref[...], cos, sin) * (hd ** -0.5)
    vb = v_ref[...].astype(BF16)

    ii = lax.broadcasted_iota(jnp.int32, (c, c), 0)
    jj = lax.broadcasted_iota(jnp.int32, (c, c), 1)
    diff = (ii - jj).astype(F32)
    dmask = jnp.where(diff >= 0.0, jnp.exp(jnp.maximum(diff, 0.0) * lg), 0.0)
    ic = lax.broadcasted_iota(jnp.int32, (c, 1), 0).astype(F32)
    cross_decay = jnp.exp((ic + 1.0) * lg)
    key_decay = jnp.exp((c - 1.0 - ic) * lg)
    chunk_decay = jnp.exp(float(c) * lg)

    s = s_ref[0, 0]
    qb = q.astype(BF16)
    scores = _dot_nt(qb, k.astype(BF16)) * dmask
    inner = _dot(scores.astype(BF16), vb)
    cross = _dot(qb, s.astype(BF16)) * cross_decay
    s_ref[0, 0] = s * chunk_decay + _dot_tn((k * key_decay).astype(BF16), vb)

    o = inner + cross
    y_ref[...] = (_group_norm_rows(o, GN_EPS) * _silu(g_ref[...])).astype(BF16)


def _ret_prompt(proj, lg, cos, sin, nb, l, nh):
    hd = RET_HEAD_DIM
    c = math.gcd(l, RET_CHUNK)
    nchunk = l // c
    sect = lambda s: pl.BlockSpec((c, hd), lambda b, h, n, s=s: (b * nchunk + n, s * nh + h))
    tab = pl.BlockSpec((c, hd // 2), lambda b, h, n: (n, 0))
    return pl.pallas_call(
        _ret_prompt_body,
        grid=(nb, nh, nchunk),
        in_specs=[pl.BlockSpec(memory_space=pltpu.SMEM), sect(0), sect(1), sect(2), sect(3), tab, tab],
        out_specs=[pl.BlockSpec((c, hd), lambda b, h, n: (b * nchunk + n, h)),
                   pl.BlockSpec((1, 1, hd, hd), lambda b, h, n: (b, h, 0, 0))],
        out_shape=[jax.ShapeDtypeStruct((nb * l, nh * hd), BF16),
                   jax.ShapeDtypeStruct((nb, nh, hd, hd), F32)],
        compiler_params=_params(("parallel", "parallel", "arbitrary")),
        name="retention_prompt",
    )(lg, proj, proj, proj, proj, cos, sin)


def _ret_sample_body(lg_ref, q_ref, k_ref, v_ref, g_ref, cos_ref, sin_ref, s0_ref, y_ref, s_ref, *, nh, n_tok):
    hd = RET_HEAD_DIM
    bs = s0_ref.shape[0]
    sl = SAMPLE_SLOTS
    cos = cos_ref[...]
    sin = sin_ref[...]
    slot_r = lax.broadcasted_iota(jnp.int32, (sl, 1), 0)
    tok_r = (slot_r - 1).astype(F32)
    valid_r = (slot_r >= 1) & (slot_r <= n_tok)
    ii = lax.broadcasted_iota(jnp.int32, (sl, sl), 0)
    jj = lax.broadcasted_iota(jnp.int32, (sl, sl), 1)
    diff = (ii - jj).astype(F32)
    pair_ok = (diff >= 0.0) & (jj >= 1) & (jj <= n_tok)
    for h in range(nh):
        lg = jnp.full((1, 1), lg_ref[h], F32)
        dmask = jnp.where(pair_ok, jnp.exp(jnp.maximum(diff, 0.0) * lg), 0.0)
        cross_decay = jnp.exp((tok_r + 1.0) * lg)
        key_decay = jnp.where(valid_r, jnp.exp((n_tok - 1.0 - tok_r) * lg), 0.0)
        chunk_decay = jnp.exp(float(n_tok) * lg)
        for b in range(bs):
            rows = slice(b * sl, (b + 1) * sl)
            cols = slice(h * hd, (h + 1) * hd)
            q = _rotary(q_ref[rows, cols], cos, sin)
            k = _rotary(k_ref[rows, cols], cos, sin) * (hd ** -0.5)
            vb = v_ref[rows, cols].astype(BF16)
            s = s0_ref[b, h]
            qb = q.astype(BF16)
            scores = _dot_nt(qb, k.astype(BF16)) * dmask
            inner = _dot(scores.astype(BF16), vb)
            cross = _dot(qb, s.astype(BF16)) * cross_decay
            kd_t = (k * key_decay).T.astype(BF16)
            s_ref[b, h] = s * chunk_decay + _dot(kd_t, vb)
            o = inner + cross
            y_ref[rows, cols] = (_group_norm_rows(o, GN_EPS) * _silu(g_ref[rows, cols])).astype(BF16)


def _ret_sample(proj, lg, cos, sin, s0, nh, n_tok):
    bsz = s0.shape[0]
    hd = RET_HEAD_DIM
    dr = nh * hd
    bs = 2
    rows = bs * SAMPLE_SLOTS
    sect = lambda s: pl.BlockSpec((rows, dr), lambda i, s=s: (i, s))
    tab = pl.BlockSpec((SAMPLE_SLOTS, hd // 2), lambda i: (0, 0))
    st = pl.BlockSpec((bs, nh, hd, hd), lambda i: (i, 0, 0, 0))
    return pl.pallas_call(
        functools.partial(_ret_sample_body, nh=nh, n_tok=n_tok),
        grid=(bsz // bs,),
        in_specs=[pl.BlockSpec(memory_space=pltpu.SMEM), sect(0), sect(1), sect(2), sect(3), tab, tab, st],
        out_specs=[pl.BlockSpec((rows, dr), lambda i: (i, 0)), st],
        out_shape=[jax.ShapeDtypeStruct((bsz * SAMPLE_SLOTS, dr), BF16),
                   jax.ShapeDtypeStruct(s0.shape, F32)],
        compiler_params=_params(("parallel",)),
        name="retention_sample",
    )(lg, proj, proj, proj, proj, cos, sin, s0)


def _out_core(yr_ref, yw_ref, w1_ref, w2_ref, x, gate, fw):
    y = _dot(yr_ref[...], w1_ref[...]) + _dot(yw_ref[...], w2_ref[...])
    xn = x + gate * y
    ms = jnp.mean(xn * xn, axis=-1, keepdims=True)
    return xn * lax.rsqrt(ms + NORM_EPS) * fw


def _out_prompt_body(yr_ref, yw_ref, w1_ref, w2_ref, x_ref, mod_ref, fw_ref, o_ref):
    o_ref[0] = _out_core(yr_ref, yw_ref, w1_ref, w2_ref, x_ref[0], mod_ref[0][2:3], fw_ref[...])


def _out_prompt(y_ret, y_rw, w1, w2, x, mod3, fw):
    b, l, d = x.shape
    dr = w1.shape[0]
    tl = _tile(l, 256, 16)
    nt = l // tl
    wspec = pl.BlockSpec((dr, d), lambda bi, i: (0, 0))
    yspec = pl.BlockSpec((tl, dr), lambda bi, i: (bi * nt + i, 0))
    return pl.pallas_call(
        _out_prompt_body,
        grid=(b, nt),
        in_specs=[yspec, yspec, wspec, wspec,
                  pl.BlockSpec((1, tl, d), lambda bi, i: (bi, i, 0)),
                  pl.BlockSpec((1, 3, d), lambda bi, i: (bi, 0, 0)),
                  pl.BlockSpec((1, d), lambda bi, i: (0, 0))],
        out_specs=pl.BlockSpec((1, tl, d), lambda bi, i: (bi, i, 0)),
        out_shape=jax.ShapeDtypeStruct((b, l, d), F32),
        compiler_params=_params(("parallel", "parallel")),
        name="out_proj_prompt",
    )(y_ret, y_rw, w1, w2, x, mod3, fw.reshape(1, d))


def _out_sample_body(yr_ref, yw_ref, w1_ref, w2_ref, x_ref, mod_ref, fw_ref, o_ref):
    bs, sl, d = x_ref.shape
    gate = jnp.broadcast_to(mod_ref[...][:, 2:3, :], (bs, sl, d)).reshape(bs * sl, d)
    x = x_ref[...].reshape(bs * sl, d)
    o = _out_core(yr_ref, yw_ref, w1_ref, w2_ref, x, gate, fw_ref[...])
    o_ref[...] = o.reshape(bs, sl, d)


def _out_sample(y_ret, y_rw, w1, w2, x_slots, mod3, fw):
    bsz, sl, d = x_slots.shape
    dr = w1.shape[0]
    bs = _tile(bsz, 32, SUBLANES)
    wspec = pl.BlockSpec((dr, d), lambda i: (0, 0))
    yspec = pl.BlockSpec((bs * sl, dr), lambda i: (i, 0))
    xspec = pl.BlockSpec((bs, sl, d), lambda i: (i, 0, 0))
    return pl.pallas_call(
        _out_sample_body,
        grid=(bsz // bs,),
        in_specs=[yspec, yspec, wspec, wspec, xspec,
                  pl.BlockSpec((bs, 3, d), lambda i: (i, 0, 0)),
                  pl.BlockSpec((1, d), lambda i: (0, 0))],
        out_specs=xspec,
        out_shape=jax.ShapeDtypeStruct((bsz, sl, d), F32),
        compiler_params=_params(("parallel",)),
        name="out_proj_sample",
    )(y_ret, y_rw, w1, w2, x_slots, mod3, fw.reshape(1, d))


def _pad_chains(x, nc_pad):
    nc = x.shape[-1]
    if nc == nc_pad:
        return x
    return jnp.pad(x, [(0, 0)] * (x.ndim - 1) + [(0, nc_pad - nc)])


def _chain_params(p, nh_rw, nb, nc_pad):
    t = p.reshape(nh_rw, RWKV_HEAD_DIM).T
    return _pad_chains(jnp.tile(t, (1, nb)), nc_pad)


def _rope_tables(pos, half):
    inv_freq = ROPE_THETA ** (-jnp.arange(half, dtype=F32) / half)
    ang = pos[:, None] * inv_freq[None, :]
    return jnp.cos(ang), jnp.sin(ang)


def kernel(x_prompt, x_sample, c_prompt, c_sample, state_ret, state_rwkv, state_shift, norm_w, w_ada,
           b_ada, w_in, mu_shift, w0_decay, w2_decay, a0, a2, k_k, k_a, r_k, ln_x_w, ln_x_b, w_out,
           final_norm_w):
    depth = w_in.shape[0]
    assert depth == 1, "single-layer trunk"
    bp, lp, d = x_prompt.shape
    bsz, ls, _ = x_sample.shape
    assert ls == 4, "sample path packs 4 tokens into slots 1..4"
    dr = d
    nh_ret = dr // RET_HEAD_DIM
    nh_rw = dr // RWKV_HEAD_DIM
    n_main = 8 * dr
    lora = w2_decay.shape[1]
    lora_pad = -(-2 * lora // LANES) * LANES

    w_in0 = w_in[0]
    w_main = w_in0[:, :n_main].astype(BF16)
    w_lora = jnp.pad(w_in0[:, n_main:], ((0, 0), (0, lora_pad - 2 * lora))).astype(BF16)
    w_o = w_out[0].astype(BF16)
    w_o_ret, w_o_rw = w_o[:dr], w_o[dr:]
    mu = mu_shift[0]
    row = lambda p: p.reshape(1, -1)
    mu_r, mu_k, mu_v = row(mu[0:dr]), row(mu[dr:2 * dr]), row(mu[2 * dr:3 * dr])
    mu_l = jnp.pad(mu[3 * dr:], (0, lora_pad - 2 * lora)).reshape(1, lora_pad)
    w2p = jnp.pad(w2_decay[0], ((0, lora_pad - lora), (0, 0))).astype(BF16)
    a2p = jnp.pad(a2[0], ((lora, lora_pad - 2 * lora), (0, 0))).astype(BF16)
    lg = jnp.log1p(-jnp.exp2(-5.0 - jnp.arange(nh_ret, dtype=F32)))
    rw_params = (mu_r, mu_k, mu_v, mu_l, row(w0_decay[0]), w2p, row(a0[0]), a2p, row(k_k[0]), row(k_a[0]))

    n_c = bp + bsz
    n_c_pad = -(-n_c // SUBLANES) * SUBLANES
    c_all = jnp.pad(jnp.concatenate([c_prompt, c_sample], axis=0), ((0, n_c_pad - n_c), (0, 0)))
    mod3 = _adaln(c_all, w_ada[0], b_ada[0]).reshape(n_c_pad, 3, d)
    mod_p, mod_s = mod3[:bp], mod3[bp:bp + bsz]

    h_p, last_p = _mod_prompt(x_prompt, mod_p, norm_w[0])
    new_shift_p = last_p[:, SUBLANES - 1, :]
    proj_p = _matmul(h_p, w_main)
    lora_p = _matmul(h_p, w_lora, name="in_proj_lora")

    cos_p, sin_p = _rope_tables(jnp.arange(lp, dtype=F32), RET_HEAD_DIM // 2)
    y_ret_p, s_ret_p = _ret_prompt(proj_p, lg, cos_p, sin_p, bp, lp, nh_ret)
    y_rw_p, s_rw_p = _rwkv_chunk(proj_p, lora_p, bp, lp, dr, *rw_params,
                                 row(r_k[0]), row(ln_x_w[0]), row(ln_x_b[0]))
    y_prompt = _out_prompt(y_ret_p, y_rw_p, w_o_ret, w_o_rw, x_prompt, mod_p, final_norm_w)

    sl = SAMPLE_SLOTS
    x_slots = jnp.pad(x_sample, ((0, 0), (1, sl - 1 - ls), (0, 0)))
    h_s, new_shift_s = _mod_sample(x_slots, state_shift[0], mod_s, norm_w[0])
    proj_s = _matmul(h_s, w_main, tm_cap=1024)
    lora_s = _matmul(h_s, w_lora, tm_cap=1024, name="in_proj_lora")

    slot_pos = jnp.arange(sl, dtype=F32) - 1.0
    pos_s = jnp.where((slot_pos >= 0) & (slot_pos < ls), float(PAST_LEN) + slot_pos, 0.0)
    cos_s, sin_s = _rope_tables(pos_s, RET_HEAD_DIM // 2)
    y_ret_s, s_ret_s = _ret_sample(proj_s, lg, cos_s, sin_s, state_ret[0], nh_ret, ls)

    nc_s = bsz * nh_rw
    nc_pad = -(-nc_s // LANES) * LANES

    def to_chain(t):
        t4 = t.reshape(bsz, sl, nh_rw, RWKV_HEAD_DIM)[:, 1:1 + ls]
        return _pad_chains(jnp.transpose(t4, (1, 3, 0, 2)).reshape(ls, RWKV_HEAD_DIM, nc_s), nc_pad)

    prep = _rwkv_prep(proj_s, lora_s, dr, 4, *rw_params)
    lnw_c = _chain_params(ln_x_w[0], nh_rw, bsz, nc_pad)
    lnb_c = _chain_params(ln_x_b[0], nh_rw, bsz, nc_pad)
    rk_c = _chain_params(r_k[0].reshape(-1), nh_rw, bsz, nc_pad)
    s0_c = jnp.transpose(state_rwkv[0], (2, 3, 0, 1)).reshape(RWKV_HEAD_DIM, RWKV_HEAD_DIM, nc_s)
    o_c, s_c = _rwkv_seq(*[to_chain(t) for t in prep], lnw_c, lnb_c, rk_c, _pad_chains(s0_c, nc_pad))
    o4 = jnp.transpose(o_c[..., :nc_s].reshape(ls, RWKV_HEAD_DIM, bsz, nh_rw), (2, 0, 3, 1)).reshape(bsz, ls, dr)
    o_nat = jnp.pad(o4, ((0, 0), (1, sl - 1 - ls), (0, 0))).reshape(bsz * sl, dr)
    y_rw_s = _gate(o_nat, proj_s, dr, 7)
    s_rw_s = jnp.transpose(s_c[..., :nc_s].reshape(RWKV_HEAD_DIM, RWKV_HEAD_DIM, bsz, nh_rw), (2, 3, 0, 1))

    y_slots = _out_sample(y_ret_s, y_rw_s, w_o_ret, w_o_rw, x_slots, mod_s, final_norm_w)
    y_sample = y_slots[:, 1:1 + ls, :]

    return (y_prompt, y_sample, s_ret_p[None], s_rw_p[None], new_shift_p[None],
            s_ret_s[None], s_rw_s[None], new_shift_s[None])
```

```python
import functools
import math

import jax
import jax.numpy as jnp
from jax import lax
from jax.experimental import pallas as pl
from jax.experimental.pallas import tpu as pltpu

F32 = jnp.float32
BF16 = jnp.bfloat16

RET_HEAD_DIM = 256
RWKV_HEAD_DIM = 64
RET_CHUNK = 128
RWKV_CHUNK = 64
RWKV_GROUP = 4
PAST_LEN = 16384
ROPE_THETA = 10000.0
NORM_EPS = 1e-6
GN_EPS = 1e-5
RWKV_GN_EPS = 64e-5
SAMPLE_SLOTS = 8
LANES = 128
SUBLANES = 8
VMEM_LIMIT_BYTES = 56 * 1024 * 1024


def _params(sem):
    return pltpu.CompilerParams(dimension_semantics=sem, vmem_limit_bytes=VMEM_LIMIT_BYTES)


def _tile(n, cap, align):
    if n <= cap:
        return n
    t = (cap // align) * align
    while t >= align:
        if n % t == 0:
            return t
        t -= align
    return n


def _silu(x):
    return x * jax.nn.sigmoid(x)


def _dot(a, b):
    return jnp.dot(a, b, preferred_element_type=F32)


def _dot_nt(a, b):
    return lax.dot_general(a, b, (((1,), (1,)), ((), ())), preferred_element_type=F32)


def _dot_tn(a, b):
    return lax.dot_general(a, b, (((0,), (0,)), ((), ())), preferred_element_type=F32)


def _adaln_body(c_ref, w_ref, b_ref, o_ref):
    s = _silu(c_ref[...]).astype(BF16)
    o_ref[...] = _dot(s, w_ref[...].astype(BF16)) + b_ref[...]


def _adaln(c, w_ada, b_ada):
    rows, d = c.shape
    n = w_ada.shape[1]
    tn = _tile(n, 768, LANES)
    return pl.pallas_call(
        _adaln_body,
        grid=(n // tn,),
        in_specs=[pl.BlockSpec((rows, d), lambda j: (0, 0)),
                  pl.BlockSpec((d, tn), lambda j: (0, j)),
                  pl.BlockSpec((1, tn), lambda j: (0, j))],
        out_specs=pl.BlockSpec((rows, tn), lambda j: (0, j)),
        out_shape=jax.ShapeDtypeStruct((rows, n), F32),
        compiler_params=_params(("parallel",)),
        name="adaln",
    )(c, w_ada, b_ada.reshape(1, n))


def _modulated(x, nw, shift, scale):
    ms = jnp.mean(x * x, axis=-1, keepdims=True)
    return x * lax.rsqrt(ms + NORM_EPS) * nw * (1.0 + scale) + shift


def _mod_prompt_body(x_ref, mod_ref, nw_ref, h_ref, last_ref):
    m = mod_ref[0]
    h = _modulated(x_ref[0], nw_ref[...], m[0:1], m[1:2])
    h_ref[...] = h.astype(BF16)
    tl = h.shape[0]
    last_ref[0] = h[tl - SUBLANES:tl]


def _mod_prompt(x, mod3, nw):
    b, l, d = x.shape
    tl = _tile(l, 256, 16)
    nt = l // tl
    return pl.pallas_call(
        _mod_prompt_body,
        grid=(b, nt),
        in_specs=[pl.BlockSpec((1, tl, d), lambda bi, i: (bi, i, 0)),
                  pl.BlockSpec((1, 3, d), lambda bi, i: (bi, 0, 0)),
                  pl.BlockSpec((1, d), lambda bi, i: (0, 0))],
        out_specs=[pl.BlockSpec((tl, d), lambda bi, i: (bi * nt + i, 0)),
                   pl.BlockSpec((1, SUBLANES, d), lambda bi, i: (bi, 0, 0))],
        out_shape=[jax.ShapeDtypeStruct((b * l, d), BF16),
                   jax.ShapeDtypeStruct((b, SUBLANES, d), F32)],
        compiler_params=_params(("parallel", "arbitrary")),
        name="modulate_prompt",
    )(x, mod3, nw.reshape(1, d))


def _mod_sample_body(x_ref, prev_ref, mod_ref, nw_ref, h_ref, new_ref):
    x = x_ref[...]
    m = mod_ref[...]
    h = _modulated(x, nw_ref[...], m[:, 0:1, :], m[:, 1:2, :])
    slot = lax.broadcasted_iota(jnp.int32, h.shape, 1)
    hp = prev_ref[...][:, None, :]
    full = jnp.where(slot == 0, hp, h)
    bs, _, d = x.shape
    h_ref[...] = full.reshape(bs * SAMPLE_SLOTS, d).astype(BF16)
    new_ref[...] = h[:, 4, :]


def _mod_sample(x_slots, prev, mod3, nw):
    bsz, _, d = x_slots.shape
    bs = _tile(bsz, 16, SUBLANES)
    return pl.pallas_call(
        _mod_sample_body,
        grid=(bsz // bs,),
        in_specs=[pl.BlockSpec((bs, SAMPLE_SLOTS, d), lambda i: (i, 0, 0)),
                  pl.BlockSpec((bs, d), lambda i: (i, 0)),
                  pl.BlockSpec((bs, 3, d), lambda i: (i, 0, 0)),
                  pl.BlockSpec((1, d), lambda i: (0, 0))],
        out_specs=[pl.BlockSpec((bs * SAMPLE_SLOTS, d), lambda i: (i, 0)),
                   pl.BlockSpec((bs, d), lambda i: (i, 0))],
        out_shape=[jax.ShapeDtypeStruct((bsz * SAMPLE_SLOTS, d), BF16),
                   jax.ShapeDtypeStruct((bsz, d), F32)],
        compiler_params=_params(("parallel",)),
        name="modulate_sample",
    )(x_slots, prev, mod3, nw.reshape(1, d))


def _matmul_body(x_ref, w_ref, o_ref):
    o_ref[...] = _dot(x_ref[...], w_ref[...])


def _matmul(x, w, tm_cap=512, tn_cap=1024, name="in_proj"):
    m, k = x.shape
    n = w.shape[1]
    tm = _tile(m, tm_cap, 16)
    tn = _tile(n, tn_cap, LANES)
    return pl.pallas_call(
        _matmul_body,
        grid=(n // tn, m // tm),
        in_specs=[pl.BlockSpec((tm, k), lambda j, i: (i, 0)),
                  pl.BlockSpec((k, tn), lambda j, i: (0, j))],
        out_specs=pl.BlockSpec((tm, tn), lambda j, i: (i, j)),
        out_shape=jax.ShapeDtypeStruct((m, n), F32),
        compiler_params=_params(("parallel", "parallel")),
        name=name,
    )(x, w)


def _token_shift(cur_ref, carry_ref, mu_ref):
    cur = cur_ref[...]
    tl = cur.shape[0]
    rolled = pltpu.roll(cur, 1, 0)
    row = lax.broadcasted_iota(jnp.int32, cur.shape, 0)
    prev = jnp.where(row == 0, carry_ref[0:1, :], rolled)
    carry_ref[0:1, :] = cur[tl - 1:tl, :]
    return cur + (prev - cur) * mu_ref[...]


def _decay_and_rate(lo, w0_ref, w2_ref, a0_ref, a2_ref):
    dec = _dot(jnp.tanh(lo).astype(BF16), w2_ref[...])
    z = -(w0_ref[...] + dec)
    softplus = jnp.maximum(z, 0.0) + jnp.log1p(jnp.exp(-jnp.abs(z)))
    w_log = -softplus - 0.5
    a = jax.nn.sigmoid(a0_ref[...] + _dot(lo.astype(BF16), a2_ref[...]))
    return -jnp.exp(w_log), a


def _zero_refs(*refs):
    for r in refs:
        r[...] = jnp.zeros_like(r)


def _rwkv_prep_body(pr_ref, pk_ref, pv_ref, pl_ref, mu_r, mu_k, mu_v, mu_l, w0_ref, w2_ref,
                    a0_ref, a2_ref, kk_ref, ka_ref,
                    r_out, w_out, k_out, v_out, kk_out, a_out,
                    c_r, c_k, c_v, c_l):
    @pl.when(pl.program_id(0) == 0)
    def _():
        _zero_refs(c_r, c_k, c_v, c_l)

    r = _token_shift(pr_ref, c_r, mu_r)
    kw = _token_shift(pk_ref, c_k, mu_k)
    vw = _token_shift(pv_ref, c_v, mu_v)
    lo = _token_shift(pl_ref, c_l, mu_l)
    logw, a = _decay_and_rate(lo, w0_ref, w2_ref, a0_ref, a2_ref)
    r_out[...] = r
    w_out[...] = jnp.exp(logw)
    k_out[...] = kw * (1.0 + (a - 1.0) * ka_ref[...])
    v_out[...] = vw
    kk_out[...] = kw * kk_ref[...]
    a_out[...] = a


def _rwkv_prep(proj, lora, dr, col0, mu_r, mu_k, mu_v, mu_l, w0, w2p, a0, a2p, k_k, k_a):
    rows = proj.shape[0]
    lp = lora.shape[1]
    tl = _tile(rows, 128, SUBLANES)
    vec = lambda: pl.BlockSpec((1, dr), lambda i: (0, 0))
    sect = lambda s: pl.BlockSpec((tl, dr), lambda i, s=s: (i, col0 + s))
    out = pl.BlockSpec((tl, dr), lambda i: (i, 0))
    shape = jax.ShapeDtypeStruct((rows, dr), F32)
    return pl.pallas_call(
        _rwkv_prep_body,
        grid=(rows // tl,),
        in_specs=[sect(0), sect(1), sect(2),
                  pl.BlockSpec((tl, lp), lambda i: (i, 0)),
                  vec(), vec(), vec(),
                  pl.BlockSpec((1, lp), lambda i: (0, 0)),
                  vec(),
                  pl.BlockSpec((lp, dr), lambda i: (0, 0)),
                  vec(),
                  pl.BlockSpec((lp, dr), lambda i: (0, 0)),
                  vec(), vec()],
        out_specs=[out] * 6,
        out_shape=[shape] * 6,
        scratch_shapes=[pltpu.VMEM((SUBLANES, dr), F32), pltpu.VMEM((SUBLANES, dr), F32),
                        pltpu.VMEM((SUBLANES, dr), F32), pltpu.VMEM((SUBLANES, lp), F32)],
        compiler_params=_params(("arbitrary",)),
        name="rwkv_prep",
    )(proj, proj, proj, lora, mu_r, mu_k, mu_v, mu_l, w0, w2p, a0, a2p, k_k, k_a)


def _rwkv_seq_body(r_ref, w_ref, k_ref, v_ref, kk_ref, a_ref, lnw_ref, lnb_ref, rk_ref, s0_ref,
                   o_ref, so_ref, s_scr, o_scr):
    ti = pl.program_id(1)
    tt_n = r_ref.shape[0]
    hd = r_ref.shape[1]

    @pl.when(ti == 0)
    def _():
        s_scr[...] = s0_ref[...]

    def token(tt, carry):
        rv = r_ref[tt]
        wv = w_ref[tt]
        kv = k_ref[tt]
        kkr = kk_ref[tt]
        nrm = jnp.sqrt(jnp.sum(kkr * kkr, axis=0, keepdims=True))
        kkn = kkr / jnp.maximum(nrm, 1e-12)
        bv = kkn * a_ref[tt]
        nkk = -kkn

        def vblock(vb, c):
            base = pl.multiple_of(vb * SUBLANES, SUBLANES)
            vt = v_ref[tt, pl.ds(base, SUBLANES), :]
            sub = lax.broadcasted_iota(jnp.int32, vt.shape, 0)
            tile = jnp.zeros_like(vt)
            for j in range(SUBLANES):
                sv = s_scr[base + j]
                sa = jnp.sum(sv * nkk, axis=0, keepdims=True)
                sn = sv * wv + sa * bv + vt[j:j + 1, :] * kv
                s_scr[base + j] = sn
                tile = jnp.where(sub == j, jnp.sum(sn * rv, axis=0, keepdims=True), tile)
            o_scr[pl.ds(base, SUBLANES), :] = tile
            return c

        lax.fori_loop(0, hd // SUBLANES, vblock, 0)
        o = o_scr[...]
        mean = jnp.mean(o, axis=0, keepdims=True)
        oc = o - mean
        var = jnp.mean(oc * oc, axis=0, keepdims=True)
        og = oc * lax.rsqrt(var + RWKV_GN_EPS) * lnw_ref[...] + lnb_ref[...]
        bonus = jnp.sum(rv * kv * rk_ref[...], axis=0, keepdims=True) * v_ref[tt]
        o_ref[tt] = og + bonus
        return carry

    lax.fori_loop(0, tt_n, token, 0)

    @pl.when(ti == pl.num_programs(1) - 1)
    def _():
        so_ref[...] = s_scr[...]


def _rwkv_seq(r, w, k, v, kk, a, lnw, lnb, rk, s0):
    t, hd, nc = r.shape
    tt = _tile(t, 32, 1)
    seq = pl.BlockSpec((tt, hd, LANES), lambda c, i: (i, 0, c))
    par = pl.BlockSpec((hd, LANES), lambda c, i: (0, c))
    st = pl.BlockSpec((hd, hd, LANES), lambda c, i: (0, 0, c))
    return pl.pallas_call(
        _rwkv_seq_body,
        grid=(nc // LANES, t // tt),
        in_specs=[seq] * 6 + [par] * 3 + [st],
        out_specs=[seq, st],
        out_shape=[jax.ShapeDtypeStruct((t, hd, nc), F32),
                   jax.ShapeDtypeStruct((hd, hd, nc), F32)],
        scratch_shapes=[pltpu.VMEM((hd, hd, LANES), F32), pltpu.VMEM((hd, LANES), F32)],
        compiler_params=_params(("parallel", "arbitrary")),
        name="rwkv_recurrence",
    )(r, w, k, v, kk, a, lnw, lnb, rk, s0)


def _gate_body(o_ref, g_ref, y_ref):
    y_ref[...] = (o_ref[...] * _silu(g_ref[...])).astype(BF16)


def _gate(o, proj, dr, gcol):
    rows = o.shape[0]
    tl = _tile(rows, 256, 16)
    return pl.pallas_call(
        _gate_body,
        grid=(rows // tl,),
        in_specs=[pl.BlockSpec((tl, dr), lambda i: (i, 0)),
                  pl.BlockSpec((tl, dr), lambda i: (i, gcol))],
        out_specs=pl.BlockSpec((tl, dr), lambda i: (i, 0)),
        out_shape=jax.ShapeDtypeStruct((rows, dr), BF16),
        compiler_params=_params(("parallel",)),
        name="rwkv_gate",
    )(o, proj)


def _split3(x):
    hi = x.astype(BF16)
    r1 = x - hi.astype(F32)
    mid = r1.astype(BF16)
    lo = (r1 - mid.astype(F32)).astype(BF16)
    return hi, mid, lo


def _rwkv_chunk_body(pr_ref, pk_ref, pv_ref, pg_ref, pl_ref, mu_r, mu_k, mu_v, mu_l, w0_ref, w2_ref,
                     a0_ref, a2_ref, kk_ref, ka_ref, rk_ref, lnw_ref, lnb_ref,
                     y_ref, so_ref, c_r, c_k, c_v, c_l, s_scr):
    n = pl.program_id(1)
    c = pr_ref.shape[0]
    gw = RWKV_GROUP * RWKV_HEAD_DIM
    n_groups = pr_ref.shape[1] // gw
    hd = RWKV_HEAD_DIM

    @pl.when(n == 0)
    def _():
        _zero_refs(c_r, c_k, c_v, c_l, s_scr)

    r_all = _token_shift(pr_ref, c_r, mu_r)
    kw_all = _token_shift(pk_ref, c_k, mu_k)
    v_all = _token_shift(pv_ref, c_v, mu_v)
    lo = _token_shift(pl_ref, c_l, mu_l)
    logw_all, a_all = _decay_and_rate(lo, w0_ref, w2_ref, a0_ref, a2_ref)
    k_all = kw_all * (1.0 + (a_all - 1.0) * ka_ref[...])
    kkr_all = kw_all * kk_ref[...]

    ti = lax.broadcasted_iota(jnp.int32, (c, c), 0)
    tj = lax.broadcasted_iota(jnp.int32, (c, c), 1)
    ltri = (ti >= tj).astype(BF16)
    hi, mid, low = _split3(logw_all)
    cum_all = _dot(ltri, hi) + _dot(ltri, mid) + _dot(ltri, low)

    hr = lax.broadcasted_iota(jnp.int32, (gw, gw), 0) // hd
    hc = lax.broadcasted_iota(jnp.int32, (gw, gw), 1) // hd
    headmask = hr == hc
    ones_bd = headmask.astype(BF16)
    tok = lax.broadcasted_iota(jnp.int32, (c, gw), 0)
    src = lax.broadcasted_iota(jnp.int32, (c, gw), 1) % hd
    strict = tok > src
    incl = tok >= src

    def bd(x):
        t = jnp.concatenate([x.astype(BF16)] * RWKV_GROUP, axis=0)
        return jnp.where(headmask, t, jnp.zeros_like(t))

    def head_sum(x, exact=False):
        xh = x.astype(BF16)
        if not exact:
            return _dot(xh, ones_bd)
        xl = (x - xh.astype(F32)).astype(BF16)
        return _dot(xh, ones_bd) + _dot(xl, ones_bd)

    groups = range(n_groups)
    sls = [slice(g * gw, (g + 1) * gw) for g in groups]
    r = [r_all[:, sl] for sl in sls]
    k = [k_all[:, sl] for sl in sls]
    v = [v_all[:, sl] for sl in sls]
    cum = [cum_all[:, sl] for sl in sls]
    cum_end = [cm[c - 1:c, :] for cm in cum]

    kk_sq = [head_sum(jnp.square(kkr_all[:, sl])) for sl in sls]
    rk_sum = [head_sum(r[g] * k[g] * rk_ref[:, sls[g]], exact=True) for g in groups]
    kkn = [kkr_all[:, sls[g]] / jnp.maximum(jnp.sqrt(kk_sq[g]), 1e-12) for g in groups]
    b = [kkn[g] * a_all[:, sls[g]] for g in groups]
    e_inv = [jnp.exp(-cum[g]) for g in groups]
    ar = [jnp.concatenate([-kkn[g] * jnp.exp(cum[g] - logw_all[:, sls[g]]), r[g] * jnp.exp(cum[g])],
                          axis=0).astype(BF16) for g in groups]
    s0 = [s_scr[g] for g in groups]
    from_state = [_dot_nt(ar[g], s0[g].astype(BF16)) for g in groups]
    s_b = [_dot_nt(ar[g], bd(b[g] * e_inv[g])) for g in groups]
    s_k = [_dot_nt(ar[g], bd(k[g] * e_inv[g])) for g in groups]
    a_ab = [jnp.where(strict, x[:c], 0.0) for x in s_b]
    a_rb = [jnp.where(incl, x[c:], 0.0) for x in s_b]
    a_ak = [jnp.where(strict, x[:c], 0.0) for x in s_k]
    a_rk = [jnp.where(incl, x[c:], 0.0) for x in s_k]

    steps = max(1, int(math.ceil(math.log2(c))))
    q = a_ab
    pm = a_ab
    q_next = [_dot(q[g].astype(BF16), bd(q[g])) for g in groups]
    for s in range(1, steps):
        q = q_next
        if s < steps - 1:
            res = [_dot(jnp.concatenate([q[g], pm[g]], axis=0).astype(BF16), bd(q[g])) for g in groups]
            q_next = [x[:c] for x in res]
            pm = [pm[g] + q[g] + res[g][c:] for g in groups]
        else:
            pm = [pm[g] + q[g] + _dot(pm[g].astype(BF16), bd(q[g])) for g in groups]

    bd_v = [bd(x) for x in v]
    w_rhs = [from_state[g][:c] + _dot(a_ak[g].astype(BF16), bd_v[g]) for g in groups]
    u = [w_rhs[g] + _dot(pm[g].astype(BF16), bd(w_rhs[g])) for g in groups]
    o = [from_state[g][c:] + _dot(jnp.concatenate([a_rb[g], a_rk[g]], axis=1).astype(BF16),
                                  jnp.concatenate([bd(u[g]), bd_v[g]], axis=0)) for g in groups]
    for g in groups:
        e_end = jnp.exp(cum_end[g] - cum[g])
        uv = jnp.concatenate([u[g], v[g]], axis=0).astype(BF16)
        bk = jnp.concatenate([b[g] * e_end, k[g] * e_end], axis=0).astype(BF16)
        s_new = s0[g] * jnp.exp(cum_end[g]) + _dot_tn(uv, bk)
        s_scr[g] = jnp.where(headmask, s_new, 0.0)

    mean = [head_sum(x) * (1.0 / hd) for x in o]
    oc = [o[g] - mean[g] for g in groups]
    var = [head_sum(jnp.square(x)) * (1.0 / hd) for x in oc]
    for g in groups:
        sl = sls[g]
        og = oc[g] * lax.rsqrt(var[g] + RWKV_GN_EPS) * lnw_ref[:, sl] + lnb_ref[:, sl]
        y_ref[:, sl] = ((og + rk_sum[g] * v[g]) * _silu(pg_ref[:, sl])).astype(BF16)

    @pl.when(n == pl.num_programs(1) - 1)
    def _():
        for g in range(n_groups):
            sg = s_scr[g]
            for j in range(RWKV_GROUP):
                so_ref[0, g * RWKV_GROUP + j] = sg[j * hd:(j + 1) * hd, j * hd:(j + 1) * hd]


def _rwkv_chunk(proj, lora, nb, l, dr, mu_r, mu_k, mu_v, mu_l, w0, w2p, a0, a2p, k_k, k_a, r_k, ln_w, ln_b):
    c = RWKV_CHUNK
    assert l % c == 0 and c == RWKV_HEAD_DIM
    nchunk = l // c
    lp = lora.shape[1]
    nh = dr // RWKV_HEAD_DIM
    gw = RWKV_GROUP * RWKV_HEAD_DIM
    vec = lambda: pl.BlockSpec((1, dr), lambda b, n: (0, 0))
    sect = lambda s: pl.BlockSpec((c, dr), lambda b, n, s=s: (b * nchunk + n, 4 + s))
    return pl.pallas_call(
        _rwkv_chunk_body,
        grid=(nb, nchunk),
        in_specs=[sect(0), sect(1), sect(2), sect(3),
                  pl.BlockSpec((c, lp), lambda b, n: (b * nchunk + n, 0)),
                  vec(), vec(), vec(),
                  pl.BlockSpec((1, lp), lambda b, n: (0, 0)),
                  vec(),
                  pl.BlockSpec((lp, dr), lambda b, n: (0, 0)),
                  vec(),
                  pl.BlockSpec((lp, dr), lambda b, n: (0, 0)),
                  vec(), vec(), vec(), vec(), vec()],
        out_specs=[pl.BlockSpec((c, dr), lambda b, n: (b * nchunk + n, 0)),
                   pl.BlockSpec((1, nh, RWKV_HEAD_DIM, RWKV_HEAD_DIM), lambda b, n: (b, 0, 0, 0))],
        out_shape=[jax.ShapeDtypeStruct((nb * l, dr), BF16),
                   jax.ShapeDtypeStruct((nb, nh, RWKV_HEAD_DIM, RWKV_HEAD_DIM), F32)],
        scratch_shapes=[pltpu.VMEM((SUBLANES, dr), F32), pltpu.VMEM((SUBLANES, dr), F32),
                        pltpu.VMEM((SUBLANES, dr), F32), pltpu.VMEM((SUBLANES, lp), F32),
                        pltpu.VMEM((dr // gw, gw, gw), F32)],
        compiler_params=_params(("parallel", "arbitrary")),
        name="rwkv_chunked",
    )(proj, proj, proj, proj, lora, mu_r, mu_k, mu_v, mu_l, w0, w2p, a0, a2p, k_k, k_a, r_k, ln_w, ln_b)


def _rotary(x, cos, sin):
    half = x.shape[-1] // 2
    x1 = x[:, :half]
    x2 = x[:, half:]
    return jnp.concatenate([x1 * cos - x2 * sin, x1 * sin + x2 * cos], axis=-1)


def _group_norm_rows(o, eps):
    mean = jnp.mean(o, axis=-1, keepdims=True)
    oc = o - mean
    var = jnp.mean(oc * oc, axis=-1, keepdims=True)
    return oc * lax.rsqrt(var + eps)


def _ret_prompt_body(lg_ref, q_ref, k_ref, v_ref, g_ref, cos_ref, sin_ref, y_ref, s_ref):
    n = pl.program_id(1)
    c = q_ref.shape[0]
    hd = RET_HEAD_DIM
    heads = range(q_ref.shape[1] // hd)
    cols = [slice(h * hd, (h + 1) * hd) for h in heads]

    @pl.when(n == 0)
    def _():
        s_ref[...] = jnp.zeros_like(s_ref)

    cos = cos_ref[...]
    sin = sin_ref[...]
    ii = lax.broadcasted_iota(jnp.int32, (c, c), 0)
    jj = lax.broadcasted_iota(jnp.int32, (c, c), 1)
    diff = (ii - jj).astype(F32)
    causal = diff >= 0.0
    dist = jnp.maximum(diff, 0.0)
    ic = lax.broadcasted_iota(jnp.int32, (c, 1), 0).astype(F32)

    lg = [jnp.full((1, 1), lg_ref[h], F32) for h in heads]
    q = [_rotary(q_ref[:, cs], cos, sin).astype(BF16) for cs in cols]
    k = [_rotary(k_ref[:, cs], cos, sin) * (hd ** -0.5) for cs in cols]
    vb = [v_ref[:, cs].astype(BF16) for cs in cols]
    s = [s_ref[0, h] for h in heads]
    scores = [_dot_nt(q[h], k[h].astype(BF16)) * jnp.where(causal, jnp.exp(dist * lg[h]), 0.0) for h in heads]
    cross = [_dot(q[h], s[h].astype(BF16)) * jnp.exp((ic + 1.0) * lg[h]) for h in heads]
    o = [_dot(scores[h].astype(BF16), vb[h]) + cross[h] for h in heads]
    for h in heads:
        kd = (k[h] * jnp.exp((c - 1.0 - ic) * lg[h])).astype(BF16)
        s_ref[0, h] = s[h] * jnp.exp(float(c) * lg[h]) + _dot_tn(kd, vb[h])
    for h in heads:
        y_ref[:, cols[h]] = (_group_norm_rows(o[h], GN_EPS) * _silu(g_ref[:, cols[h]])).astype(BF16)


def _ret_prompt(proj, lg, cos, sin, nb, l, nh):
    hd = RET_HEAD_DIM
    dr = nh * hd
    c = math.gcd(l, RET_CHUNK)
    nchunk = l // c
    sect = lambda s: pl.BlockSpec((c, dr), lambda b, n, s=s: (b * nchunk + n, s))
    tab = pl.BlockSpec((c, hd // 2), lambda b, n: (n, 0))
    return pl.pallas_call(
        _ret_prompt_body,
        grid=(nb, nchunk),
        in_specs=[pl.BlockSpec(memory_space=pltpu.SMEM), sect(0), sect(1), sect(2), sect(3), tab, tab],
        out_specs=[pl.BlockSpec((c, dr), lambda b, n: (b * nchunk + n, 0)),
                   pl.BlockSpec((1, nh, hd, hd), lambda b, n: (b, 0, 0, 0))],
        out_shape=[jax.ShapeDtypeStruct((nb * l, dr), BF16),
                   jax.ShapeDtypeStruct((nb, nh, hd, hd), F32)],
        compiler_params=_params(("parallel", "arbitrary")),
        name="retention_prompt",
    )(lg, proj, proj, proj, proj, cos, sin)


def _ret_sample_body(lg_ref, q_ref, k_ref, v_ref, g_ref, cos_ref, sin_ref, s0_ref, y_ref, s_ref, *, nh, n_tok):
    hd = RET_HEAD_DIM
    bs = s0_ref.shape[0]
    sl = SAMPLE_SLOTS
    cos = cos_ref[...]
    sin = sin_ref[...]
    slot_r = lax.broadcasted_iota(jnp.int32, (sl, 1), 0)
    tok_r = (slot_r - 1).astype(F32)
    valid_r = (slot_r >= 1) & (slot_r <= n_tok)
    ii = lax.broadcasted_iota(jnp.int32, (sl, sl), 0)
    jj = lax.broadcasted_iota(jnp.int32, (sl, sl), 1)
    diff = (ii - jj).astype(F32)
    pair_ok = (diff >= 0.0) & (jj >= 1) & (jj <= n_tok)
    for h in range(nh):
        lg = jnp.full((1, 1), lg_ref[h], F32)
        dmask = jnp.where(pair_ok, jnp.exp(jnp.maximum(diff, 0.0) * lg), 0.0)
        cross_decay = jnp.exp((tok_r + 1.0) * lg)
        key_decay = jnp.where(valid_r, jnp.exp((n_tok - 1.0 - tok_r) * lg), 0.0)
        chunk_decay = jnp.exp(float(n_tok) * lg)
        for b in range(bs):
            rows = slice(b * sl, (b + 1) * sl)
            cols = slice(h * hd, (h + 1) * hd)
            q = _rotary(q_ref[rows, cols], cos, sin)
            k = _rotary(k_ref[rows, cols], cos, sin) * (hd ** -0.5)
            vb = v_ref[rows, cols].astype(BF16)
            s = s0_ref[b, h]
            qb = q.astype(BF16)
            scores = _dot_nt(qb, k.astype(BF16)) * dmask
            inner = _dot(scores.astype(BF16), vb)
            cross = _dot(qb, s.astype(BF16)) * cross_decay
            kd_t = (k * key_decay).T.astype(BF16)
            s_ref[b, h] = s * chunk_decay + _dot(kd_t, vb)
            o = inner + cross
            y_ref[rows, cols] = (_group_norm_rows(o, GN_EPS) * _silu(g_ref[rows, cols])).astype(BF16)


def _ret_sample(proj, lg, cos, sin, s0, nh, n_tok):
    bsz = s0.shape[0]
    hd = RET_HEAD_DIM
    dr = nh * hd
    bs = 2
    rows = bs * SAMPLE_SLOTS
    sect = lambda s: pl.BlockSpec((rows, dr), lambda i, s=s: (i, s))
    tab = pl.BlockSpec((SAMPLE_SLOTS, hd // 2), lambda i: (0, 0))
    st = pl.BlockSpec((bs, nh, hd, hd), lambda i: (i, 0, 0, 0))
    return pl.pallas_call(
        functools.partial(_ret_sample_body, nh=nh, n_tok=n_tok),
        grid=(bsz // bs,),
        in_specs=[pl.BlockSpec(memory_space=pltpu.SMEM), sect(0), sect(1), sect(2), sect(3), tab, tab, st],
        out_specs=[pl.BlockSpec((rows, dr), lambda i: (i, 0)), st],
        out_shape=[jax.ShapeDtypeStruct((bsz * SAMPLE_SLOTS, dr), BF16),
                   jax.ShapeDtypeStruct(s0.shape, F32)],
        compiler_params=_params(("parallel",)),
        name="retention_sample",
    )(lg, proj, proj, proj, proj, cos, sin, s0)


def _out_core(yr_ref, yw_ref, w1_ref, w2_ref, x, gate, fw):
    y = _dot(yr_ref[...], w1_ref[...]) + _dot(yw_ref[...], w2_ref[...])
    xn = x + gate * y
    ms = jnp.mean(xn * xn, axis=-1, keepdims=True)
    return xn * lax.rsqrt(ms + NORM_EPS) * fw


def _out_prompt_body(yr_ref, yw_ref, w1_ref, w2_ref, x_ref, mod_ref, fw_ref, o_ref):
    o_ref[0] = _out_core(yr_ref, yw_ref, w1_ref, w2_ref, x_ref[0], mod_ref[0][2:3], fw_ref[...])


def _out_prompt(y_ret, y_rw, w1, w2, x, mod3, fw):
    b, l, d = x.shape
    dr = w1.shape[0]
    tl = _tile(l, 256, 16)
    nt = l // tl
    wspec = pl.BlockSpec((dr, d), lambda bi, i: (0, 0))
    yspec = pl.BlockSpec((tl, dr), lambda bi, i: (bi * nt + i, 0))
    return pl.pallas_call(
        _out_prompt_body,
        grid=(b, nt),
        in_specs=[yspec, yspec, wspec, wspec,
                  pl.BlockSpec((1, tl, d), lambda bi, i: (bi, i, 0)),
                  pl.BlockSpec((1, 3, d), lambda bi, i: (bi, 0, 0)),
                  pl.BlockSpec((1, d), lambda bi, i: (0, 0))],
        out_specs=pl.BlockSpec((1, tl, d), lambda bi, i: (bi, i, 0)),
        out_shape=jax.ShapeDtypeStruct((b, l, d), F32),
        compiler_params=_params(("parallel", "parallel")),
        name="out_proj_prompt",
    )(y_ret, y_rw, w1, w2, x, mod3, fw.reshape(1, d))


def _out_sample_body(yr_ref, yw_ref, w1_ref, w2_ref, x_ref, mod_ref, fw_ref, o_ref):
    bs, sl, d = x_ref.shape
    gate = jnp.broadcast_to(mod_ref[...][:, 2:3, :], (bs, sl, d)).reshape(bs * sl, d)
    x = x_ref[...].reshape(bs * sl, d)
    o = _out_core(yr_ref, yw_ref, w1_ref, w2_ref, x, gate, fw_ref[...])
    o_ref[...] = o.reshape(bs, sl, d)


def _out_sample(y_ret, y_rw, w1, w2, x_slots, mod3, fw):
    bsz, sl, d = x_slots.shape
    dr = w1.shape[0]
    bs = _tile(bsz, 32, SUBLANES)
    wspec = pl.BlockSpec((dr, d), lambda i: (0, 0))
    yspec = pl.BlockSpec((bs * sl, dr), lambda i: (i, 0))
    xspec = pl.BlockSpec((bs, sl, d), lambda i: (i, 0, 0))
    return pl.pallas_call(
        _out_sample_body,
        grid=(bsz // bs,),
        in_specs=[yspec, yspec, wspec, wspec, xspec,
                  pl.BlockSpec((bs, 3, d), lambda i: (i, 0, 0)),
                  pl.BlockSpec((1, d), lambda i: (0, 0))],
        out_specs=xspec,
        out_shape=jax.ShapeDtypeStruct((bsz, sl, d), F32),
        compiler_params=_params(("parallel",)),
        name="out_proj_sample",
    )(y_ret, y_rw, w1, w2, x_slots, mod3, fw.reshape(1, d))


def _pad_chains(x, nc_pad):
    nc = x.shape[-1]
    if nc == nc_pad:
        return x
    return jnp.pad(x, [(0, 0)] * (x.ndim - 1) + [(0, nc_pad - nc)])


def _chain_params(p, nh_rw, nb, nc_pad):
    t = p.reshape(nh_rw, RWKV_HEAD_DIM).T
    return _pad_chains(jnp.tile(t, (1, nb)), nc_pad)


def _rope_tables(pos, half):
    inv_freq = ROPE_THETA ** (-jnp.arange(half, dtype=F32) / half)
    ang = pos[:, None] * inv_freq[None, :]
    return jnp.cos(ang), jnp.sin(ang)


def kernel(x_prompt, x_sample, c_prompt, c_sample, state_ret, state_rwkv, state_shift, norm_w, w_ada,
           b_ada, w_in, mu_shift, w0_decay, w2_decay, a0, a2, k_k, k_a, r_k, ln_x_w, ln_x_b, w_out,
           final_norm_w):
    depth = w_in.shape[0]
    assert depth == 1, "single-layer trunk"
    bp, lp, d = x_prompt.shape
    bsz, ls, _ = x_sample.shape
    assert ls == 4, "sample path packs 4 tokens into slots 1..4"
    dr = d
    nh_ret = dr // RET_HEAD_DIM
    nh_rw = dr // RWKV_HEAD_DIM
    n_main = 8 * dr
    lora = w2_decay.shape[1]
    lora_pad = -(-2 * lora // LANES) * LANES

    w_in0 = w_in[0]
    w_main = w_in0[:, :n_main].astype(BF16)
    w_lora = jnp.pad(w_in0[:, n_main:], ((0, 0), (0, lora_pad - 2 * lora))).astype(BF16)
    w_o = w_out[0].astype(BF16)
    w_o_ret, w_o_rw = w_o[:dr], w_o[dr:]
    mu = mu_shift[0]
    row = lambda p: p.reshape(1, -1)
    mu_r, mu_k, mu_v = row(mu[0:dr]), row(mu[dr:2 * dr]), row(mu[2 * dr:3 * dr])
    mu_l = jnp.pad(mu[3 * dr:], (0, lora_pad - 2 * lora)).reshape(1, lora_pad)
    w2p = jnp.pad(w2_decay[0], ((0, lora_pad - lora), (0, 0))).astype(BF16)
    a2p = jnp.pad(a2[0], ((lora, lora_pad - 2 * lora), (0, 0))).astype(BF16)
    lg = jnp.log1p(-jnp.exp2(-5.0 - jnp.arange(nh_ret, dtype=F32)))
    rw_params = (mu_r, mu_k, mu_v, mu_l, row(w0_decay[0]), w2p, row(a0[0]), a2p, row(k_k[0]), row(k_a[0]))

    n_c = bp + bsz
    n_c_pad = -(-n_c // SUBLANES) * SUBLANES
    c_all = jnp.pad(jnp.concatenate([c_prompt, c_sample], axis=0), ((0, n_c_pad - n_c), (0, 0)))
    mod3 = _adaln(c_all, w_ada[0], b_ada[0]).reshape(n_c_pad, 3, d)
    mod_p, mod_s = mod3[:bp], mod3[bp:bp + bsz]

    h_p, last_p = _mod_prompt(x_prompt, mod_p, norm_w[0])
    new_shift_p = last_p[:, SUBLANES - 1, :]
    proj_p = _matmul(h_p, w_main)
    lora_p = _matmul(h_p, w_lora, name="in_proj_lora")

    cos_p, sin_p = _rope_tables(jnp.arange(lp, dtype=F32), RET_HEAD_DIM // 2)
    y_ret_p, s_ret_p = _ret_prompt(proj_p, lg, cos_p, sin_p, bp, lp, nh_ret)
    y_rw_p, s_rw_p = _rwkv_chunk(proj_p, lora_p, bp, lp, dr, *rw_params,
                                 row(r_k[0]), row(ln_x_w[0]), row(ln_x_b[0]))
    y_prompt = _out_prompt(y_ret_p, y_rw_p, w_o_ret, w_o_rw, x_prompt, mod_p, final_norm_w)

    sl = SAMPLE_SLOTS
    x_slots = jnp.pad(x_sample, ((0, 0), (1, sl - 1 - ls), (0, 0)))
    h_s, new_shift_s = _mod_sample(x_slots, state_shift[0], mod_s, norm_w[0])
    proj_s = _matmul(h_s, w_main, tm_cap=1024)
    lora_s = _matmul(h_s, w_lora, tm_cap=1024, name="in_proj_lora")

    slot_pos = jnp.arange(sl, dtype=F32) - 1.0
    pos_s = jnp.where((slot_pos >= 0) & (slot_pos < ls), float(PAST_LEN) + slot_pos, 0.0)
    cos_s, sin_s = _rope_tables(pos_s, RET_HEAD_DIM // 2)
    y_ret_s, s_ret_s = _ret_sample(proj_s, lg, cos_s, sin_s, state_ret[0], nh_ret, ls)

    nc_s = bsz * nh_rw
    nc_pad = -(-nc_s // LANES) * LANES

    def to_chain(t):
        t4 = t.reshape(bsz, sl, nh_rw, RWKV_HEAD_DIM)[:, 1:1 + ls]
        return _pad_chains(jnp.transpose(t4, (1, 3, 0, 2)).reshape(ls, RWKV_HEAD_DIM, nc_s), nc_pad)

    prep = _rwkv_prep(proj_s, lora_s, dr, 4, *rw_params)
    lnw_c = _chain_params(ln_x_w[0], nh_rw, bsz, nc_pad)
    lnb_c = _chain_params(ln_x_b[0], nh_rw, bsz, nc_pad)
    rk_c = _chain_params(r_k[0].reshape(-1), nh_rw, bsz, nc_pad)
    s0_c = jnp.transpose(state_rwkv[0], (2, 3, 0, 1)).reshape(RWKV_HEAD_DIM, RWKV_HEAD_DIM, nc_s)
    o_c, s_c = _rwkv_seq(*[to_chain(t) for t in prep], lnw_c, lnb_c, rk_c, _pad_chains(s0_c, nc_pad))
    o4 = jnp.transpose(o_c[..., :nc_s].reshape(ls, RWKV_HEAD_DIM, bsz, nh_rw), (2, 0, 3, 1)).reshape(bsz, ls, dr)
    o_nat = jnp.pad(o4, ((0, 0), (1, sl - 1 - ls), (0, 0))).reshape(bsz * sl, dr)
    y_rw_s = _gate(o_nat, proj_s, dr, 7)
    s_rw_s = jnp.transpose(s_c[..., :nc_s].reshape(RWKV_HEAD_DIM, RWKV_HEAD_DIM, bsz, nh_rw), (2, 3, 0, 1))

    y_slots = _out_sample(y_ret_s, y_rw_s, w_o_ret, w_o_rw, x_slots, mod_s, final_norm_w)
    y_sample = y_slots[:, 1:1 + ls, :]

    return (y_prompt, y_sample, s_ret_p[None], s_rw_p[None], new_shift_p[None],
            s_ret_s[None], s_rw_s[None], new_shift_s[None])
```

```python
import functools
import math

import jax
import jax.numpy as jnp
from jax import lax
from jax.experimental import pallas as pl
from jax.experimental.pallas import tpu as pltpu

F32 = jnp.float32
BF16 = jnp.bfloat16

RET_HEAD_DIM = 256
RWKV_HEAD_DIM = 64
RET_CHUNK = 128
RWKV_CHUNK = 64
RWKV_GROUP = 4
PAST_LEN = 16384
ROPE_THETA = 10000.0
NORM_EPS = 1e-6
GN_EPS = 1e-5
RWKV_GN_EPS = 64e-5
SAMPLE_SLOTS = 8
LANES = 128
SUBLANES = 8
VMEM_LIMIT_BYTES = 56 * 1024 * 1024


def _params(sem):
    return pltpu.CompilerParams(dimension_semantics=sem, vmem_limit_bytes=VMEM_LIMIT_BYTES)


def _tile(n, cap, align):
    if n <= cap:
        return n
    t = (cap // align) * align
    while t >= align:
        if n % t == 0:
            return t
        t -= align
    return n


def _silu(x):
    return x * jax.nn.sigmoid(x)


def _dot(a, b):
    return jnp.dot(a, b, preferred_element_type=F32)


def _dot_nt(a, b):
    return lax.dot_general(a, b, (((1,), (1,)), ((), ())), preferred_element_type=F32)


def _dot_tn(a, b):
    return lax.dot_general(a, b, (((0,), (0,)), ((), ())), preferred_element_type=F32)


def _adaln_body(c_ref, w_ref, b_ref, o_ref):
    s = _silu(c_ref[...]).astype(BF16)
    o_ref[...] = _dot(s, w_ref[...].astype(BF16)) + b_ref[...]


def _adaln(c, w_ada, b_ada):
    rows, d = c.shape
    n = w_ada.shape[1]
    tn = _tile(n, 768, LANES)
    return pl.pallas_call(
        _adaln_body,
        grid=(n // tn,),
        in_specs=[pl.BlockSpec((rows, d), lambda j: (0, 0)),
                  pl.BlockSpec((d, tn), lambda j: (0, j)),
                  pl.BlockSpec((1, tn), lambda j: (0, j))],
        out_specs=pl.BlockSpec((rows, tn), lambda j: (0, j)),
        out_shape=jax.ShapeDtypeStruct((rows, n), F32),
        compiler_params=_params(("parallel",)),
        name="adaln",
    )(c, w_ada, b_ada.reshape(1, n))


def _modulated(x, nw, shift, scale):
    ms = jnp.mean(x * x, axis=-1, keepdims=True)
    return x * lax.rsqrt(ms + NORM_EPS) * nw * (1.0 + scale) + shift


def _mod_prompt_body(x_ref, mod_ref, nw_ref, h_ref, last_ref):
    m = mod_ref[0]
    h = _modulated(x_ref[0], nw_ref[...], m[0:1], m[1:2])
    h_ref[...] = h.astype(BF16)
    tl = h.shape[0]
    last_ref[0] = h[tl - SUBLANES:tl]


def _mod_prompt(x, mod3, nw):
    b, l, d = x.shape
    tl = _tile(l, 256, 16)
    nt = l // tl
    return pl.pallas_call(
        _mod_prompt_body,
        grid=(b, nt),
        in_specs=[pl.BlockSpec((1, tl, d), lambda bi, i: (bi, i, 0)),
                  pl.BlockSpec((1, 3, d), lambda bi, i: (bi, 0, 0)),
                  pl.BlockSpec((1, d), lambda bi, i: (0, 0))],
        out_specs=[pl.BlockSpec((tl, d), lambda bi, i: (bi * nt + i, 0)),
                   pl.BlockSpec((1, SUBLANES, d), lambda bi, i: (bi, 0, 0))],
        out_shape=[jax.ShapeDtypeStruct((b * l, d), BF16),
                   jax.ShapeDtypeStruct((b, SUBLANES, d), F32)],
        compiler_params=_params(("parallel", "arbitrary")),
        name="modulate_prompt",
    )(x, mod3, nw.reshape(1, d))


def _mod_sample_body(x_ref, prev_ref, mod_ref, nw_ref, h_ref, new_ref):
    x = x_ref[...]
    m = mod_ref[...]
    h = _modulated(x, nw_ref[...], m[:, 0:1, :], m[:, 1:2, :])
    slot = lax.broadcasted_iota(jnp.int32, h.shape, 1)
    hp = prev_ref[...][:, None, :]
    full = jnp.where(slot == 0, hp, h)
    bs, _, d = x.shape
    h_ref[...] = full.reshape(bs * SAMPLE_SLOTS, d).astype(BF16)
    new_ref[...] = h[:, 4, :]


def _mod_sample(x_slots, prev, mod3, nw):
    bsz, _, d = x_slots.shape
    bs = _tile(bsz, 16, SUBLANES)
    return pl.pallas_call(
        _mod_sample_body,
        grid=(bsz // bs,),
        in_specs=[pl.BlockSpec((bs, SAMPLE_SLOTS, d), lambda i: (i, 0, 0)),
                  pl.BlockSpec((bs, d), lambda i: (i, 0)),
                  pl.BlockSpec((bs, 3, d), lambda i: (i, 0, 0)),
                  pl.BlockSpec((1, d), lambda i: (0, 0))],
        out_specs=[pl.BlockSpec((bs * SAMPLE_SLOTS, d), lambda i: (i, 0)),
                   pl.BlockSpec((bs, d), lambda i: (i, 0))],
        out_shape=[jax.ShapeDtypeStruct((bsz * SAMPLE_SLOTS, d), BF16),
                   jax.ShapeDtypeStruct((bsz, d), F32)],
        compiler_params=_params(("parallel",)),
        name="modulate_sample",
    )(x_slots, prev, mod3, nw.reshape(1, d))


def _matmul_body(x_ref, w_ref, o_ref):
    o_ref[...] = _dot(x_ref[...], w_ref[...])


def _matmul(x, w, tm_cap=512, tn_cap=1024, name="in_proj"):
    m, k = x.shape
    n = w.shape[1]
    tm = _tile(m, tm_cap, 16)
    tn = _tile(n, tn_cap, LANES)
    return pl.pallas_call(
        _matmul_body,
        grid=(n // tn, m // tm),
        in_specs=[pl.BlockSpec((tm, k), lambda j, i: (i, 0)),
                  pl.BlockSpec((k, tn), lambda j, i: (0, j))],
        out_specs=pl.BlockSpec((tm, tn), lambda j, i: (i, j)),
        out_shape=jax.ShapeDtypeStruct((m, n), F32),
        compiler_params=_params(("parallel", "parallel")),
        name=name,
    )(x, w)


def _token_shift(cur_ref, carry_ref, mu_ref):
    cur = cur_ref[...]
    tl = cur.shape[0]
    prev = pltpu.roll(cur, 1, 0)
    if carry_ref is not None:
        row = lax.broadcasted_iota(jnp.int32, cur.shape, 0)
        prev = jnp.where(row == 0, carry_ref[0:1, :], prev)
        carry_ref[0:1, :] = cur[tl - 1:tl, :]
    return cur + (prev - cur) * mu_ref[...]


def _decay_and_rate(lo, w0_ref, w2_ref, a0_ref, a2_ref):
    dec = _dot(jnp.tanh(lo).astype(BF16), w2_ref[...])
    z = -(w0_ref[...] + dec)
    softplus = jnp.maximum(z, 0.0) + jnp.log1p(jnp.exp(-jnp.abs(z)))
    w_log = -softplus - 0.5
    a = jax.nn.sigmoid(a0_ref[...] + _dot(lo.astype(BF16), a2_ref[...]))
    return -jnp.exp(w_log), a


def _zero_refs(*refs):
    for r in refs:
        r[...] = jnp.zeros_like(r)


def _split3(x):
    hi = x.astype(BF16)
    r1 = x - hi.astype(F32)
    mid = r1.astype(BF16)
    lo = (r1 - mid.astype(F32)).astype(BF16)
    return hi, mid, lo


def _dot_split3(m, parts):
    return _dot(m, parts[0]) + _dot(m, parts[1]) + _dot(m, parts[2])


def _rwkv_tile(pr_ref, pk_ref, pv_ref, pg_ref, pl_ref, mu_r, mu_k, mu_v, mu_l, w0_ref, w2_ref,
               a0_ref, a2_ref, kk_ref, ka_ref, rk_ref, lnw_ref, lnb_ref, y_ref,
               carries, seq, valid_slots, load_state, store_state):
    c = pr_ref.shape[0]
    hd = RWKV_HEAD_DIM
    gw = RWKV_GROUP * hd
    groups = range(pr_ref.shape[1] // gw)
    nseq = c // seq
    seqs = range(nseq)
    rows_of = [slice(q * seq, (q + 1) * seq) for q in seqs]
    c_r, c_k, c_v, c_l = carries

    r_all = _token_shift(pr_ref, c_r, mu_r)
    kw_all = _token_shift(pk_ref, c_k, mu_k)
    v_all = _token_shift(pv_ref, c_v, mu_v)
    lo = _token_shift(pl_ref, c_l, mu_l)
    logw_all, a_all = _decay_and_rate(lo, w0_ref, w2_ref, a0_ref, a2_ref)
    k_all = kw_all * (1.0 + (a_all - 1.0) * ka_ref[...])
    kkr_all = kw_all * kk_ref[...]
    if valid_slots is not None:
        slot = lax.broadcasted_iota(jnp.int32, (c, 1), 0) % seq
        valid = (slot >= valid_slots[0]) & (slot <= valid_slots[1])
        logw_all = jnp.where(valid, logw_all, 0.0)
        kkr_all = jnp.where(valid, kkr_all, 0.0)
        k_all = jnp.where(valid, k_all, 0.0)

    ti = lax.broadcasted_iota(jnp.int32, (c, c), 0)
    tj = lax.broadcasted_iota(jnp.int32, (c, c), 1)
    same_seq = (ti // seq) == (tj // seq)
    logw_parts = _split3(logw_all)
    cum_all = _dot_split3(((ti >= tj) & same_seq).astype(BF16), logw_parts)
    if nseq > 1:
        cum_end_all = _dot_split3(same_seq.astype(BF16), logw_parts)

    hr = lax.broadcasted_iota(jnp.int32, (gw, gw), 0) // hd
    hc = lax.broadcasted_iota(jnp.int32, (gw, gw), 1) // hd
    headmask = hr == hc
    ones_bd = headmask.astype(BF16)
    tok = lax.broadcasted_iota(jnp.int32, (c, gw), 0)
    src = lax.broadcasted_iota(jnp.int32, (c, gw), 1) % hd
    same = (tok // seq) == (src // seq)
    strict = (tok > src) & same
    incl = (tok >= src) & same

    def bd(x):
        t = jnp.concatenate([x.astype(BF16)] * RWKV_GROUP, axis=0)
        return jnp.where(headmask, t, jnp.zeros_like(t))

    def head_sum(x, exact=False):
        xh = x.astype(BF16)
        if not exact:
            return _dot(xh, ones_bd)
        xl = (x - xh.astype(F32)).astype(BF16)
        return _dot(xh, ones_bd) + _dot(xl, ones_bd)

    sls = [slice(g * gw, (g + 1) * gw) for g in groups]
    r = [r_all[:, sl] for sl in sls]
    k = [k_all[:, sl] for sl in sls]
    v = [v_all[:, sl] for sl in sls]
    cum = [cum_all[:, sl] for sl in sls]
    if nseq > 1:
        cum_end = [cum_end_all[:, sl] for sl in sls]
    else:
        cum_end = [cm[c - 1:c, :] for cm in cum]

    kk_sq = [head_sum(jnp.square(kkr_all[:, sl])) for sl in sls]
    rk_sum = [head_sum(r[g] * k[g] * rk_ref[:, sls[g]], exact=True) for g in groups]
    kkn = [kkr_all[:, sls[g]] / jnp.maximum(jnp.sqrt(kk_sq[g]), 1e-12) for g in groups]
    b = [kkn[g] * a_all[:, sls[g]] for g in groups]
    e_inv = [jnp.exp(-cum[g]) for g in groups]
    at = [-kkn[g] * jnp.exp(cum[g] - logw_all[:, sls[g]]) for g in groups]
    rt = [r[g] * jnp.exp(cum[g]) for g in groups]
    ar = [jnp.concatenate([at[g], rt[g]], axis=0).astype(BF16) for g in groups]
    states = [load_state(g) for g in groups]

    def from_state(g):
        if nseq == 1:
            fs = _dot_nt(ar[g], states[g][0].astype(BF16))
            return fs[:c], fs[c:]
        fa, fr = [], []
        for q in seqs:
            arq = jnp.concatenate([at[g][rows_of[q]], rt[g][rows_of[q]]], axis=0).astype(BF16)
            fs = _dot_nt(arq, states[g][q].astype(BF16))
            fa.append(fs[:seq])
            fr.append(fs[seq:])
        return jnp.concatenate(fa, axis=0), jnp.concatenate(fr, axis=0)

    fstate = [from_state(g) for g in groups]
    s_b = [_dot_nt(ar[g], bd(b[g] * e_inv[g])) for g in groups]
    s_k = [_dot_nt(ar[g], bd(k[g] * e_inv[g])) for g in groups]
    a_ab = [jnp.where(strict, x[:c], 0.0) for x in s_b]
    a_rb = [jnp.where(incl, x[c:], 0.0) for x in s_b]
    a_ak = [jnp.where(strict, x[:c], 0.0) for x in s_k]
    a_rk = [jnp.where(incl, x[c:], 0.0) for x in s_k]

    steps = max(1, int(math.ceil(math.log2(seq))))
    q = a_ab
    pm = a_ab
    q_next = [_dot(q[g].astype(BF16), bd(q[g])) for g in groups] if steps > 1 else None
    for s in range(1, steps):
        q = q_next
        if s < steps - 1:
            res = [_dot(jnp.concatenate([q[g], pm[g]], axis=0).astype(BF16), bd(q[g])) for g in groups]
            q_next = [x[:c] for x in res]
            pm = [pm[g] + q[g] + res[g][c:] for g in groups]
        else:
            pm = [pm[g] + q[g] + _dot(pm[g].astype(BF16), bd(q[g])) for g in groups]

    bd_v = [bd(x) for x in v]
    w_rhs = [fstate[g][0] + _dot(a_ak[g].astype(BF16), bd_v[g]) for g in groups]
    u = [w_rhs[g] + _dot(pm[g].astype(BF16), bd(w_rhs[g])) for g in groups]
    o = [fstate[g][1] + _dot(jnp.concatenate([a_rb[g], a_rk[g]], axis=1).astype(BF16),
                             jnp.concatenate([bd(u[g]), bd_v[g]], axis=0)) for g in groups]
    for g in groups:
        e_end = jnp.exp(cum_end[g] - cum[g])
        b_end = b[g] * e_end
        k_end = k[g] * e_end
        for q in seqs:
            rq = rows_of[q]
            uv = jnp.concatenate([u[g][rq], v[g][rq]], axis=0).astype(BF16)
            bk = jnp.concatenate([b_end[rq], k_end[rq]], axis=0).astype(BF16)
            keep = jnp.exp(cum_end[g][q * seq:q * seq + 1, :]) if nseq > 1 else jnp.exp(cum_end[g])
            store_state(g, q, states[g][q] * keep + _dot_tn(uv, bk))

    mean = [head_sum(x) * (1.0 / hd) for x in o]
    oc = [o[g] - mean[g] for g in groups]
    var = [head_sum(jnp.square(x)) * (1.0 / hd) for x in oc]
    for g in groups:
        sl = sls[g]
        og = oc[g] * lax.rsqrt(var[g] + RWKV_GN_EPS) * lnw_ref[:, sl] + lnb_ref[:, sl]
        y_ref[:, sl] = ((og + rk_sum[g] * v[g]) * _silu(pg_ref[:, sl])).astype(BF16)


def _head_mask(gw, hd):
    hr = lax.broadcasted_iota(jnp.int32, (gw, gw), 0) // hd
    hc = lax.broadcasted_iota(jnp.int32, (gw, gw), 1) // hd
    return hr == hc


def _rwkv_prompt_body(*refs):
    ins, (y_ref, so_ref), (c_r, c_k, c_v, c_l, s_scr) = refs[:18], refs[18:20], refs[20:]
    n = pl.program_id(1)
    hd = RWKV_HEAD_DIM
    gw = RWKV_GROUP * hd
    headmask = _head_mask(gw, hd)

    @pl.when(n == 0)
    def _():
        _zero_refs(c_r, c_k, c_v, c_l, s_scr)

    def load_state(g):
        return [s_scr[g]]

    def store_state(g, q, s):
        s_scr[g] = jnp.where(headmask, s, 0.0)

    _rwkv_tile(*ins, y_ref, (c_r, c_k, c_v, c_l), ins[0].shape[0], None, load_state, store_state)

    @pl.when(n == pl.num_programs(1) - 1)
    def _():
        for g in range(s_scr.shape[0]):
            sg = s_scr[g]
            for j in range(RWKV_GROUP):
                so_ref[0, g * RWKV_GROUP + j] = sg[j * hd:(j + 1) * hd, j * hd:(j + 1) * hd]


def _rwkv_prompt(proj, lora, nb, l, dr, mu_r, mu_k, mu_v, mu_l, w0, w2p, a0, a2p, k_k, k_a, r_k, ln_w, ln_b):
    c = RWKV_CHUNK
    assert l % c == 0 and c == RWKV_HEAD_DIM
    nchunk = l // c
    lp = lora.shape[1]
    nh = dr // RWKV_HEAD_DIM
    gw = RWKV_GROUP * RWKV_HEAD_DIM
    vec = lambda: pl.BlockSpec((1, dr), lambda b, n: (0, 0))
    sect = lambda s: pl.BlockSpec((c, dr), lambda b, n, s=s: (b * nchunk + n, 4 + s))
    return pl.pallas_call(
        _rwkv_prompt_body,
        grid=(nb, nchunk),
        in_specs=[sect(0), sect(1), sect(2), sect(3),
                  pl.BlockSpec((c, lp), lambda b, n: (b * nchunk + n, 0)),
                  vec(), vec(), vec(),
                  pl.BlockSpec((1, lp), lambda b, n: (0, 0)),
                  vec(),
                  pl.BlockSpec((lp, dr), lambda b, n: (0, 0)),
                  vec(),
                  pl.BlockSpec((lp, dr), lambda b, n: (0, 0)),
                  vec(), vec(), vec(), vec(), vec()],
        out_specs=[pl.BlockSpec((c, dr), lambda b, n: (b * nchunk + n, 0)),
                   pl.BlockSpec((1, nh, RWKV_HEAD_DIM, RWKV_HEAD_DIM), lambda b, n: (b, 0, 0, 0))],
        out_shape=[jax.ShapeDtypeStruct((nb * l, dr), BF16),
                   jax.ShapeDtypeStruct((nb, nh, RWKV_HEAD_DIM, RWKV_HEAD_DIM), F32)],
        scratch_shapes=[pltpu.VMEM((SUBLANES, dr), F32), pltpu.VMEM((SUBLANES, dr), F32),
                        pltpu.VMEM((SUBLANES, dr), F32), pltpu.VMEM((SUBLANES, lp), F32),
                        pltpu.VMEM((dr // gw, gw, gw), F32)],
        compiler_params=_params(("parallel", "arbitrary")),
        name="rwkv_chunked",
    )(proj, proj, proj, proj, lora, mu_r, mu_k, mu_v, mu_l, w0, w2p, a0, a2p, k_k, k_a, r_k, ln_w, ln_b)


def _rwkv_sample_body(*refs, n_tok):
    ins, s0_ref, y_ref, so_ref = refs[:18], refs[18], refs[19], refs[20]
    hd = RWKV_HEAD_DIM
    zero = jnp.zeros((hd, hd), F32)

    def load_state(g):
        out = []
        for q in range(s0_ref.shape[0]):
            rows = [jnp.concatenate([s0_ref[q, g * RWKV_GROUP + j] if i == j else zero
                                     for i in range(RWKV_GROUP)], axis=1) for j in range(RWKV_GROUP)]
            out.append(jnp.concatenate(rows, axis=0))
        return out

    def store_state(g, q, s):
        for j in range(RWKV_GROUP):
            so_ref[q, g * RWKV_GROUP + j] = s[j * hd:(j + 1) * hd, j * hd:(j + 1) * hd]

    _rwkv_tile(*ins, y_ref, (None, None, None, None), SAMPLE_SLOTS, (1, n_tok), load_state, store_state)


def _rwkv_sample(proj, lora, s0, n_tok, dr, mu_r, mu_k, mu_v, mu_l, w0, w2p, a0, a2p, k_k, k_a, r_k, ln_w, ln_b):
    c = RWKV_CHUNK
    rows = proj.shape[0]
    assert rows % c == 0 and c == RWKV_HEAD_DIM
    nseq = c // SAMPLE_SLOTS
    lp = lora.shape[1]
    hd = RWKV_HEAD_DIM
    dw = _tile(dr, 1024, RWKV_GROUP * hd)
    nw = dr // dw
    vec = lambda: pl.BlockSpec((1, dw), lambda i, hf: (0, hf))
    sect = lambda s: pl.BlockSpec((c, dw), lambda i, hf, s=s: (i, (4 + s) * nw + hf))
    lora_w = lambda: pl.BlockSpec((lp, dw), lambda i, hf: (0, hf))
    st = pl.BlockSpec((nseq, dw // hd, hd, hd), lambda i, hf: (i, hf, 0, 0))
    return pl.pallas_call(
        functools.partial(_rwkv_sample_body, n_tok=n_tok),
        grid=(rows // c, nw),
        in_specs=[sect(0), sect(1), sect(2), sect(3),
                  pl.BlockSpec((c, lp), lambda i, hf: (i, 0)),
                  vec(), vec(), vec(),
                  pl.BlockSpec((1, lp), lambda i, hf: (0, 0)),
                  vec(), lora_w(), vec(), lora_w(),
                  vec(), vec(), vec(), vec(), vec(), st],
        out_specs=[pl.BlockSpec((c, dw), lambda i, hf: (i, hf)), st],
        out_shape=[jax.ShapeDtypeStruct((rows, dr), BF16),
                   jax.ShapeDtypeStruct(s0.shape, F32)],
        compiler_params=_params(("parallel", "parallel")),
        name="rwkv_chunked_sample",
    )(proj, proj, proj, proj, lora, mu_r, mu_k, mu_v, mu_l, w0, w2p, a0, a2p, k_k, k_a, r_k, ln_w, ln_b, s0)


def _rotary(x, cos, sin):
    half = x.shape[-1] // 2
    x1 = x[:, :half]
    x2 = x[:, half:]
    return jnp.concatenate([x1 * cos - x2 * sin, x1 * sin + x2 * cos], axis=-1)


def _group_norm_rows(o, eps):
    mean = jnp.mean(o, axis=-1, keepdims=True)
    oc = o - mean
    var = jnp.mean(oc * oc, axis=-1, keepdims=True)
    return oc * lax.rsqrt(var + eps)


def _ret_prompt_body(lg_ref, q_ref, k_ref, v_ref, g_ref, cos_ref, sin_ref, y_ref, s_ref):
    n = pl.program_id(1)
    c = q_ref.shape[0]
    hd = RET_HEAD_DIM
    heads = range(q_ref.shape[1] // hd)
    cols = [slice(h * hd, (h + 1) * hd) for h in heads]

    @pl.when(n == 0)
    def _():
        s_ref[...] = jnp.zeros_like(s_ref)

    cos = cos_ref[...]
    sin = sin_ref[...]
    ii = lax.broadcasted_iota(jnp.int32, (c, c), 0)
    jj = lax.broadcasted_iota(jnp.int32, (c, c), 1)
    diff = (ii - jj).astype(F32)
    causal = diff >= 0.0
    dist = jnp.maximum(diff, 0.0)
    ic = lax.broadcasted_iota(jnp.int32, (c, 1), 0).astype(F32)

    lg = [jnp.full((1, 1), lg_ref[h], F32) for h in heads]
    q = [_rotary(q_ref[:, cs], cos, sin).astype(BF16) for cs in cols]
    k = [_rotary(k_ref[:, cs], cos, sin) * (hd ** -0.5) for cs in cols]
    vb = [v_ref[:, cs].astype(BF16) for cs in cols]
    s = [s_ref[0, h] for h in heads]
    scores = [_dot_nt(q[h], k[h].astype(BF16)) * jnp.where(causal, jnp.exp(dist * lg[h]), 0.0) for h in heads]
    cross = [_dot(q[h], s[h].astype(BF16)) * jnp.exp((ic + 1.0) * lg[h]) for h in heads]
    o = [_dot(scores[h].astype(BF16), vb[h]) + cross[h] for h in heads]
    for h in heads:
        kd = (k[h] * jnp.exp((c - 1.0 - ic) * lg[h])).astype(BF16)
        s_ref[0, h] = s[h] * jnp.exp(float(c) * lg[h]) + _dot_tn(kd, vb[h])
    for h in heads:
        y_ref[:, cols[h]] = (_group_norm_rows(o[h], GN_EPS) * _silu(g_ref[:, cols[h]])).astype(BF16)


def _ret_prompt(proj, lg, cos, sin, nb, l, nh):
    hd = RET_HEAD_DIM
    dr = nh * hd
    c = math.gcd(l, RET_CHUNK)
    nchunk = l // c
    sect = lambda s: pl.BlockSpec((c, dr), lambda b, n, s=s: (b * nchunk + n, s))
    tab = pl.BlockSpec((c, hd // 2), lambda b, n: (n, 0))
    return pl.pallas_call(
        _ret_prompt_body,
        grid=(nb, nchunk),
        in_specs=[pl.BlockSpec(memory_space=pltpu.SMEM), sect(0), sect(1), sect(2), sect(3), tab, tab],
        out_specs=[pl.BlockSpec((c, dr), lambda b, n: (b * nchunk + n, 0)),
                   pl.BlockSpec((1, nh, hd, hd), lambda b, n: (b, 0, 0, 0))],
        out_shape=[jax.ShapeDtypeStruct((nb * l, dr), BF16),
                   jax.ShapeDtypeStruct((nb, nh, hd, hd), F32)],
        compiler_params=_params(("parallel", "arbitrary")),
        name="retention_prompt",
    )(lg, proj, proj, proj, proj, cos, sin)


def _ret_sample_body(lg_ref, q_ref, k_ref, v_ref, g_ref, cos_ref, sin_ref, s0_ref, y_ref, s_ref, *, nh, n_tok):
    hd = RET_HEAD_DIM
    bs = s0_ref.shape[0]
    sl = SAMPLE_SLOTS
    cos = cos_ref[...]
    sin = sin_ref[...]
    slot_r = lax.broadcasted_iota(jnp.int32, (sl, 1), 0)
    tok_r = (slot_r - 1).astype(F32)
    valid_r = (slot_r >= 1) & (slot_r <= n_tok)
    ii = lax.broadcasted_iota(jnp.int32, (sl, sl), 0)
    jj = lax.broadcasted_iota(jnp.int32, (sl, sl), 1)
    diff = (ii - jj).astype(F32)
    pair_ok = (diff >= 0.0) & (jj >= 1) & (jj <= n_tok)
    for h in range(nh):
        lg = jnp.full((1, 1), lg_ref[h], F32)
        dmask = jnp.where(pair_ok, jnp.exp(jnp.maximum(diff, 0.0) * lg), 0.0)
        cross_decay = jnp.exp((tok_r + 1.0) * lg)
        key_decay = jnp.where(valid_r, jnp.exp((n_tok - 1.0 - tok_r) * lg), 0.0)
        chunk_decay = jnp.exp(float(n_tok) * lg)
        for b in range(bs):
            rows = slice(b * sl, (b + 1) * sl)
            cols = slice(h * hd, (h + 1) * hd)
            q = _rotary(q_ref[rows, cols], cos, sin)
            k = _rotary(k_ref[rows, cols], cos, sin) * (hd ** -0.5)
            vb = v_ref[rows, cols].astype(BF16)
            s = s0_ref[b, h]
            qb = q.astype(BF16)
            scores = _dot_nt(qb, k.astype(BF16)) * dmask
            inner = _dot(scores.astype(BF16), vb)
            cross = _dot(qb, s.astype(BF16)) * cross_decay
            kd_t = (k * key_decay).T.astype(BF16)
            s_ref[b, h] = s * chunk_decay + _dot(kd_t, vb)
            o = inner + cross
            y_ref[rows, cols] = (_group_norm_rows(o, GN_EPS) * _silu(g_ref[rows, cols])).astype(BF16)


def _ret_sample(proj, lg, cos, sin, s0, nh, n_tok):
    bsz = s0.shape[0]
    hd = RET_HEAD_DIM
    dr = nh * hd
    bs = 2
    rows = bs * SAMPLE_SLOTS
    sect = lambda s: pl.BlockSpec((rows, dr), lambda i, s=s: (i, s))
    tab = pl.BlockSpec((SAMPLE_SLOTS, hd // 2), lambda i: (0, 0))
    st = pl.BlockSpec((bs, nh, hd, hd), lambda i: (i, 0, 0, 0))
    return pl.pallas_call(
        functools.partial(_ret_sample_body, nh=nh, n_tok=n_tok),
        grid=(bsz // bs,),
        in_specs=[pl.BlockSpec(memory_space=pltpu.SMEM), sect(0), sect(1), sect(2), sect(3), tab, tab, st],
        out_specs=[pl.BlockSpec((rows, dr), lambda i: (i, 0)), st],
        out_shape=[jax.ShapeDtypeStruct((bsz * SAMPLE_SLOTS, dr), BF16),
                   jax.ShapeDtypeStruct(s0.shape, F32)],
        compiler_params=_params(("parallel",)),
        name="retention_sample",
    )(lg, proj, proj, proj, proj, cos, sin, s0)


def _out_core(yr_ref, yw_ref, w1_ref, w2_ref, x, gate, fw):
    y = _dot(yr_ref[...], w1_ref[...]) + _dot(yw_ref[...], w2_ref[...])
    xn = x + gate * y
    ms = jnp.mean(xn * xn, axis=-1, keepdims=True)
    return xn * lax.rsqrt(ms + NORM_EPS) * fw


def _out_prompt_body(yr_ref, yw_ref, w1_ref, w2_ref, x_ref, mod_ref, fw_ref, o_ref):
    o_ref[0] = _out_core(yr_ref, yw_ref, w1_ref, w2_ref, x_ref[0], mod_ref[0][2:3], fw_ref[...])


def _out_prompt(y_ret, y_rw, w1, w2, x, mod3, fw):
    b, l, d = x.shape
    dr = w1.shape[0]
    tl = _tile(l, 256, 16)
    nt = l // tl
    wspec = pl.BlockSpec((dr, d), lambda bi, i: (0, 0))
    yspec = pl.BlockSpec((tl, dr), lambda bi, i: (bi * nt + i, 0))
    return pl.pallas_call(
        _out_prompt_body,
        grid=(b, nt),
        in_specs=[yspec, yspec, wspec, wspec,
                  pl.BlockSpec((1, tl, d), lambda bi, i: (bi, i, 0)),
                  pl.BlockSpec((1, 3, d), lambda bi, i: (bi, 0, 0)),
                  pl.BlockSpec((1, d), lambda bi, i: (0, 0))],
        out_specs=pl.BlockSpec((1, tl, d), lambda bi, i: (bi, i, 0)),
        out_shape=jax.ShapeDtypeStruct((b, l, d), F32),
        compiler_params=_params(("parallel", "parallel")),
        name="out_proj_prompt",
    )(y_ret, y_rw, w1, w2, x, mod3, fw.reshape(1, d))


def _out_sample_body(yr_ref, yw_ref, w1_ref, w2_ref, x_ref, mod_ref, fw_ref, o_ref):
    bs, sl, d = x_ref.shape
    gate = jnp.broadcast_to(mod_ref[...][:, 2:3, :], (bs, sl, d)).reshape(bs * sl, d)
    x = x_ref[...].reshape(bs * sl, d)
    o = _out_core(yr_ref, yw_ref, w1_ref, w2_ref, x, gate, fw_ref[...])
    o_ref[...] = o.reshape(bs, sl, d)


def _out_sample(y_ret, y_rw, w1, w2, x_slots, mod3, fw):
    bsz, sl, d = x_slots.shape
    dr = w1.shape[0]
    bs = _tile(bsz, 32, SUBLANES)
    wspec = pl.BlockSpec((dr, d), lambda i: (0, 0))
    yspec = pl.BlockSpec((bs * sl, dr), lambda i: (i, 0))
    xspec = pl.BlockSpec((bs, sl, d), lambda i: (i, 0, 0))
    return pl.pallas_call(
        _out_sample_body,
        grid=(bsz // bs,),
        in_specs=[yspec, yspec, wspec, wspec, xspec,
                  pl.BlockSpec((bs, 3, d), lambda i: (i, 0, 0)),
                  pl.BlockSpec((1, d), lambda i: (0, 0))],
        out_specs=xspec,
        out_shape=jax.ShapeDtypeStruct((bsz, sl, d), F32),
        compiler_params=_params(("parallel",)),
        name="out_proj_sample",
    )(y_ret, y_rw, w1, w2, x_slots, mod3, fw.reshape(1, d))


def _rope_tables(pos, half):
    inv_freq = ROPE_THETA ** (-jnp.arange(half, dtype=F32) / half)
    ang = pos[:, None] * inv_freq[None, :]
    return jnp.cos(ang), jnp.sin(ang)


def kernel(x_prompt, x_sample, c_prompt, c_sample, state_ret, state_rwkv, state_shift, norm_w, w_ada,
           b_ada, w_in, mu_shift, w0_decay, w2_decay, a0, a2, k_k, k_a, r_k, ln_x_w, ln_x_b, w_out,
           final_norm_w):
    depth = w_in.shape[0]
    assert depth == 1, "single-layer trunk"
    bp, lp, d = x_prompt.shape
    bsz, ls, _ = x_sample.shape
    assert ls == 4, "sample path packs 4 tokens into slots 1..4"
    dr = d
    nh_ret = dr // RET_HEAD_DIM
    n_main = 8 * dr
    lora = w2_decay.shape[1]
    lora_pad = -(-2 * lora // LANES) * LANES

    w_in0 = w_in[0]
    w_main = w_in0[:, :n_main].astype(BF16)
    w_lora = jnp.pad(w_in0[:, n_main:], ((0, 0), (0, lora_pad - 2 * lora))).astype(BF16)
    w_o = w_out[0].astype(BF16)
    w_o_ret, w_o_rw = w_o[:dr], w_o[dr:]
    mu = mu_shift[0]
    row = lambda p: p.reshape(1, -1)
    mu_r, mu_k, mu_v = row(mu[0:dr]), row(mu[dr:2 * dr]), row(mu[2 * dr:3 * dr])
    mu_l = jnp.pad(mu[3 * dr:], (0, lora_pad - 2 * lora)).reshape(1, lora_pad)
    w2p = jnp.pad(w2_decay[0], ((0, lora_pad - lora), (0, 0))).astype(BF16)
    a2p = jnp.pad(a2[0], ((lora, lora_pad - 2 * lora), (0, 0))).astype(BF16)
    lg = jnp.log1p(-jnp.exp2(-5.0 - jnp.arange(nh_ret, dtype=F32)))
    rw_params = (mu_r, mu_k, mu_v, mu_l, row(w0_decay[0]), w2p, row(a0[0]), a2p, row(k_k[0]), row(k_a[0]),
                 row(r_k[0]), row(ln_x_w[0]), row(ln_x_b[0]))

    n_c = bp + bsz
    n_c_pad = -(-n_c // SUBLANES) * SUBLANES
    c_all = jnp.pad(jnp.concatenate([c_prompt, c_sample], axis=0), ((0, n_c_pad - n_c), (0, 0)))
    mod3 = _adaln(c_all, w_ada[0], b_ada[0]).reshape(n_c_pad, 3, d)
    mod_p, mod_s = mod3[:bp], mod3[bp:bp + bsz]

    h_p, last_p = _mod_prompt(x_prompt, mod_p, norm_w[0])
    new_shift_p = last_p[:, SUBLANES - 1, :]
    proj_p = _matmul(h_p, w_main)
    lora_p = _matmul(h_p, w_lora, name="in_proj_lora")

    cos_p, sin_p = _rope_tables(jnp.arange(lp, dtype=F32), RET_HEAD_DIM // 2)
    y_ret_p, s_ret_p = _ret_prompt(proj_p, lg, cos_p, sin_p, bp, lp, nh_ret)
    y_rw_p, s_rw_p = _rwkv_prompt(proj_p, lora_p, bp, lp, dr, *rw_params)
    y_prompt = _out_prompt(y_ret_p, y_rw_p, w_o_ret, w_o_rw, x_prompt, mod_p, final_norm_w)

    sl = SAMPLE_SLOTS
    x_slots = jnp.pad(x_sample, ((0, 0), (1, sl - 1 - ls), (0, 0)))
    h_s, new_shift_s = _mod_sample(x_slots, state_shift[0], mod_s, norm_w[0])
    proj_s = _matmul(h_s, w_main, tm_cap=1024)
    lora_s = _matmul(h_s, w_lora, tm_cap=1024, name="in_proj_lora")

    slot_pos = jnp.arange(sl, dtype=F32) - 1.0
    pos_s = jnp.where((slot_pos >= 0) & (slot_pos < ls), float(PAST_LEN) + slot_pos, 0.0)
    cos_s, sin_s = _rope_tables(pos_s, RET_HEAD_DIM // 2)
    y_ret_s, s_ret_s = _ret_sample(proj_s, lg, cos_s, sin_s, state_ret[0], nh_ret, ls)
    y_rw_s, s_rw_s = _rwkv_sample(proj_s, lora_s, state_rwkv[0], ls, dr, *rw_params)
    y_slots = _out_sample(y_ret_s, y_rw_s, w_o_ret, w_o_rw, x_slots, mod_s, final_norm_w)
    y_sample = y_slots[:, 1:1 + ls, :]

    return (y_prompt, y_sample, s_ret_p[None], s_rw_p[None], new_shift_p[None],
            s_ret_s[None], s_rw_s[None], new_shift_s[None])
```

```python
import functools
import math

import jax
import jax.numpy as jnp
from jax import lax
from jax.experimental import pallas as pl
from jax.experimental.pallas import tpu as pltpu

F32 = jnp.float32
BF16 = jnp.bfloat16

RET_HEAD_DIM = 256
RWKV_HEAD_DIM = 64
RET_CHUNK = 128
RWKV_CHUNK = 64
RWKV_GROUP = 4
PAST_LEN = 16384
ROPE_THETA = 10000.0
NORM_EPS = 1e-6
GN_EPS = 1e-5
RWKV_GN_EPS = 64e-5
SAMPLE_SLOTS = 8
LANES = 128
SUBLANES = 8
VMEM_LIMIT_BYTES = 56 * 1024 * 1024


def _params(sem):
    return pltpu.CompilerParams(dimension_semantics=sem, vmem_limit_bytes=VMEM_LIMIT_BYTES)


def _tile(n, cap, align):
    if n <= cap:
        return n
    t = (cap // align) * align
    while t >= align:
        if n % t == 0:
            return t
        t -= align
    return n


def _silu(x):
    return x * jax.nn.sigmoid(x)


def _dot(a, b):
    return jnp.dot(a, b, preferred_element_type=F32)


def _dot_nt(a, b):
    return lax.dot_general(a, b, (((1,), (1,)), ((), ())), preferred_element_type=F32)


def _dot_tn(a, b):
    return lax.dot_general(a, b, (((0,), (0,)), ((), ())), preferred_element_type=F32)


def _adaln_body(c_ref, w_ref, b_ref, o_ref):
    s = _silu(c_ref[...]).astype(BF16)
    o_ref[...] = _dot(s, w_ref[...].astype(BF16)) + b_ref[...]


def _adaln(c, w_ada, b_ada):
    rows, d = c.shape
    n = w_ada.shape[1]
    tn = _tile(n, 768, LANES)
    return pl.pallas_call(
        _adaln_body,
        grid=(n // tn,),
        in_specs=[pl.BlockSpec((rows, d), lambda j: (0, 0)),
                  pl.BlockSpec((d, tn), lambda j: (0, j)),
                  pl.BlockSpec((1, tn), lambda j: (0, j))],
        out_specs=pl.BlockSpec((rows, tn), lambda j: (0, j)),
        out_shape=jax.ShapeDtypeStruct((rows, n), F32),
        compiler_params=_params(("parallel",)),
        name="adaln",
    )(c, w_ada, b_ada.reshape(1, n))


def _modulated(x, nw, shift, scale):
    ms = jnp.mean(x * x, axis=-1, keepdims=True)
    return x * lax.rsqrt(ms + NORM_EPS) * nw * (1.0 + scale) + shift


def _modulate_body(xp_ref, modp_ref, xs_ref, prev_ref, mods_ref, nw_ref, h_ref, last_ref, new_ref, *, n_prompt):
    s = pl.program_id(0)

    @pl.when(s < n_prompt)
    def _():
        m = modp_ref[0]
        h = _modulated(xp_ref[0], nw_ref[...], m[0:1], m[1:2])
        h_ref[...] = h.astype(BF16)
        tl = h.shape[0]
        last_ref[0] = h[tl - SUBLANES:tl]

    @pl.when(s >= n_prompt)
    def _():
        x = xs_ref[...]
        m = mods_ref[...]
        h = _modulated(x, nw_ref[...], m[:, 0:1, :], m[:, 1:2, :])
        slot = lax.broadcasted_iota(jnp.int32, h.shape, 1)
        full = jnp.where(slot == 0, prev_ref[...][:, None, :], h)
        bs, sl, d = x.shape
        h_ref[...] = full.reshape(bs * sl, d).astype(BF16)
        new_ref[...] = h[:, 4, :]


def _modulate(x_prompt, mod_p, x_slots, prev, mod_s, nw):
    b, l, d = x_prompt.shape
    bsz, sl, _ = x_slots.shape
    tl = _tile(l, min(256, bsz * sl), 16)
    nt = l // tl
    bs = tl // sl
    assert bsz % bs == 0 and bs % SUBLANES == 0
    n_prompt = b * nt
    n_sample = bsz // bs
    samp = lambda s: jnp.maximum(s - n_prompt, 0)
    pb = lambda s: jnp.minimum(s // nt, b - 1)
    return pl.pallas_call(
        functools.partial(_modulate_body, n_prompt=n_prompt),
        grid=(n_prompt + n_sample,),
        in_specs=[pl.BlockSpec((1, tl, d), lambda s: (pb(s), jnp.where(s < n_prompt, s % nt, nt - 1), 0)),
                  pl.BlockSpec((1, 3, d), lambda s: (pb(s), 0, 0)),
                  pl.BlockSpec((bs, sl, d), lambda s: (samp(s), 0, 0)),
                  pl.BlockSpec((bs, d), lambda s: (samp(s), 0)),
                  pl.BlockSpec((bs, 3, d), lambda s: (samp(s), 0, 0)),
                  pl.BlockSpec((1, d), lambda s: (0, 0))],
        out_specs=[pl.BlockSpec((tl, d), lambda s: (s, 0)),
                   pl.BlockSpec((1, SUBLANES, d), lambda s: (pb(s), 0, 0)),
                   pl.BlockSpec((bs, d), lambda s: (samp(s), 0))],
        out_shape=[jax.ShapeDtypeStruct((b * l + bsz * sl, d), BF16),
                   jax.ShapeDtypeStruct((b, SUBLANES, d), F32),
                   jax.ShapeDtypeStruct((bsz, d), F32)],
        compiler_params=_params(("arbitrary",)),
        name="modulate",
    )(x_prompt, mod_p, x_slots, prev, mod_s, nw.reshape(1, d))


def _in_proj_body(x_ref, wt_ref, o_ref, w_scr):
    @pl.when(pl.program_id(1) == 0)
    def _():
        w_scr[...] = wt_ref[...].astype(BF16)

    o_ref[...] = _dot_nt(x_ref[...], w_scr[...])


def _in_proj(x, wt, n, name="in_proj"):
    m, k = x.shape
    tm = _tile(m, 512, 16)
    tn = _tile(n, 1024, LANES)
    return pl.pallas_call(
        _in_proj_body,
        grid=(n // tn, m // tm),
        in_specs=[pl.BlockSpec((tm, k), lambda j, i: (i, 0)),
                  pl.BlockSpec((tn, k), lambda j, i: (j, 0))],
        out_specs=pl.BlockSpec((tm, tn), lambda j, i: (i, j)),
        out_shape=jax.ShapeDtypeStruct((m, n), F32),
        scratch_shapes=[pltpu.VMEM((tn, k), BF16)],
        compiler_params=_params(("arbitrary", "arbitrary")),
        name=name,
    )(x, wt)


def _token_shift(cur_ref, carry_ref, mu_ref):
    cur = cur_ref[...]
    tl = cur.shape[0]
    prev = pltpu.roll(cur, 1, 0)
    if carry_ref is not None:
        row = lax.broadcasted_iota(jnp.int32, cur.shape, 0)
        prev = jnp.where(row == 0, carry_ref[0:1, :], prev)
        carry_ref[0:1, :] = cur[tl - 1:tl, :]
    return cur + (prev - cur) * mu_ref[...]


def _decay_and_rate(lo, w0_ref, w2_ref, a0_ref, a2_ref):
    dec = _dot(jnp.tanh(lo).astype(BF16), w2_ref[...])
    z = -(w0_ref[...] + dec)
    softplus = jnp.maximum(z, 0.0) + jnp.log1p(jnp.exp(-jnp.abs(z)))
    w_log = -softplus - 0.5
    a = jax.nn.sigmoid(a0_ref[...] + _dot(lo.astype(BF16), a2_ref[...]))
    return -jnp.exp(w_log), a


def _zero_refs(*refs):
    for r in refs:
        r[...] = jnp.zeros_like(r)


def _split3(x):
    hi = x.astype(BF16)
    r1 = x - hi.astype(F32)
    mid = r1.astype(BF16)
    lo = (r1 - mid.astype(F32)).astype(BF16)
    return hi, mid, lo


def _dot_split3(m, parts):
    return _dot(m, parts[0]) + _dot(m, parts[1]) + _dot(m, parts[2])


def _rwkv_tile(pr_ref, pk_ref, pv_ref, pg_ref, pl_ref, mu_r, mu_k, mu_v, mu_l, w0_ref, w2_ref,
               a0_ref, a2_ref, kk_ref, ka_ref, rk_ref, lnw_ref, lnb_ref, y_ref,
               carries, seq, valid_slots, load_state, store_state):
    c = pr_ref.shape[0]
    hd = RWKV_HEAD_DIM
    gw = RWKV_GROUP * hd
    groups = range(pr_ref.shape[1] // gw)
    nseq = c // seq
    seqs = range(nseq)
    rows_of = [slice(q * seq, (q + 1) * seq) for q in seqs]
    c_r, c_k, c_v, c_l = carries

    r_all = _token_shift(pr_ref, c_r, mu_r)
    kw_all = _token_shift(pk_ref, c_k, mu_k)
    v_all = _token_shift(pv_ref, c_v, mu_v)
    lo = _token_shift(pl_ref, c_l, mu_l)
    logw_all, a_all = _decay_and_rate(lo, w0_ref, w2_ref, a0_ref, a2_ref)
    k_all = kw_all * (1.0 + (a_all - 1.0) * ka_ref[...])
    kkr_all = kw_all * kk_ref[...]
    if valid_slots is not None:
        slot = lax.broadcasted_iota(jnp.int32, (c, 1), 0) % seq
        valid = (slot >= valid_slots[0]) & (slot <= valid_slots[1])
        logw_all = jnp.where(valid, logw_all, 0.0)
        kkr_all = jnp.where(valid, kkr_all, 0.0)
        k_all = jnp.where(valid, k_all, 0.0)

    ti = lax.broadcasted_iota(jnp.int32, (c, c), 0)
    tj = lax.broadcasted_iota(jnp.int32, (c, c), 1)
    same_seq = (ti // seq) == (tj // seq)
    logw_parts = _split3(logw_all)
    cum_all = _dot_split3(((ti >= tj) & same_seq).astype(BF16), logw_parts)
    if nseq > 1:
        cum_end_all = _dot_split3(same_seq.astype(BF16), logw_parts)

    hr = lax.broadcasted_iota(jnp.int32, (gw, gw), 0) // hd
    hc = lax.broadcasted_iota(jnp.int32, (gw, gw), 1) // hd
    headmask = hr == hc
    ones_bd = headmask.astype(BF16)
    tok = lax.broadcasted_iota(jnp.int32, (c, gw), 0)
    src = lax.broadcasted_iota(jnp.int32, (c, gw), 1) % hd
    same = (tok // seq) == (src // seq)
    strict = (tok > src) & same
    incl = (tok >= src) & same

    def bd(x):
        t = jnp.concatenate([x.astype(BF16)] * RWKV_GROUP, axis=0)
        return jnp.where(headmask, t, jnp.zeros_like(t))

    def head_sum(x, exact=False):
        xh = x.astype(BF16)
        if not exact:
            return _dot(xh, ones_bd)
        xl = (x - xh.astype(F32)).astype(BF16)
        return _dot(xh, ones_bd) + _dot(xl, ones_bd)

    sls = [slice(g * gw, (g + 1) * gw) for g in groups]
    r = [r_all[:, sl] for sl in sls]
    k = [k_all[:, sl] for sl in sls]
    v = [v_all[:, sl] for sl in sls]
    cum = [cum_all[:, sl] for sl in sls]
    if nseq > 1:
        cum_end = [cum_end_all[:, sl] for sl in sls]
    else:
        cum_end = [cm[c - 1:c, :] for cm in cum]

    kk_sq = [head_sum(jnp.square(kkr_all[:, sl])) for sl in sls]
    rk_sum = [head_sum(r[g] * k[g] * rk_ref[:, sls[g]], exact=True) for g in groups]
    kkn = [kkr_all[:, sls[g]] / jnp.maximum(jnp.sqrt(kk_sq[g]), 1e-12) for g in groups]
    b = [kkn[g] * a_all[:, sls[g]] for g in groups]
    e_inv = [jnp.exp(-cum[g]) for g in groups]
    at = [-kkn[g] * jnp.exp(cum[g] - logw_all[:, sls[g]]) for g in groups]
    rt = [r[g] * jnp.exp(cum[g]) for g in groups]
    ar = [jnp.concatenate([at[g], rt[g]], axis=0).astype(BF16) for g in groups]
    states = [load_state(g) for g in groups]

    def from_state(g):
        if nseq == 1:
            fs = _dot_nt(ar[g], states[g][0].astype(BF16))
            return fs[:c], fs[c:]
        fa, fr = [], []
        for q in seqs:
            arq = jnp.concatenate([at[g][rows_of[q]], rt[g][rows_of[q]]], axis=0).astype(BF16)
            fs = _dot_nt(arq, states[g][q].astype(BF16))
            fa.append(fs[:seq])
            fr.append(fs[seq:])
        return jnp.concatenate(fa, axis=0), jnp.concatenate(fr, axis=0)

    fstate = [from_state(g) for g in groups]
    s_b = [_dot_nt(ar[g], bd(b[g] * e_inv[g])) for g in groups]
    s_k = [_dot_nt(ar[g], bd(k[g] * e_inv[g])) for g in groups]
    a_ab = [jnp.where(strict, x[:c], 0.0) for x in s_b]
    a_rb = [jnp.where(incl, x[c:], 0.0) for x in s_b]
    a_ak = [jnp.where(strict, x[:c], 0.0) for x in s_k]
    a_rk = [jnp.where(incl, x[c:], 0.0) for x in s_k]

    steps = max(1, int(math.ceil(math.log2(seq))))
    q = a_ab
    pm = a_ab
    q_next = [_dot(q[g].astype(BF16), bd(q[g])) for g in groups] if steps > 1 else None
    for s in range(1, steps):
        q = q_next
        if s < steps - 1:
            res = [_dot(jnp.concatenate([q[g], pm[g]], axis=0).astype(BF16), bd(q[g])) for g in groups]
            q_next = [x[:c] for x in res]
            pm = [pm[g] + q[g] + res[g][c:] for g in groups]
        else:
            pm = [pm[g] + q[g] + _dot(pm[g].astype(BF16), bd(q[g])) for g in groups]

    bd_v = [bd(x) for x in v]
    w_rhs = [fstate[g][0] + _dot(a_ak[g].astype(BF16), bd_v[g]) for g in groups]
    u = [w_rhs[g] + _dot(pm[g].astype(BF16), bd(w_rhs[g])) for g in groups]
    o = [fstate[g][1] + _dot(jnp.concatenate([a_rb[g], a_rk[g]], axis=1).astype(BF16),
                             jnp.concatenate([bd(u[g]), bd_v[g]], axis=0)) for g in groups]
    for g in groups:
        e_end = jnp.exp(cum_end[g] - cum[g])
        b_end = b[g] * e_end
        k_end = k[g] * e_end
        for q in seqs:
            rq = rows_of[q]
            uv = jnp.concatenate([u[g][rq], v[g][rq]], axis=0).astype(BF16)
            bk = jnp.concatenate([b_end[rq], k_end[rq]], axis=0).astype(BF16)
            keep = jnp.exp(cum_end[g][q * seq:q * seq + 1, :]) if nseq > 1 else jnp.exp(cum_end[g])
            store_state(g, q, states[g][q] * keep + _dot_tn(uv, bk))

    mean = [head_sum(x) * (1.0 / hd) for x in o]
    oc = [o[g] - mean[g] for g in groups]
    var = [head_sum(jnp.square(x)) * (1.0 / hd) for x in oc]
    for g in groups:
        sl = sls[g]
        og = oc[g] * lax.rsqrt(var[g] + RWKV_GN_EPS) * lnw_ref[:, sl] + lnb_ref[:, sl]
        y_ref[:, sl] = ((og + rk_sum[g] * v[g]) * _silu(pg_ref[:, sl])).astype(BF16)


def _head_mask(gw, hd):
    hr = lax.broadcasted_iota(jnp.int32, (gw, gw), 0) // hd
    hc = lax.broadcasted_iota(jnp.int32, (gw, gw), 1) // hd
    return hr == hc


def _rwkv_prompt_body(*refs):
    ins, (y_ref, so_ref), (c_r, c_k, c_v, c_l, s_scr) = refs[:18], refs[18:20], refs[20:]
    n = pl.program_id(1)
    hd = RWKV_HEAD_DIM
    gw = RWKV_GROUP * hd
    headmask = _head_mask(gw, hd)

    @pl.when(n == 0)
    def _():
        _zero_refs(c_r, c_k, c_v, c_l, s_scr)

    def load_state(g):
        return [s_scr[g]]

    def store_state(g, q, s):
        s_scr[g] = jnp.where(headmask, s, 0.0)

    _rwkv_tile(*ins, y_ref, (c_r, c_k, c_v, c_l), ins[0].shape[0], None, load_state, store_state)

    @pl.when(n == pl.num_programs(1) - 1)
    def _():
        for g in range(s_scr.shape[0]):
            sg = s_scr[g]
            for j in range(RWKV_GROUP):
                so_ref[0, g * RWKV_GROUP + j] = sg[j * hd:(j + 1) * hd, j * hd:(j + 1) * hd]


def _rwkv_prompt(proj, lora, nb, l, dr, mu_r, mu_k, mu_v, mu_l, w0, w2p, a0, a2p, k_k, k_a, r_k, ln_w, ln_b):
    c = RWKV_CHUNK
    assert l % c == 0 and c == RWKV_HEAD_DIM
    nchunk = l // c
    lp = lora.shape[1]
    nh = dr // RWKV_HEAD_DIM
    gw = RWKV_GROUP * RWKV_HEAD_DIM
    vec = lambda: pl.BlockSpec((1, dr), lambda b, n: (0, 0))
    sect = lambda s: pl.BlockSpec((c, dr), lambda b, n, s=s: (b * nchunk + n, 4 + s))
    return pl.pallas_call(
        _rwkv_prompt_body,
        grid=(nb, nchunk),
        in_specs=[sect(0), sect(1), sect(2), sect(3),
                  pl.BlockSpec((c, lp), lambda b, n: (b * nchunk + n, 0)),
                  vec(), vec(), vec(),
                  pl.BlockSpec((1, lp), lambda b, n: (0, 0)),
                  vec(),
                  pl.BlockSpec((lp, dr), lambda b, n: (0, 0)),
                  vec(),
                  pl.BlockSpec((lp, dr), lambda b, n: (0, 0)),
                  vec(), vec(), vec(), vec(), vec()],
        out_specs=[pl.BlockSpec((c, dr), lambda b, n: (b * nchunk + n, 0)),
                   pl.BlockSpec((1, nh, RWKV_HEAD_DIM, RWKV_HEAD_DIM), lambda b, n: (b, 0, 0, 0))],
        out_shape=[jax.ShapeDtypeStruct((nb * l, dr), BF16),
                   jax.ShapeDtypeStruct((nb, nh, RWKV_HEAD_DIM, RWKV_HEAD_DIM), F32)],
        scratch_shapes=[pltpu.VMEM((SUBLANES, dr), F32), pltpu.VMEM((SUBLANES, dr), F32),
                        pltpu.VMEM((SUBLANES, dr), F32), pltpu.VMEM((SUBLANES, lp), F32),
                        pltpu.VMEM((dr // gw, gw, gw), F32)],
        compiler_params=_params(("parallel", "arbitrary")),
        name="rwkv_chunked",
    )(proj, proj, proj, proj, lora, mu_r, mu_k, mu_v, mu_l, w0, w2p, a0, a2p, k_k, k_a, r_k, ln_w, ln_b)


def _rwkv_sample_body(*refs, n_tok):
    ins, s0_ref, y_ref, so_ref = refs[:18], refs[18], refs[19], refs[20]
    hd = RWKV_HEAD_DIM
    zero = jnp.zeros((hd, hd), F32)

    def load_state(g):
        out = []
        for q in range(s0_ref.shape[0]):
            rows = [jnp.concatenate([s0_ref[q, g * RWKV_GROUP + j] if i == j else zero
                                     for i in range(RWKV_GROUP)], axis=1) for j in range(RWKV_GROUP)]
            out.append(jnp.concatenate(rows, axis=0))
        return out

    def store_state(g, q, s):
        for j in range(RWKV_GROUP):
            so_ref[q, g * RWKV_GROUP + j] = s[j * hd:(j + 1) * hd, j * hd:(j + 1) * hd]

    _rwkv_tile(*ins, y_ref, (None, None, None, None), SAMPLE_SLOTS, (1, n_tok), load_state, store_state)


def _rwkv_sample(proj, lora, row0, s0, n_tok, dr, mu_r, mu_k, mu_v, mu_l, w0, w2p, a0, a2p, k_k, k_a,
                 r_k, ln_w, ln_b):
    c = RWKV_CHUNK
    rows = s0.shape[0] * SAMPLE_SLOTS
    assert rows % c == 0 and row0 % c == 0 and c == RWKV_HEAD_DIM
    t0 = row0 // c
    nseq = c // SAMPLE_SLOTS
    lp = lora.shape[1]
    hd = RWKV_HEAD_DIM
    dw = _tile(dr, 1024, RWKV_GROUP * hd)
    nw = dr // dw
    vec = lambda: pl.BlockSpec((1, dw), lambda i, hf: (0, hf))
    sect = lambda s: pl.BlockSpec((c, dw), lambda i, hf, s=s: (t0 + i, (4 + s) * nw + hf))
    lora_w = lambda: pl.BlockSpec((lp, dw), lambda i, hf: (0, hf))
    st = pl.BlockSpec((nseq, dw // hd, hd, hd), lambda i, hf: (i, hf, 0, 0))
    return pl.pallas_call(
        functools.partial(_rwkv_sample_body, n_tok=n_tok),
        grid=(rows // c, nw),
        in_specs=[sect(0), sect(1), sect(2), sect(3),
                  pl.BlockSpec((c, lp), lambda i, hf: (t0 + i, 0)),
                  vec(), vec(), vec(),
                  pl.BlockSpec((1, lp), lambda i, hf: (0, 0)),
                  vec(), lora_w(), vec(), lora_w(),
                  vec(), vec(), vec(), vec(), vec(), st],
        out_specs=[pl.BlockSpec((c, dw), lambda i, hf: (i, hf)), st],
        out_shape=[jax.ShapeDtypeStruct((rows, dr), BF16),
                   jax.ShapeDtypeStruct(s0.shape, F32)],
        compiler_params=_params(("parallel", "parallel")),
        name="rwkv_chunked_sample",
    )(proj, proj, proj, proj, lora, mu_r, mu_k, mu_v, mu_l, w0, w2p, a0, a2p, k_k, k_a, r_k, ln_w, ln_b, s0)


def _rotary(x, cos, sin):
    half = x.shape[-1] // 2
    x1 = x[:, :half]
    x2 = x[:, half:]
    return jnp.concatenate([x1 * cos - x2 * sin, x1 * sin + x2 * cos], axis=-1)


def _group_norm_rows(o, eps):
    mean = jnp.mean(o, axis=-1, keepdims=True)
    oc = o - mean
    var = jnp.mean(oc * oc, axis=-1, keepdims=True)
    return oc * lax.rsqrt(var + eps)


def _ret_prompt_body(lg_ref, q_ref, k_ref, v_ref, g_ref, cos_ref, sin_ref, y_ref, s_ref):
    n = pl.program_id(1)
    c = q_ref.shape[0]
    hd = RET_HEAD_DIM
    heads = range(q_ref.shape[1] // hd)
    cols = [slice(h * hd, (h + 1) * hd) for h in heads]

    @pl.when(n == 0)
    def _():
        s_ref[...] = jnp.zeros_like(s_ref)

    cos = cos_ref[...]
    sin = sin_ref[...]
    ii = lax.broadcasted_iota(jnp.int32, (c, c), 0)
    jj = lax.broadcasted_iota(jnp.int32, (c, c), 1)
    diff = (ii - jj).astype(F32)
    causal = diff >= 0.0
    dist = jnp.maximum(diff, 0.0)
    ic = lax.broadcasted_iota(jnp.int32, (c, 1), 0).astype(F32)

    lg = [jnp.full((1, 1), lg_ref[h], F32) for h in heads]
    q = [_rotary(q_ref[:, cs], cos, sin).astype(BF16) for cs in cols]
    k = [_rotary(k_ref[:, cs], cos, sin) * (hd ** -0.5) for cs in cols]
    vb = [v_ref[:, cs].astype(BF16) for cs in cols]
    s = [s_ref[0, h] for h in heads]
    scores = [_dot_nt(q[h], k[h].astype(BF16)) * jnp.where(causal, jnp.exp(dist * lg[h]), 0.0) for h in heads]
    cross = [_dot(q[h], s[h].astype(BF16)) * jnp.exp((ic + 1.0) * lg[h]) for h in heads]
    o = [_dot(scores[h].astype(BF16), vb[h]) + cross[h] for h in heads]
    for h in heads:
        kd = (k[h] * jnp.exp((c - 1.0 - ic) * lg[h])).astype(BF16)
        s_ref[0, h] = s[h] * jnp.exp(float(c) * lg[h]) + _dot_tn(kd, vb[h])
    for h in heads:
        y_ref[:, cols[h]] = (_group_norm_rows(o[h], GN_EPS) * _silu(g_ref[:, cols[h]])).astype(BF16)


def _ret_prompt(proj, lg, cos, sin, nb, l, nh):
    hd = RET_HEAD_DIM
    dr = nh * hd
    c = math.gcd(l, RET_CHUNK)
    nchunk = l // c
    sect = lambda s: pl.BlockSpec((c, dr), lambda b, n, s=s: (b * nchunk + n, s))
    tab = pl.BlockSpec((c, hd // 2), lambda b, n: (n, 0))
    return pl.pallas_call(
        _ret_prompt_body,
        grid=(nb, nchunk),
        in_specs=[pl.BlockSpec(memory_space=pltpu.SMEM), sect(0), sect(1), sect(2), sect(3), tab, tab],
        out_specs=[pl.BlockSpec((c, dr), lambda b, n: (b * nchunk + n, 0)),
                   pl.BlockSpec((1, nh, hd, hd), lambda b, n: (b, 0, 0, 0))],
        out_shape=[jax.ShapeDtypeStruct((nb * l, dr), BF16),
                   jax.ShapeDtypeStruct((nb, nh, hd, hd), F32)],
        compiler_params=_params(("parallel", "arbitrary")),
        name="retention_prompt",
    )(lg, proj, proj, proj, proj, cos, sin)


def _ret_sample_body(lg_ref, q_ref, k_ref, v_ref, g_ref, cos_ref, sin_ref, s0_ref, y_ref, s_ref, *, nh, n_tok):
    hd = RET_HEAD_DIM
    bs = s0_ref.shape[0]
    sl = SAMPLE_SLOTS
    cos = cos_ref[...]
    sin = sin_ref[...]
    slot_r = lax.broadcasted_iota(jnp.int32, (sl, 1), 0)
    tok_r = (slot_r - 1).astype(F32)
    valid_r = (slot_r >= 1) & (slot_r <= n_tok)
    ii = lax.broadcasted_iota(jnp.int32, (sl, sl), 0)
    jj = lax.broadcasted_iota(jnp.int32, (sl, sl), 1)
    diff = (ii - jj).astype(F32)
    pair_ok = (diff >= 0.0) & (jj >= 1) & (jj <= n_tok)
    for h in range(nh):
        lg = jnp.full((1, 1), lg_ref[h], F32)
        dmask = jnp.where(pair_ok, jnp.exp(jnp.maximum(diff, 0.0) * lg), 0.0)
        cross_decay = jnp.exp((tok_r + 1.0) * lg)
        key_decay = jnp.where(valid_r, jnp.exp((n_tok - 1.0 - tok_r) * lg), 0.0)
        chunk_decay = jnp.exp(float(n_tok) * lg)
        for b in range(bs):
            rows = slice(b * sl, (b + 1) * sl)
            cols = slice(h * hd, (h + 1) * hd)
            q = _rotary(q_ref[rows, cols], cos, sin)
            k = _rotary(k_ref[rows, cols], cos, sin) * (hd ** -0.5)
            vb = v_ref[rows, cols].astype(BF16)
            s = s0_ref[b, h]
            qb = q.astype(BF16)
            scores = _dot_nt(qb, k.astype(BF16)) * dmask
            inner = _dot(scores.astype(BF16), vb)
            cross = _dot(qb, s.astype(BF16)) * cross_decay
            kd_t = (k * key_decay).T.astype(BF16)
            s_ref[b, h] = s * chunk_decay + _dot(kd_t, vb)
            o = inner + cross
            y_ref[rows, cols] = (_group_norm_rows(o, GN_EPS) * _silu(g_ref[rows, cols])).astype(BF16)


def _ret_sample(proj, row0, lg, cos, sin, s0, nh, n_tok):
    bsz = s0.shape[0]
    hd = RET_HEAD_DIM
    dr = nh * hd
    bs = 2
    rows = bs * SAMPLE_SLOTS
    assert row0 % rows == 0
    t0 = row0 // rows
    sect = lambda s: pl.BlockSpec((rows, dr), lambda i, s=s: (t0 + i, s))
    tab = pl.BlockSpec((SAMPLE_SLOTS, hd // 2), lambda i: (0, 0))
    st = pl.BlockSpec((bs, nh, hd, hd), lambda i: (i, 0, 0, 0))
    return pl.pallas_call(
        functools.partial(_ret_sample_body, nh=nh, n_tok=n_tok),
        grid=(bsz // bs,),
        in_specs=[pl.BlockSpec(memory_space=pltpu.SMEM), sect(0), sect(1), sect(2), sect(3), tab, tab, st],
        out_specs=[pl.BlockSpec((rows, dr), lambda i: (i, 0)), st],
        out_shape=[jax.ShapeDtypeStruct((bsz * SAMPLE_SLOTS, dr), BF16),
                   jax.ShapeDtypeStruct(s0.shape, F32)],
        compiler_params=_params(("parallel",)),
        name="retention_sample",
    )(lg, proj, proj, proj, proj, cos, sin, s0)


def _out_core(yr_ref, yw_ref, w1_ref, w2_ref, x, gate, fw):
    y = _dot(yr_ref[...], w1_ref[...]) + _dot(yw_ref[...], w2_ref[...])
    xn = x + gate * y
    ms = jnp.mean(xn * xn, axis=-1, keepdims=True)
    return xn * lax.rsqrt(ms + NORM_EPS) * fw


def _out_prompt_body(yr_ref, yw_ref, w1_ref, w2_ref, x_ref, mod_ref, fw_ref, o_ref):
    o_ref[0] = _out_core(yr_ref, yw_ref, w1_ref, w2_ref, x_ref[0], mod_ref[0][2:3], fw_ref[...])


def _out_prompt(y_ret, y_rw, w1, w2, x, mod3, fw):
    b, l, d = x.shape
    dr = w1.shape[0]
    tl = _tile(l, 256, 16)
    nt = l // tl
    wspec = pl.BlockSpec((dr, d), lambda bi, i: (0, 0))
    yspec = pl.BlockSpec((tl, dr), lambda bi, i: (bi * nt + i, 0))
    return pl.pallas_call(
        _out_prompt_body,
        grid=(b, nt),
        in_specs=[yspec, yspec, wspec, wspec,
                  pl.BlockSpec((1, tl, d), lambda bi, i: (bi, i, 0)),
                  pl.BlockSpec((1, 3, d), lambda bi, i: (bi, 0, 0)),
                  pl.BlockSpec((1, d), lambda bi, i: (0, 0))],
        out_specs=pl.BlockSpec((1, tl, d), lambda bi, i: (bi, i, 0)),
        out_shape=jax.ShapeDtypeStruct((b, l, d), F32),
        compiler_params=_params(("parallel", "parallel")),
        name="out_proj_prompt",
    )(y_ret, y_rw, w1, w2, x, mod3, fw.reshape(1, d))


def _out_sample_body(yr_ref, yw_ref, w1_ref, w2_ref, x_ref, mod_ref, fw_ref, o_ref):
    bs, sl, d = x_ref.shape
    gate = jnp.broadcast_to(mod_ref[...][:, 2:3, :], (bs, sl, d)).reshape(bs * sl, d)
    x = x_ref[...].reshape(bs * sl, d)
    o = _out_core(yr_ref, yw_ref, w1_ref, w2_ref, x, gate, fw_ref[...])
    o_ref[...] = o.reshape(bs, sl, d)


def _out_sample(y_ret, y_rw, w1, w2, x_slots, mod3, fw):
    bsz, sl, d = x_slots.shape
    dr = w1.shape[0]
    bs = _tile(bsz, 32, SUBLANES)
    wspec = pl.BlockSpec((dr, d), lambda i: (0, 0))
    yspec = pl.BlockSpec((bs * sl, dr), lambda i: (i, 0))
    xspec = pl.BlockSpec((bs, sl, d), lambda i: (i, 0, 0))
    return pl.pallas_call(
        _out_sample_body,
        grid=(bsz // bs,),
        in_specs=[yspec, yspec, wspec, wspec, xspec,
                  pl.BlockSpec((bs, 3, d), lambda i: (i, 0, 0)),
                  pl.BlockSpec((1, d), lambda i: (0, 0))],
        out_specs=xspec,
        out_shape=jax.ShapeDtypeStruct((bsz, sl, d), F32),
        compiler_params=_params(("parallel",)),
        name="out_proj_sample",
    )(y_ret, y_rw, w1, w2, x_slots, mod3, fw.reshape(1, d))


def _rope_tables(pos, half):
    inv_freq = ROPE_THETA ** (-jnp.arange(half, dtype=F32) / half)
    ang = pos[:, None] * inv_freq[None, :]
    return jnp.cos(ang), jnp.sin(ang)


def kernel(x_prompt, x_sample, c_prompt, c_sample, state_ret, state_rwkv, state_shift, norm_w, w_ada,
           b_ada, w_in, mu_shift, w0_decay, w2_decay, a0, a2, k_k, k_a, r_k, ln_x_w, ln_x_b, w_out,
           final_norm_w):
    depth = w_in.shape[0]
    assert depth == 1, "single-layer trunk"
    bp, lp, d = x_prompt.shape
    bsz, ls, _ = x_sample.shape
    assert ls == 4, "sample path packs 4 tokens into slots 1..4"
    dr = d
    nh_ret = dr // RET_HEAD_DIM
    n_main = 8 * dr
    lora = w2_decay.shape[1]
    lora_pad = -(-2 * lora // LANES) * LANES

    w_t = jnp.swapaxes(w_in[0], 0, 1)
    w_t_lora = jnp.pad(w_t[n_main:], ((0, lora_pad - 2 * lora), (0, 0)))
    w_o = w_out[0].astype(BF16)
    w_o_ret, w_o_rw = w_o[:dr], w_o[dr:]
    mu = mu_shift[0]
    row = lambda p: p.reshape(1, -1)
    mu_r, mu_k, mu_v = row(mu[0:dr]), row(mu[dr:2 * dr]), row(mu[2 * dr:3 * dr])
    mu_l = jnp.pad(mu[3 * dr:], (0, lora_pad - 2 * lora)).reshape(1, lora_pad)
    w2p = jnp.pad(w2_decay[0], ((0, lora_pad - lora), (0, 0))).astype(BF16)
    a2p = jnp.pad(a2[0], ((lora, lora_pad - 2 * lora), (0, 0))).astype(BF16)
    lg = jnp.log1p(-jnp.exp2(-5.0 - jnp.arange(nh_ret, dtype=F32)))
    rw_params = (mu_r, mu_k, mu_v, mu_l, row(w0_decay[0]), w2p, row(a0[0]), a2p, row(k_k[0]), row(k_a[0]),
                 row(r_k[0]), row(ln_x_w[0]), row(ln_x_b[0]))

    n_c = bp + bsz
    n_c_pad = -(-n_c // SUBLANES) * SUBLANES
    c_all = jnp.pad(jnp.concatenate([c_prompt, c_sample], axis=0), ((0, n_c_pad - n_c), (0, 0)))
    mod3 = _adaln(c_all, w_ada[0], b_ada[0]).reshape(n_c_pad, 3, d)
    mod_p, mod_s = mod3[:bp], mod3[bp:bp + bsz]

    sl = SAMPLE_SLOTS
    row_s = bp * lp
    x_slots = jnp.pad(x_sample, ((0, 0), (1, sl - 1 - ls), (0, 0)))
    h_all, last_p, new_shift_s = _modulate(x_prompt, mod_p, x_slots, state_shift[0], mod_s, norm_w[0])
    new_shift_p = last_p[:, SUBLANES - 1, :]
    proj = _in_proj(h_all, w_t, n_main)
    lora_proj = _in_proj(h_all, w_t_lora, lora_pad, name="in_proj_lora")

    cos_p, sin_p = _rope_tables(jnp.arange(lp, dtype=F32), RET_HEAD_DIM // 2)
    y_ret_p, s_ret_p = _ret_prompt(proj, lg, cos_p, sin_p, bp, lp, nh_ret)
    y_rw_p, s_rw_p = _rwkv_prompt(proj, lora_proj, bp, lp, dr, *rw_params)
    y_prompt = _out_prompt(y_ret_p, y_rw_p, w_o_ret, w_o_rw, x_prompt, mod_p, final_norm_w)

    slot_pos = jnp.arange(sl, dtype=F32) - 1.0
    pos_s = jnp.where((slot_pos >= 0) & (slot_pos < ls), float(PAST_LEN) + slot_pos, 0.0)
    cos_s, sin_s = _rope_tables(pos_s, RET_HEAD_DIM // 2)
    y_ret_s, s_ret_s = _ret_sample(proj, row_s, lg, cos_s, sin_s, state_ret[0], nh_ret, ls)
    y_rw_s, s_rw_s = _rwkv_sample(proj, lora_proj, row_s, state_rwkv[0], ls, dr, *rw_params)
    y_slots = _out_sample(y_ret_s, y_rw_s, w_o_ret, w_o_rw, x_slots, mod_s, final_norm_w)
    y_sample = y_slots[:, 1:1 + ls, :]

    return (y_prompt, y_sample, s_ret_p[None], s_rw_p[None], new_shift_p[None],
            s_ret_s[None], s_rw_s[None], new_shift_s[None])
```

```python
import functools
import math

import jax
import jax.numpy as jnp
from jax import lax
from jax.experimental import pallas as pl
from jax.experimental.pallas import tpu as pltpu

F32 = jnp.float32
BF16 = jnp.bfloat16

RET_HEAD_DIM = 256
RWKV_HEAD_DIM = 64
RET_CHUNK = 128
RWKV_CHUNK = 64
RWKV_GROUP = 4
PAST_LEN = 16384
ROPE_THETA = 10000.0
NORM_EPS = 1e-6
GN_EPS = 1e-5
RWKV_GN_EPS = 64e-5
SAMPLE_SLOTS = 8
LANES = 128
SUBLANES = 8
VMEM_LIMIT_BYTES = 56 * 1024 * 1024


def _params(sem):
    return pltpu.CompilerParams(dimension_semantics=sem, vmem_limit_bytes=VMEM_LIMIT_BYTES)


def _tile(n, cap, align):
    if n <= cap:
        return n
    t = (cap // align) * align
    while t >= align:
        if n % t == 0:
            return t
        t -= align
    return n


def _silu(x):
    return x * jax.nn.sigmoid(x)


def _dot(a, b):
    return jnp.dot(a, b, preferred_element_type=F32)


def _dot_nt(a, b):
    return lax.dot_general(a, b, (((1,), (1,)), ((), ())), preferred_element_type=F32)


def _dot_tn(a, b):
    return lax.dot_general(a, b, (((0,), (0,)), ((), ())), preferred_element_type=F32)


def _adaln_body(c_ref, w_ref, b_ref, o_ref):
    s = _silu(c_ref[...]).astype(BF16)
    o_ref[...] = _dot(s, w_ref[...].astype(BF16)) + b_ref[...]


def _adaln(c, w_ada, b_ada):
    rows, d = c.shape
    n = w_ada.shape[1]
    tn = _tile(n, 768, LANES)
    return pl.pallas_call(
        _adaln_body,
        grid=(n // tn,),
        in_specs=[pl.BlockSpec((rows, d), lambda j: (0, 0)),
                  pl.BlockSpec((d, tn), lambda j: (0, j)),
                  pl.BlockSpec((1, tn), lambda j: (0, j))],
        out_specs=pl.BlockSpec((rows, tn), lambda j: (0, j)),
        out_shape=jax.ShapeDtypeStruct((rows, n), F32),
        compiler_params=_params(("parallel",)),
        name="adaln",
    )(c, w_ada, b_ada.reshape(1, n))


def _modulated(x, nw, shift, scale):
    ms = jnp.mean(x * x, axis=-1, keepdims=True)
    return x * lax.rsqrt(ms + NORM_EPS) * nw * (1.0 + scale) + shift


def _modulate_body(xp_ref, modp_ref, xs_ref, prev_ref, mods_ref, nw_ref, h_ref, last_ref, new_ref, *, n_prompt):
    s = pl.program_id(0)

    @pl.when(s < n_prompt)
    def _():
        m = modp_ref[0]
        h = _modulated(xp_ref[0], nw_ref[...], m[0:1], m[1:2])
        h_ref[...] = h.astype(BF16)
        tl = h.shape[0]
        last_ref[0] = h[tl - SUBLANES:tl]

    @pl.when(s >= n_prompt)
    def _():
        x = xs_ref[...]
        m = mods_ref[...]
        h = _modulated(x, nw_ref[...], m[:, 0:1, :], m[:, 1:2, :])
        slot = lax.broadcasted_iota(jnp.int32, h.shape, 1)
        full = jnp.where(slot == 0, prev_ref[...][:, None, :], h)
        bs, sl, d = x.shape
        h_ref[...] = full.reshape(bs * sl, d).astype(BF16)
        new_ref[...] = h[:, 4, :]


def _modulate(x_prompt, mod_p, x_slots, prev, mod_s, nw):
    b, l, d = x_prompt.shape
    bsz, sl, _ = x_slots.shape
    tl = _tile(l, min(256, bsz * sl), 16)
    nt = l // tl
    bs = tl // sl
    assert bsz % bs == 0 and bs % SUBLANES == 0
    n_prompt = b * nt
    n_sample = bsz // bs
    samp = lambda s: jnp.maximum(s - n_prompt, 0)
    pb = lambda s: jnp.minimum(s // nt, b - 1)
    return pl.pallas_call(
        functools.partial(_modulate_body, n_prompt=n_prompt),
        grid=(n_prompt + n_sample,),
        in_specs=[pl.BlockSpec((1, tl, d), lambda s: (pb(s), jnp.where(s < n_prompt, s % nt, nt - 1), 0)),
                  pl.BlockSpec((1, 3, d), lambda s: (pb(s), 0, 0)),
                  pl.BlockSpec((bs, sl, d), lambda s: (samp(s), 0, 0)),
                  pl.BlockSpec((bs, d), lambda s: (samp(s), 0)),
                  pl.BlockSpec((bs, 3, d), lambda s: (samp(s), 0, 0)),
                  pl.BlockSpec((1, d), lambda s: (0, 0))],
        out_specs=[pl.BlockSpec((tl, d), lambda s: (s, 0)),
                   pl.BlockSpec((1, SUBLANES, d), lambda s: (pb(s), 0, 0)),
                   pl.BlockSpec((bs, d), lambda s: (samp(s), 0))],
        out_shape=[jax.ShapeDtypeStruct((b * l + bsz * sl, d), BF16),
                   jax.ShapeDtypeStruct((b, SUBLANES, d), F32),
                   jax.ShapeDtypeStruct((bsz, d), F32)],
        compiler_params=_params(("arbitrary",)),
        name="modulate",
    )(x_prompt, mod_p, x_slots, prev, mod_s, nw.reshape(1, d))


def _in_proj_body(x_ref, wt_ref, o_ref, w_scr):
    @pl.when(pl.program_id(1) == 0)
    def _():
        w_scr[...] = wt_ref[...].astype(BF16)

    o_ref[...] = _dot_nt(x_ref[...], w_scr[...])


def _in_proj(x, wt, n, name="in_proj"):
    m, k = x.shape
    tm = _tile(m, 1024, 16)
    tn = _tile(n, 1024, LANES)
    return pl.pallas_call(
        _in_proj_body,
        grid=(n // tn, m // tm),
        in_specs=[pl.BlockSpec((tm, k), lambda j, i: (i, 0)),
                  pl.BlockSpec((tn, k), lambda j, i: (j, 0))],
        out_specs=pl.BlockSpec((tm, tn), lambda j, i: (i, j)),
        out_shape=jax.ShapeDtypeStruct((m, n), F32),
        scratch_shapes=[pltpu.VMEM((tn, k), BF16)],
        compiler_params=_params(("arbitrary", "arbitrary")),
        name=name,
    )(x, wt)


def _token_shift(cur_ref, carry_ref, mu_ref):
    cur = cur_ref[...]
    tl = cur.shape[0]
    prev = pltpu.roll(cur, 1, 0)
    if carry_ref is not None:
        row = lax.broadcasted_iota(jnp.int32, cur.shape, 0)
        prev = jnp.where(row == 0, carry_ref[0:1, :], prev)
        carry_ref[0:1, :] = cur[tl - 1:tl, :]
    return cur + (prev - cur) * mu_ref[...]


def _decay_and_rate(lo, w0_ref, w2_ref, a0_ref, a2_ref):
    dec = _dot(jnp.tanh(lo).astype(BF16), w2_ref[...])
    z = -(w0_ref[...] + dec)
    softplus = jnp.maximum(z, 0.0) + jnp.log1p(jnp.exp(-jnp.abs(z)))
    w_log = -softplus - 0.5
    a = jax.nn.sigmoid(a0_ref[...] + _dot(lo.astype(BF16), a2_ref[...]))
    return -jnp.exp(w_log), a


def _zero_refs(*refs):
    for r in refs:
        r[...] = jnp.zeros_like(r)


def _split3(x):
    hi = x.astype(BF16)
    r1 = x - hi.astype(F32)
    mid = r1.astype(BF16)
    lo = (r1 - mid.astype(F32)).astype(BF16)
    return hi, mid, lo


def _dot_split3(m, parts):
    return _dot(m, parts[0]) + _dot(m, parts[1]) + _dot(m, parts[2])


def _rwkv_tile(pr_ref, pk_ref, pv_ref, pg_ref, pl_ref, mu_r, mu_k, mu_v, mu_l, w0_ref, w2_ref,
               a0_ref, a2_ref, kk_ref, ka_ref, rk_ref, lnw_ref, lnb_ref, y_ref,
               carries, seq, valid_slots, load_state, store_state):
    c = pr_ref.shape[0]
    hd = RWKV_HEAD_DIM
    gw = RWKV_GROUP * hd
    groups = range(pr_ref.shape[1] // gw)
    nseq = c // seq
    seqs = range(nseq)
    rows_of = [slice(q * seq, (q + 1) * seq) for q in seqs]
    c_r, c_k, c_v, c_l = carries

    r_all = _token_shift(pr_ref, c_r, mu_r)
    kw_all = _token_shift(pk_ref, c_k, mu_k)
    v_all = _token_shift(pv_ref, c_v, mu_v)
    lo = _token_shift(pl_ref, c_l, mu_l)
    logw_all, a_all = _decay_and_rate(lo, w0_ref, w2_ref, a0_ref, a2_ref)
    k_all = kw_all * (1.0 + (a_all - 1.0) * ka_ref[...])
    kkr_all = kw_all * kk_ref[...]
    if valid_slots is not None:
        slot = lax.broadcasted_iota(jnp.int32, (c, 1), 0) % seq
        valid = (slot >= valid_slots[0]) & (slot <= valid_slots[1])
        logw_all = jnp.where(valid, logw_all, 0.0)
        kkr_all = jnp.where(valid, kkr_all, 0.0)
        k_all = jnp.where(valid, k_all, 0.0)

    ti = lax.broadcasted_iota(jnp.int32, (c, c), 0)
    tj = lax.broadcasted_iota(jnp.int32, (c, c), 1)
    same_seq = (ti // seq) == (tj // seq)
    logw_parts = _split3(logw_all)
    cum_all = _dot_split3(((ti >= tj) & same_seq).astype(BF16), logw_parts)
    if nseq > 1:
        cum_end_all = _dot_split3(same_seq.astype(BF16), logw_parts)

    hr = lax.broadcasted_iota(jnp.int32, (gw, gw), 0) // hd
    hc = lax.broadcasted_iota(jnp.int32, (gw, gw), 1) // hd
    headmask = hr == hc
    ones_bd = headmask.astype(BF16)
    tok = lax.broadcasted_iota(jnp.int32, (c, gw), 0)
    src = lax.broadcasted_iota(jnp.int32, (c, gw), 1) % hd
    same = (tok // seq) == (src // seq)
    strict = (tok > src) & same
    incl = (tok >= src) & same

    def bd(x):
        t = jnp.concatenate([x.astype(BF16)] * RWKV_GROUP, axis=0)
        return jnp.where(headmask, t, jnp.zeros_like(t))

    def head_sum(x, exact=False):
        xh = x.astype(BF16)
        if not exact:
            return _dot(xh, ones_bd)
        xl = (x - xh.astype(F32)).astype(BF16)
        return _dot(xh, ones_bd) + _dot(xl, ones_bd)

    sls = [slice(g * gw, (g + 1) * gw) for g in groups]
    r = [r_all[:, sl] for sl in sls]
    k = [k_all[:, sl] for sl in sls]
    v = [v_all[:, sl] for sl in sls]
    cum = [cum_all[:, sl] for sl in sls]
    if nseq > 1:
        cum_end = [cum_end_all[:, sl] for sl in sls]
    else:
        cum_end = [cm[c - 1:c, :] for cm in cum]

    kk_sq = [head_sum(jnp.square(kkr_all[:, sl])) for sl in sls]
    rk_sum = [head_sum(r[g] * k[g] * rk_ref[:, sls[g]], exact=True) for g in groups]
    kkn = [kkr_all[:, sls[g]] / jnp.maximum(jnp.sqrt(kk_sq[g]), 1e-12) for g in groups]
    b = [kkn[g] * a_all[:, sls[g]] for g in groups]
    e_inv = [jnp.exp(-cum[g]) for g in groups]
    at = [-kkn[g] * jnp.exp(cum[g] - logw_all[:, sls[g]]) for g in groups]
    rt = [r[g] * jnp.exp(cum[g]) for g in groups]
    ar = [jnp.concatenate([at[g], rt[g]], axis=0).astype(BF16) for g in groups]
    states = [load_state(g) for g in groups]

    def from_state(g):
        if nseq == 1:
            fs = _dot_nt(ar[g], states[g][0].astype(BF16))
            return fs[:c], fs[c:]
        fa, fr = [], []
        for q in seqs:
            arq = jnp.concatenate([at[g][rows_of[q]], rt[g][rows_of[q]]], axis=0).astype(BF16)
            fs = _dot_nt(arq, states[g][q].astype(BF16))
            fa.append(fs[:seq])
            fr.append(fs[seq:])
        return jnp.concatenate(fa, axis=0), jnp.concatenate(fr, axis=0)

    fstate = [from_state(g) for g in groups]
    s_b = [_dot_nt(ar[g], bd(b[g] * e_inv[g])) for g in groups]
    s_k = [_dot_nt(ar[g], bd(k[g] * e_inv[g])) for g in groups]
    a_ab = [jnp.where(strict, x[:c], 0.0) for x in s_b]
    a_rb = [jnp.where(incl, x[c:], 0.0) for x in s_b]
    a_ak = [jnp.where(strict, x[:c], 0.0) for x in s_k]
    a_rk = [jnp.where(incl, x[c:], 0.0) for x in s_k]

    steps = max(1, int(math.ceil(math.log2(seq))))
    q = a_ab
    pm = a_ab
    q_next = [_dot(q[g].astype(BF16), bd(q[g])) for g in groups] if steps > 1 else None
    for s in range(1, steps):
        q = q_next
        if s < steps - 1:
            res = [_dot(jnp.concatenate([q[g], pm[g]], axis=0).astype(BF16), bd(q[g])) for g in groups]
            q_next = [x[:c] for x in res]
            pm = [pm[g] + q[g] + res[g][c:] for g in groups]
        else:
            pm = [pm[g] + q[g] + _dot(pm[g].astype(BF16), bd(q[g])) for g in groups]

    bd_v = [bd(x) for x in v]
    w_rhs = [fstate[g][0] + _dot(a_ak[g].astype(BF16), bd_v[g]) for g in groups]
    u = [w_rhs[g] + _dot(pm[g].astype(BF16), bd(w_rhs[g])) for g in groups]
    o = [fstate[g][1] + _dot(jnp.concatenate([a_rb[g], a_rk[g]], axis=1).astype(BF16),
                             jnp.concatenate([bd(u[g]), bd_v[g]], axis=0)) for g in groups]
    for g in groups:
        e_end = jnp.exp(cum_end[g] - cum[g])
        b_end = b[g] * e_end
        k_end = k[g] * e_end
        for q in seqs:
            rq = rows_of[q]
            uv = jnp.concatenate([u[g][rq], v[g][rq]], axis=0).astype(BF16)
            bk = jnp.concatenate([b_end[rq], k_end[rq]], axis=0).astype(BF16)
            keep = jnp.exp(cum_end[g][q * seq:q * seq + 1, :]) if nseq > 1 else jnp.exp(cum_end[g])
            store_state(g, q, states[g][q] * keep + _dot_tn(uv, bk))

    mean = [head_sum(x) * (1.0 / hd) for x in o]
    oc = [o[g] - mean[g] for g in groups]
    var = [head_sum(jnp.square(x)) * (1.0 / hd) for x in oc]
    for g in groups:
        sl = sls[g]
        og = oc[g] * lax.rsqrt(var[g] + RWKV_GN_EPS) * lnw_ref[:, sl] + lnb_ref[:, sl]
        y_ref[:, sl] = ((og + rk_sum[g] * v[g]) * _silu(pg_ref[:, sl])).astype(BF16)


def _head_mask(gw, hd):
    hr = lax.broadcasted_iota(jnp.int32, (gw, gw), 0) // hd
    hc = lax.broadcasted_iota(jnp.int32, (gw, gw), 1) // hd
    return hr == hc


def _rwkv_prompt_body(*refs):
    ins, (y_ref, so_ref), (c_r, c_k, c_v, c_l, s_scr) = refs[:18], refs[18:20], refs[20:]
    n = pl.program_id(1)
    hd = RWKV_HEAD_DIM
    gw = RWKV_GROUP * hd
    headmask = _head_mask(gw, hd)

    @pl.when(n == 0)
    def _():
        _zero_refs(c_r, c_k, c_v, c_l, s_scr)

    def load_state(g):
        return [s_scr[g]]

    def store_state(g, q, s):
        s_scr[g] = jnp.where(headmask, s, 0.0)

    _rwkv_tile(*ins, y_ref, (c_r, c_k, c_v, c_l), ins[0].shape[0], None, load_state, store_state)

    @pl.when(n == pl.num_programs(1) - 1)
    def _():
        for g in range(s_scr.shape[0]):
            sg = s_scr[g]
            for j in range(RWKV_GROUP):
                so_ref[0, g * RWKV_GROUP + j] = sg[j * hd:(j + 1) * hd, j * hd:(j + 1) * hd]


def _rwkv_prompt(proj, lora, nb, l, dr, mu_r, mu_k, mu_v, mu_l, w0, w2p, a0, a2p, k_k, k_a, r_k, ln_w, ln_b):
    c = RWKV_CHUNK
    assert l % c == 0 and c == RWKV_HEAD_DIM
    nchunk = l // c
    lp = lora.shape[1]
    nh = dr // RWKV_HEAD_DIM
    gw = RWKV_GROUP * RWKV_HEAD_DIM
    vec = lambda: pl.BlockSpec((1, dr), lambda b, n: (0, 0))
    sect = lambda s: pl.BlockSpec((c, dr), lambda b, n, s=s: (b * nchunk + n, 4 + s))
    return pl.pallas_call(
        _rwkv_prompt_body,
        grid=(nb, nchunk),
        in_specs=[sect(0), sect(1), sect(2), sect(3),
                  pl.BlockSpec((c, lp), lambda b, n: (b * nchunk + n, 0)),
                  vec(), vec(), vec(),
                  pl.BlockSpec((1, lp), lambda b, n: (0, 0)),
                  vec(),
                  pl.BlockSpec((lp, dr), lambda b, n: (0, 0)),
                  vec(),
                  pl.BlockSpec((lp, dr), lambda b, n: (0, 0)),
                  vec(), vec(), vec(), vec(), vec()],
        out_specs=[pl.BlockSpec((c, dr), lambda b, n: (b * nchunk + n, 0)),
                   pl.BlockSpec((1, nh, RWKV_HEAD_DIM, RWKV_HEAD_DIM), lambda b, n: (b, 0, 0, 0))],
        out_shape=[jax.ShapeDtypeStruct((nb * l, dr), BF16),
                   jax.ShapeDtypeStruct((nb, nh, RWKV_HEAD_DIM, RWKV_HEAD_DIM), F32)],
        scratch_shapes=[pltpu.VMEM((SUBLANES, dr), F32), pltpu.VMEM((SUBLANES, dr), F32),
                        pltpu.VMEM((SUBLANES, dr), F32), pltpu.VMEM((SUBLANES, lp), F32),
                        pltpu.VMEM((dr // gw, gw, gw), F32)],
        compiler_params=_params(("parallel", "arbitrary")),
        name="rwkv_chunked",
    )(proj, proj, proj, proj, lora, mu_r, mu_k, mu_v, mu_l, w0, w2p, a0, a2p, k_k, k_a, r_k, ln_w, ln_b)


def _rwkv_sample_body(*refs, n_tok):
    ins, s0_ref, y_ref, so_ref = refs[:18], refs[18], refs[19], refs[20]
    hd = RWKV_HEAD_DIM
    zero = jnp.zeros((hd, hd), F32)

    def load_state(g):
        out = []
        for q in range(s0_ref.shape[0]):
            rows = [jnp.concatenate([s0_ref[q, g * RWKV_GROUP + j] if i == j else zero
                                     for i in range(RWKV_GROUP)], axis=1) for j in range(RWKV_GROUP)]
            out.append(jnp.concatenate(rows, axis=0))
        return out

    def store_state(g, q, s):
        for j in range(RWKV_GROUP):
            so_ref[q, g * RWKV_GROUP + j] = s[j * hd:(j + 1) * hd, j * hd:(j + 1) * hd]

    _rwkv_tile(*ins, y_ref, (None, None, None, None), SAMPLE_SLOTS, (1, n_tok), load_state, store_state)


def _rwkv_sample(proj, lora, row0, s0, n_tok, dr, mu_r, mu_k, mu_v, mu_l, w0, w2p, a0, a2p, k_k, k_a,
                 r_k, ln_w, ln_b):
    c = RWKV_CHUNK
    rows = s0.shape[0] * SAMPLE_SLOTS
    assert rows % c == 0 and row0 % c == 0 and c == RWKV_HEAD_DIM
    t0 = row0 // c
    nseq = c // SAMPLE_SLOTS
    lp = lora.shape[1]
    hd = RWKV_HEAD_DIM
    dw = _tile(dr, 1024, RWKV_GROUP * hd)
    nw = dr // dw
    vec = lambda: pl.BlockSpec((1, dw), lambda i, hf: (0, hf))
    sect = lambda s: pl.BlockSpec((c, dw), lambda i, hf, s=s: (t0 + i, (4 + s) * nw + hf))
    lora_w = lambda: pl.BlockSpec((lp, dw), lambda i, hf: (0, hf))
    st = pl.BlockSpec((nseq, dw // hd, hd, hd), lambda i, hf: (i, hf, 0, 0))
    return pl.pallas_call(
        functools.partial(_rwkv_sample_body, n_tok=n_tok),
        grid=(rows // c, nw),
        in_specs=[sect(0), sect(1), sect(2), sect(3),
                  pl.BlockSpec((c, lp), lambda i, hf: (t0 + i, 0)),
                  vec(), vec(), vec(),
                  pl.BlockSpec((1, lp), lambda i, hf: (0, 0)),
                  vec(), lora_w(), vec(), lora_w(),
                  vec(), vec(), vec(), vec(), vec(), st],
        out_specs=[pl.BlockSpec((c, dw), lambda i, hf: (i, hf)), st],
        out_shape=[jax.ShapeDtypeStruct((rows, dr), BF16),
                   jax.ShapeDtypeStruct(s0.shape, F32)],
        compiler_params=_params(("parallel", "parallel")),
        name="rwkv_chunked_sample",
    )(proj, proj, proj, proj, lora, mu_r, mu_k, mu_v, mu_l, w0, w2p, a0, a2p, k_k, k_a, r_k, ln_w, ln_b, s0)


def _rotary(x, cos, sin):
    half = x.shape[-1] // 2
    x1 = x[:, :half]
    x2 = x[:, half:]
    return jnp.concatenate([x1 * cos - x2 * sin, x1 * sin + x2 * cos], axis=-1)


def _group_norm_rows(o, eps):
    mean = jnp.mean(o, axis=-1, keepdims=True)
    oc = o - mean
    var = jnp.mean(oc * oc, axis=-1, keepdims=True)
    return oc * lax.rsqrt(var + eps)


def _ret_prompt_body(lg_ref, q_ref, k_ref, v_ref, g_ref, cos_ref, sin_ref, y_ref, s_ref):
    n = pl.program_id(1)
    c = q_ref.shape[0]
    hd = RET_HEAD_DIM
    heads = range(q_ref.shape[1] // hd)
    cols = [slice(h * hd, (h + 1) * hd) for h in heads]

    @pl.when(n == 0)
    def _():
        s_ref[...] = jnp.zeros_like(s_ref)

    cos = cos_ref[...]
    sin = sin_ref[...]
    ii = lax.broadcasted_iota(jnp.int32, (c, c), 0)
    jj = lax.broadcasted_iota(jnp.int32, (c, c), 1)
    diff = (ii - jj).astype(F32)
    causal = diff >= 0.0
    dist = jnp.maximum(diff, 0.0)
    ic = lax.broadcasted_iota(jnp.int32, (c, 1), 0).astype(F32)

    lg = [jnp.full((1, 1), lg_ref[h], F32) for h in heads]
    q = [_rotary(q_ref[:, cs], cos, sin).astype(BF16) for cs in cols]
    k = [_rotary(k_ref[:, cs], cos, sin) * (hd ** -0.5) for cs in cols]
    vb = [v_ref[:, cs].astype(BF16) for cs in cols]
    s = [s_ref[0, h] for h in heads]
    scores = [_dot_nt(q[h], k[h].astype(BF16)) * jnp.where(causal, jnp.exp(dist * lg[h]), 0.0) for h in heads]
    cross = [_dot(q[h], s[h].astype(BF16)) * jnp.exp((ic + 1.0) * lg[h]) for h in heads]
    o = [_dot(scores[h].astype(BF16), vb[h]) + cross[h] for h in heads]
    for h in heads:
        kd = (k[h] * jnp.exp((c - 1.0 - ic) * lg[h])).astype(BF16)
        s_ref[0, h] = s[h] * jnp.exp(float(c) * lg[h]) + _dot_tn(kd, vb[h])
    for h in heads:
        y_ref[:, cols[h]] = (_group_norm_rows(o[h], GN_EPS) * _silu(g_ref[:, cols[h]])).astype(BF16)


def _ret_prompt(proj, lg, cos, sin, nb, l, nh):
    hd = RET_HEAD_DIM
    dr = nh * hd
    c = math.gcd(l, RET_CHUNK)
    nchunk = l // c
    sect = lambda s: pl.BlockSpec((c, dr), lambda b, n, s=s: (b * nchunk + n, s))
    tab = pl.BlockSpec((c, hd // 2), lambda b, n: (n, 0))
    return pl.pallas_call(
        _ret_prompt_body,
        grid=(nb, nchunk),
        in_specs=[pl.BlockSpec(memory_space=pltpu.SMEM), sect(0), sect(1), sect(2), sect(3), tab, tab],
        out_specs=[pl.BlockSpec((c, dr), lambda b, n: (b * nchunk + n, 0)),
                   pl.BlockSpec((1, nh, hd, hd), lambda b, n: (b, 0, 0, 0))],
        out_shape=[jax.ShapeDtypeStruct((nb * l, dr), BF16),
                   jax.ShapeDtypeStruct((nb, nh, hd, hd), F32)],
        compiler_params=_params(("parallel", "arbitrary")),
        name="retention_prompt",
    )(lg, proj, proj, proj, proj, cos, sin)


def _ret_sample_body(lg_ref, q_ref, k_ref, v_ref, g_ref, cos_ref, sin_ref, s0_ref, y_ref, s_ref, *, nh, n_tok):
    hd = RET_HEAD_DIM
    bs = s0_ref.shape[0]
    sl = SAMPLE_SLOTS
    cos = cos_ref[...]
    sin = sin_ref[...]
    slot_r = lax.broadcasted_iota(jnp.int32, (sl, 1), 0)
    tok_r = (slot_r - 1).astype(F32)
    valid_r = (slot_r >= 1) & (slot_r <= n_tok)
    ii = lax.broadcasted_iota(jnp.int32, (sl, sl), 0)
    jj = lax.broadcasted_iota(jnp.int32, (sl, sl), 1)
    diff = (ii - jj).astype(F32)
    pair_ok = (diff >= 0.0) & (jj >= 1) & (jj <= n_tok)
    for h in range(nh):
        lg = jnp.full((1, 1), lg_ref[h], F32)
        dmask = jnp.where(pair_ok, jnp.exp(jnp.maximum(diff, 0.0) * lg), 0.0)
        cross_decay = jnp.exp((tok_r + 1.0) * lg)
        key_decay = jnp.where(valid_r, jnp.exp((n_tok - 1.0 - tok_r) * lg), 0.0)
        chunk_decay = jnp.exp(float(n_tok) * lg)
        for b in range(bs):
            rows = slice(b * sl, (b + 1) * sl)
            cols = slice(h * hd, (h + 1) * hd)
            q = _rotary(q_ref[rows, cols], cos, sin)
            k = _rotary(k_ref[rows, cols], cos, sin) * (hd ** -0.5)
            vb = v_ref[rows, cols].astype(BF16)
            s = s0_ref[b, h]
            qb = q.astype(BF16)
            scores = _dot_nt(qb, k.astype(BF16)) * dmask
            inner = _dot(scores.astype(BF16), vb)
            cross = _dot(qb, s.astype(BF16)) * cross_decay
            kd_t = (k * key_decay).T.astype(BF16)
            s_ref[b, h] = s * chunk_decay + _dot(kd_t, vb)
            o = inner + cross
            y_ref[rows, cols] = (_group_norm_rows(o, GN_EPS) * _silu(g_ref[rows, cols])).astype(BF16)


def _ret_sample(proj, row0, lg, cos, sin, s0, nh, n_tok):
    bsz = s0.shape[0]
    hd = RET_HEAD_DIM
    dr = nh * hd
    bs = 2
    rows = bs * SAMPLE_SLOTS
    assert row0 % rows == 0
    t0 = row0 // rows
    sect = lambda s: pl.BlockSpec((rows, dr), lambda i, s=s: (t0 + i, s))
    tab = pl.BlockSpec((SAMPLE_SLOTS, hd // 2), lambda i: (0, 0))
    st = pl.BlockSpec((bs, nh, hd, hd), lambda i: (i, 0, 0, 0))
    return pl.pallas_call(
        functools.partial(_ret_sample_body, nh=nh, n_tok=n_tok),
        grid=(bsz // bs,),
        in_specs=[pl.BlockSpec(memory_space=pltpu.SMEM), sect(0), sect(1), sect(2), sect(3), tab, tab, st],
        out_specs=[pl.BlockSpec((rows, dr), lambda i: (i, 0)), st],
        out_shape=[jax.ShapeDtypeStruct((bsz * SAMPLE_SLOTS, dr), BF16),
                   jax.ShapeDtypeStruct(s0.shape, F32)],
        compiler_params=_params(("parallel",)),
        name="retention_sample",
    )(lg, proj, proj, proj, proj, cos, sin, s0)


def _out_core(yr_ref, yw_ref, w1_ref, w2_ref, x, gate, fw):
    y = _dot(yr_ref[...], w1_ref[...]) + _dot(yw_ref[...], w2_ref[...])
    xn = x + gate * y
    ms = jnp.mean(xn * xn, axis=-1, keepdims=True)
    return xn * lax.rsqrt(ms + NORM_EPS) * fw


def _out_prompt_body(yr_ref, yw_ref, w1_ref, w2_ref, x_ref, mod_ref, fw_ref, o_ref):
    o_ref[0] = _out_core(yr_ref, yw_ref, w1_ref, w2_ref, x_ref[0], mod_ref[0][2:3], fw_ref[...])


def _out_prompt(y_ret, y_rw, w1, w2, x, mod3, fw):
    b, l, d = x.shape
    dr = w1.shape[0]
    tl = _tile(l, 512, 16)
    nt = l // tl
    wspec = pl.BlockSpec((dr, d), lambda bi, i: (0, 0), pipeline_mode=pl.Buffered(1))
    yspec = pl.BlockSpec((tl, dr), lambda bi, i: (bi * nt + i, 0))
    return pl.pallas_call(
        _out_prompt_body,
        grid=(b, nt),
        in_specs=[yspec, yspec, wspec, wspec,
                  pl.BlockSpec((1, tl, d), lambda bi, i: (bi, i, 0)),
                  pl.BlockSpec((1, 3, d), lambda bi, i: (bi, 0, 0)),
                  pl.BlockSpec((1, d), lambda bi, i: (0, 0))],
        out_specs=pl.BlockSpec((1, tl, d), lambda bi, i: (bi, i, 0)),
        out_shape=jax.ShapeDtypeStruct((b, l, d), F32),
        compiler_params=_params(("parallel", "parallel")),
        name="out_proj_prompt",
    )(y_ret, y_rw, w1, w2, x, mod3, fw.reshape(1, d))


def _out_sample_body(yr_ref, yw_ref, w1_ref, w2_ref, x_ref, mod_ref, fw_ref, o_ref):
    bs, sl, d = x_ref.shape
    gate = jnp.broadcast_to(mod_ref[...][:, 2:3, :], (bs, sl, d)).reshape(bs * sl, d)
    x = x_ref[...].reshape(bs * sl, d)
    o = _out_core(yr_ref, yw_ref, w1_ref, w2_ref, x, gate, fw_ref[...])
    o_ref[...] = o.reshape(bs, sl, d)


def _out_sample(y_ret, y_rw, w1, w2, x_slots, mod3, fw):
    bsz, sl, d = x_slots.shape
    dr = w1.shape[0]
    bs = _tile(bsz, 32, SUBLANES)
    wspec = pl.BlockSpec((dr, d), lambda i: (0, 0))
    yspec = pl.BlockSpec((bs * sl, dr), lambda i: (i, 0))
    xspec = pl.BlockSpec((bs, sl, d), lambda i: (i, 0, 0))
    return pl.pallas_call(
        _out_sample_body,
        grid=(bsz // bs,),
        in_specs=[yspec, yspec, wspec, wspec, xspec,
                  pl.BlockSpec((bs, 3, d), lambda i: (i, 0, 0)),
                  pl.BlockSpec((1, d), lambda i: (0, 0))],
        out_specs=xspec,
        out_shape=jax.ShapeDtypeStruct((bsz, sl, d), F32),
        compiler_params=_params(("parallel",)),
        name="out_proj_sample",
    )(y_ret, y_rw, w1, w2, x_slots, mod3, fw.reshape(1, d))


def _rope_tables(pos, half):
    inv_freq = ROPE_THETA ** (-jnp.arange(half, dtype=F32) / half)
    ang = pos[:, None] * inv_freq[None, :]
    return jnp.cos(ang), jnp.sin(ang)


def kernel(x_prompt, x_sample, c_prompt, c_sample, state_ret, state_rwkv, state_shift, norm_w, w_ada,
           b_ada, w_in, mu_shift, w0_decay, w2_decay, a0, a2, k_k, k_a, r_k, ln_x_w, ln_x_b, w_out,
           final_norm_w):
    depth = w_in.shape[0]
    assert depth == 1, "single-layer trunk"
    bp, lp, d = x_prompt.shape
    bsz, ls, _ = x_sample.shape
    assert ls == 4, "sample path packs 4 tokens into slots 1..4"
    dr = d
    nh_ret = dr // RET_HEAD_DIM
    n_main = 8 * dr
    lora = w2_decay.shape[1]
    lora_pad = -(-2 * lora // LANES) * LANES

    w_t = jnp.swapaxes(w_in[0], 0, 1)
    w_t_lora = jnp.pad(w_t[n_main:], ((0, lora_pad - 2 * lora), (0, 0)))
    w_o = w_out[0].astype(BF16)
    w_o_ret, w_o_rw = w_o[:dr], w_o[dr:]
    mu = mu_shift[0]
    row = lambda p: p.reshape(1, -1)
    mu_r, mu_k, mu_v = row(mu[0:dr]), row(mu[dr:2 * dr]), row(mu[2 * dr:3 * dr])
    mu_l = jnp.pad(mu[3 * dr:], (0, lora_pad - 2 * lora)).reshape(1, lora_pad)
    w2p = jnp.pad(w2_decay[0], ((0, lora_pad - lora), (0, 0))).astype(BF16)
    a2p = jnp.pad(a2[0], ((lora, lora_pad - 2 * lora), (0, 0))).astype(BF16)
    lg = jnp.log1p(-jnp.exp2(-5.0 - jnp.arange(nh_ret, dtype=F32)))
    rw_params = (mu_r, mu_k, mu_v, mu_l, row(w0_decay[0]), w2p, row(a0[0]), a2p, row(k_k[0]), row(k_a[0]),
                 row(r_k[0]), row(ln_x_w[0]), row(ln_x_b[0]))

    n_c = bp + bsz
    n_c_pad = -(-n_c // SUBLANES) * SUBLANES
    c_all = jnp.pad(jnp.concatenate([c_prompt, c_sample], axis=0), ((0, n_c_pad - n_c), (0, 0)))
    mod3 = _adaln(c_all, w_ada[0], b_ada[0]).reshape(n_c_pad, 3, d)
    mod_p, mod_s = mod3[:bp], mod3[bp:bp + bsz]

    sl = SAMPLE_SLOTS
    row_s = bp * lp
    x_slots = jnp.pad(x_sample, ((0, 0), (1, sl - 1 - ls), (0, 0)))
    h_all, last_p, new_shift_s = _modulate(x_prompt, mod_p, x_slots, state_shift[0], mod_s, norm_w[0])
    new_shift_p = last_p[:, SUBLANES - 1, :]
    proj = _in_proj(h_all, w_t, n_main)
    lora_proj = _in_proj(h_all, w_t_lora, lora_pad, name="in_proj_lora")

    cos_p, sin_p = _rope_tables(jnp.arange(lp, dtype=F32), RET_HEAD_DIM // 2)
    y_ret_p, s_ret_p = _ret_prompt(proj, lg, cos_p, sin_p, bp, lp, nh_ret)
    y_rw_p, s_rw_p = _rwkv_prompt(proj, lora_proj, bp, lp, dr, *rw_params)
    y_prompt = _out_prompt(y_ret_p, y_rw_p, w_o_ret, w_o_rw, x_prompt, mod_p, final_norm_w)

    slot_pos = jnp.arange(sl, dtype=F32) - 1.0
    pos_s = jnp.where((slot_pos >= 0) & (slot_pos < ls), float(PAST_LEN) + slot_pos, 0.0)
    cos_s, sin_s = _rope_tables(pos_s, RET_HEAD_DIM // 2)
    y_ret_s, s_ret_s = _ret_sample(proj, row_s, lg, cos_s, sin_s, state_ret[0], nh_ret, ls)
    y_rw_s, s_rw_s = _rwkv_sample(proj, lora_proj, row_s, state_rwkv[0], ls, dr, *rw_params)
    y_slots = _out_sample(y_ret_s, y_rw_s, w_o_ret, w_o_rw, x_slots, mod_s, final_norm_w)
    y_sample = y_slots[:, 1:1 + ls, :]

    return (y_prompt, y_sample, s_ret_p[None], s_rw_p[None], new_shift_p[None],
            s_ret_s[None], s_rw_s[None], new_shift_s[None])
```

```python
import functools
import math

import jax
import jax.numpy as jnp
from jax import lax
from jax.experimental import pallas as pl
from jax.experimental.pallas import tpu as pltpu

F32 = jnp.float32
BF16 = jnp.bfloat16

RET_HEAD_DIM = 256
RWKV_HEAD_DIM = 64
RET_CHUNK = 128
RWKV_CHUNK = 64
RWKV_GROUP = 2
PAST_LEN = 16384
ROPE_THETA = 10000.0
NORM_EPS = 1e-6
GN_EPS = 1e-5
RWKV_GN_EPS = 64e-5
SAMPLE_SLOTS = 8
LANES = 128
SUBLANES = 8
VMEM_LIMIT_BYTES = 56 * 1024 * 1024


def _params(sem):
    return pltpu.CompilerParams(dimension_semantics=sem, vmem_limit_bytes=VMEM_LIMIT_BYTES)


def _tile(n, cap, align):
    if n <= cap:
        return n
    t = (cap // align) * align
    while t >= align:
        if n % t == 0:
            return t
        t -= align
    return n


def _silu(x):
    return x * jax.nn.sigmoid(x)


def _dot(a, b):
    return jnp.dot(a, b, preferred_element_type=F32)


def _dot_nt(a, b):
    return lax.dot_general(a, b, (((1,), (1,)), ((), ())), preferred_element_type=F32)


def _dot_tn(a, b):
    return lax.dot_general(a, b, (((0,), (0,)), ((), ())), preferred_element_type=F32)


def _adaln_body(c_ref, w_ref, b_ref, o_ref):
    s = _silu(c_ref[...]).astype(BF16)
    o_ref[...] = _dot(s, w_ref[...].astype(BF16)) + b_ref[...]


def _adaln(c, w_ada, b_ada):
    rows, d = c.shape
    n = w_ada.shape[1]
    tn = _tile(n, 768, LANES)
    return pl.pallas_call(
        _adaln_body,
        grid=(n // tn,),
        in_specs=[pl.BlockSpec((rows, d), lambda j: (0, 0)),
                  pl.BlockSpec((d, tn), lambda j: (0, j)),
                  pl.BlockSpec((1, tn), lambda j: (0, j))],
        out_specs=pl.BlockSpec((rows, tn), lambda j: (0, j)),
        out_shape=jax.ShapeDtypeStruct((rows, n), F32),
        compiler_params=_params(("parallel",)),
        name="adaln",
    )(c, w_ada, b_ada.reshape(1, n))


def _modulated(x, nw, shift, scale):
    ms = jnp.mean(x * x, axis=-1, keepdims=True)
    return x * lax.rsqrt(ms + NORM_EPS) * nw * (1.0 + scale) + shift


def _modulate_body(xp_ref, modp_ref, xs_ref, prev_ref, mods_ref, nw_ref, h_ref, last_ref, new_ref, *, n_prompt):
    s = pl.program_id(0)

    @pl.when(s < n_prompt)
    def _():
        m = modp_ref[0]
        h = _modulated(xp_ref[0], nw_ref[...], m[0:1], m[1:2])
        h_ref[...] = h.astype(BF16)
        tl = h.shape[0]
        last_ref[0] = h[tl - SUBLANES:tl]

    @pl.when(s >= n_prompt)
    def _():
        x = xs_ref[...]
        m = mods_ref[...]
        h = _modulated(x, nw_ref[...], m[:, 0:1, :], m[:, 1:2, :])
        slot = lax.broadcasted_iota(jnp.int32, h.shape, 1)
        full = jnp.where(slot == 0, prev_ref[...][:, None, :], h)
        bs, sl, d = x.shape
        h_ref[...] = full.reshape(bs * sl, d).astype(BF16)
        new_ref[...] = h[:, 4, :]


def _modulate(x_prompt, mod_p, x_slots, prev, mod_s, nw):
    b, l, d = x_prompt.shape
    bsz, sl, _ = x_slots.shape
    tl = _tile(l, min(512, bsz * sl), 16)
    nt = l // tl
    bs = tl // sl
    assert bsz % bs == 0 and bs % SUBLANES == 0
    n_prompt = b * nt
    n_sample = bsz // bs
    samp = lambda s: jnp.maximum(s - n_prompt, 0)
    pb = lambda s: jnp.minimum(s // nt, b - 1)
    return pl.pallas_call(
        functools.partial(_modulate_body, n_prompt=n_prompt),
        grid=(n_prompt + n_sample,),
        in_specs=[pl.BlockSpec((1, tl, d), lambda s: (pb(s), jnp.where(s < n_prompt, s % nt, nt - 1), 0)),
                  pl.BlockSpec((1, 3, d), lambda s: (pb(s), 0, 0)),
                  pl.BlockSpec((bs, sl, d), lambda s: (samp(s), 0, 0)),
                  pl.BlockSpec((bs, d), lambda s: (samp(s), 0)),
                  pl.BlockSpec((bs, 3, d), lambda s: (samp(s), 0, 0)),
                  pl.BlockSpec((1, d), lambda s: (0, 0))],
        out_specs=[pl.BlockSpec((tl, d), lambda s: (s, 0)),
                   pl.BlockSpec((1, SUBLANES, d), lambda s: (pb(s), 0, 0)),
                   pl.BlockSpec((bs, d), lambda s: (samp(s), 0))],
        out_shape=[jax.ShapeDtypeStruct((b * l + bsz * sl, d), BF16),
                   jax.ShapeDtypeStruct((b, SUBLANES, d), F32),
                   jax.ShapeDtypeStruct((bsz, d), F32)],
        compiler_params=_params(("arbitrary",)),
        name="modulate",
    )(x_prompt, mod_p, x_slots, prev, mod_s, nw.reshape(1, d))


def _in_proj_body(x_ref, wt_ref, o_ref, w_scr):
    @pl.when(pl.program_id(1) == 0)
    def _():
        w_scr[...] = wt_ref[...].astype(BF16)

    o_ref[...] = _dot_nt(x_ref[...], w_scr[...])


def _in_proj(x, wt, n, name="in_proj"):
    m, k = x.shape
    tm = _tile(m, 1024, 16)
    tn = _tile(n, 1024, LANES)
    return pl.pallas_call(
        _in_proj_body,
        grid=(n // tn, m // tm),
        in_specs=[pl.BlockSpec((tm, k), lambda j, i: (i, 0)),
                  pl.BlockSpec((tn, k), lambda j, i: (j, 0))],
        out_specs=pl.BlockSpec((tm, tn), lambda j, i: (i, j)),
        out_shape=jax.ShapeDtypeStruct((m, n), F32),
        scratch_shapes=[pltpu.VMEM((tn, k), BF16)],
        compiler_params=_params(("arbitrary", "arbitrary")),
        name=name,
    )(x, wt)


def _token_shift(cur_ref, carry_ref, mu_ref):
    cur = cur_ref[...]
    tl = cur.shape[0]
    prev = pltpu.roll(cur, 1, 0)
    if carry_ref is not None:
        row = lax.broadcasted_iota(jnp.int32, cur.shape, 0)
        prev = jnp.where(row == 0, carry_ref[0:1, :], prev)
        carry_ref[0:1, :] = cur[tl - 1:tl, :]
    return cur + (prev - cur) * mu_ref[...]


def _decay_and_rate(lo, w0_ref, w2_ref, a0_ref, a2_ref):
    dec = _dot(jnp.tanh(lo).astype(BF16), w2_ref[...])
    logw = -math.exp(-0.5) * jax.nn.sigmoid(w0_ref[...] + dec)
    a = jax.nn.sigmoid(a0_ref[...] + _dot(lo.astype(BF16), a2_ref[...]))
    return logw, a


def _zero_refs(*refs):
    for r in refs:
        r[...] = jnp.zeros_like(r)


def _split3(x):
    hi = x.astype(BF16)
    r1 = x - hi.astype(F32)
    mid = r1.astype(BF16)
    lo = (r1 - mid.astype(F32)).astype(BF16)
    return hi, mid, lo


def _dot_split3(m, parts):
    return _dot(m, parts[0]) + _dot(m, parts[1]) + _dot(m, parts[2])


def _rwkv_tile(pr_ref, pk_ref, pv_ref, pg_ref, pl_ref, mu_r, mu_k, mu_v, mu_l, w0_ref, w2_ref,
               a0_ref, a2_ref, kk_ref, ka_ref, rk_ref, lnw_ref, lnb_ref, y_ref,
               carries, seq, valid_slots, load_state, store_state):
    c = pr_ref.shape[0]
    hd = RWKV_HEAD_DIM
    gw = RWKV_GROUP * hd
    groups = range(pr_ref.shape[1] // gw)
    nseq = c // seq
    seqs = range(nseq)
    rows_of = [slice(q * seq, (q + 1) * seq) for q in seqs]
    c_r, c_k, c_v, c_l = carries

    r_all = _token_shift(pr_ref, c_r, mu_r)
    kw_all = _token_shift(pk_ref, c_k, mu_k)
    v_all = _token_shift(pv_ref, c_v, mu_v)
    lo = _token_shift(pl_ref, c_l, mu_l)
    logw_all, a_all = _decay_and_rate(lo, w0_ref, w2_ref, a0_ref, a2_ref)
    k_all = kw_all * (1.0 + (a_all - 1.0) * ka_ref[...])
    kkr_all = kw_all * kk_ref[...]
    if valid_slots is not None:
        slot = lax.broadcasted_iota(jnp.int32, (c, 1), 0) % seq
        valid = (slot >= valid_slots[0]) & (slot <= valid_slots[1])
        logw_all = jnp.where(valid, logw_all, 0.0)
        kkr_all = jnp.where(valid, kkr_all, 0.0)
        k_all = jnp.where(valid, k_all, 0.0)

    ti = lax.broadcasted_iota(jnp.int32, (c, c), 0)
    tj = lax.broadcasted_iota(jnp.int32, (c, c), 1)
    same_seq = (ti // seq) == (tj // seq)
    logw_parts = _split3(logw_all)
    cum_all = _dot_split3(((ti >= tj) & same_seq).astype(BF16), logw_parts)
    if nseq > 1:
        cum_end_all = _dot_split3(same_seq.astype(BF16), logw_parts)

    hr = lax.broadcasted_iota(jnp.int32, (gw, gw), 0) // hd
    hc = lax.broadcasted_iota(jnp.int32, (gw, gw), 1) // hd
    headmask = hr == hc
    tok = lax.broadcasted_iota(jnp.int32, (c, gw), 0)
    src = lax.broadcasted_iota(jnp.int32, (c, gw), 1) % hd
    same = (tok // seq) == (src // seq)
    strict = (tok > src) & same
    incl = (tok >= src) & same

    def bd(x):
        t = jnp.concatenate([x.astype(BF16)] * RWKV_GROUP, axis=0)
        return jnp.where(headmask, t, jnp.zeros_like(t))

    lane_head = lax.broadcasted_iota(jnp.int32, (c, gw), 1) // hd

    def head_sum(x):
        out = jnp.zeros_like(x)
        for j in range(RWKV_GROUP):
            mine = lane_head == j
            out = jnp.where(mine, jnp.sum(jnp.where(mine, x, 0.0), axis=-1, keepdims=True), out)
        return out

    sls = [slice(g * gw, (g + 1) * gw) for g in groups]
    r = [r_all[:, sl] for sl in sls]
    k = [k_all[:, sl] for sl in sls]
    v = [v_all[:, sl] for sl in sls]
    cum = [cum_all[:, sl] for sl in sls]
    if nseq > 1:
        cum_end = [cum_end_all[:, sl] for sl in sls]
    else:
        cum_end = [cm[c - 1:c, :] for cm in cum]

    kk_sq = [head_sum(jnp.square(kkr_all[:, sl])) for sl in sls]
    rk_sum = [head_sum(r[g] * k[g] * rk_ref[:, sls[g]]) for g in groups]
    kkn = [kkr_all[:, sls[g]] / jnp.maximum(jnp.sqrt(kk_sq[g]), 1e-12) for g in groups]
    b = [kkn[g] * a_all[:, sls[g]] for g in groups]
    e_inv = [jnp.exp(-cum[g]) for g in groups]
    at = [-kkn[g] * jnp.exp(cum[g] - logw_all[:, sls[g]]) for g in groups]
    rt = [r[g] * jnp.exp(cum[g]) for g in groups]
    ar = [jnp.concatenate([at[g], rt[g]], axis=0).astype(BF16) for g in groups]
    states = [load_state(g) for g in groups]

    def from_state(g):
        if nseq == 1:
            fs = _dot_nt(ar[g], states[g][0].astype(BF16))
            return fs[:c], fs[c:]
        fa, fr = [], []
        for q in seqs:
            arq = jnp.concatenate([at[g][rows_of[q]], rt[g][rows_of[q]]], axis=0).astype(BF16)
            fs = _dot_nt(arq, states[g][q].astype(BF16))
            fa.append(fs[:seq])
            fr.append(fs[seq:])
        return jnp.concatenate(fa, axis=0), jnp.concatenate(fr, axis=0)

    fstate = [from_state(g) for g in groups]
    s_b = [_dot_nt(ar[g], bd(b[g] * e_inv[g])) for g in groups]
    s_k = [_dot_nt(ar[g], bd(k[g] * e_inv[g])) for g in groups]
    a_ab = [jnp.where(strict, x[:c], 0.0) for x in s_b]
    a_rb = [jnp.where(incl, x[c:], 0.0) for x in s_b]
    a_ak = [jnp.where(strict, x[:c], 0.0) for x in s_k]
    a_rk = [jnp.where(incl, x[c:], 0.0) for x in s_k]

    steps = max(1, int(math.ceil(math.log2(seq))))
    q = a_ab
    pm = a_ab
    q_next = [_dot(q[g].astype(BF16), bd(q[g])) for g in groups] if steps > 1 else None
    for s in range(1, steps):
        q = q_next
        if s < steps - 1:
            res = [_dot(jnp.concatenate([q[g], pm[g]], axis=0).astype(BF16), bd(q[g])) for g in groups]
            q_next = [x[:c] for x in res]
            pm = [pm[g] + q[g] + res[g][c:] for g in groups]
        else:
            pm = [pm[g] + q[g] + _dot(pm[g].astype(BF16), bd(q[g])) for g in groups]

    bd_v = [bd(x) for x in v]
    w_rhs = [fstate[g][0] + _dot(a_ak[g].astype(BF16), bd_v[g]) for g in groups]
    u = [w_rhs[g] + _dot(pm[g].astype(BF16), bd(w_rhs[g])) for g in groups]
    o = [fstate[g][1] + _dot(jnp.concatenate([a_rb[g], a_rk[g]], axis=1).astype(BF16),
                             jnp.concatenate([bd(u[g]), bd_v[g]], axis=0)) for g in groups]
    for g in groups:
        e_end = jnp.exp(cum_end[g] - cum[g])
        b_end = b[g] * e_end
        k_end = k[g] * e_end
        for q in seqs:
            rq = rows_of[q]
            uv = jnp.concatenate([u[g][rq], v[g][rq]], axis=0).astype(BF16)
            bk = jnp.concatenate([b_end[rq], k_end[rq]], axis=0).astype(BF16)
            keep = jnp.exp(cum_end[g][q * seq:q * seq + 1, :]) if nseq > 1 else jnp.exp(cum_end[g])
            store_state(g, q, states[g][q] * keep + _dot_tn(uv, bk))

    mean = [head_sum(x) * (1.0 / hd) for x in o]
    oc = [o[g] - mean[g] for g in groups]
    var = [head_sum(jnp.square(x)) * (1.0 / hd) for x in oc]
    for g in groups:
        sl = sls[g]
        og = oc[g] * lax.rsqrt(var[g] + RWKV_GN_EPS) * lnw_ref[:, sl] + lnb_ref[:, sl]
        y_ref[:, sl] = ((og + rk_sum[g] * v[g]) * _silu(pg_ref[:, sl])).astype(BF16)


def _head_mask(gw, hd):
    hr = lax.broadcasted_iota(jnp.int32, (gw, gw), 0) // hd
    hc = lax.broadcasted_iota(jnp.int32, (gw, gw), 1) // hd
    return hr == hc


def _rwkv_prompt_body(*refs):
    ins, (y_ref, so_ref), (c_r, c_k, c_v, c_l, s_scr) = refs[:18], refs[18:20], refs[20:]
    n = pl.program_id(1)
    hd = RWKV_HEAD_DIM
    gw = RWKV_GROUP * hd
    headmask = _head_mask(gw, hd)

    @pl.when(n == 0)
    def _():
        _zero_refs(c_r, c_k, c_v, c_l, s_scr)

    def load_state(g):
        return [s_scr[g]]

    def store_state(g, q, s):
        s_scr[g] = jnp.where(headmask, s, 0.0)

    _rwkv_tile(*ins, y_ref, (c_r, c_k, c_v, c_l), ins[0].shape[0], None, load_state, store_state)

    @pl.when(n == pl.num_programs(1) - 1)
    def _():
        for g in range(s_scr.shape[0]):
            sg = s_scr[g]
            for j in range(RWKV_GROUP):
                so_ref[0, g * RWKV_GROUP + j] = sg[j * hd:(j + 1) * hd, j * hd:(j + 1) * hd]


def _rwkv_prompt(proj, lora, nb, l, dr, mu_r, mu_k, mu_v, mu_l, w0, w2p, a0, a2p, k_k, k_a, r_k, ln_w, ln_b):
    c = RWKV_CHUNK
    assert l % c == 0 and c == RWKV_HEAD_DIM
    nchunk = l // c
    lp = lora.shape[1]
    nh = dr // RWKV_HEAD_DIM
    gw = RWKV_GROUP * RWKV_HEAD_DIM
    vec = lambda: pl.BlockSpec((1, dr), lambda b, n: (0, 0))
    sect = lambda s: pl.BlockSpec((c, dr), lambda b, n, s=s: (b * nchunk + n, 4 + s))
    return pl.pallas_call(
        _rwkv_prompt_body,
        grid=(nb, nchunk),
        in_specs=[sect(0), sect(1), sect(2), sect(3),
                  pl.BlockSpec((c, lp), lambda b, n: (b * nchunk + n, 0)),
                  vec(), vec(), vec(),
                  pl.BlockSpec((1, lp), lambda b, n: (0, 0)),
                  vec(),
                  pl.BlockSpec((lp, dr), lambda b, n: (0, 0)),
                  vec(),
                  pl.BlockSpec((lp, dr), lambda b, n: (0, 0)),
                  vec(), vec(), vec(), vec(), vec()],
        out_specs=[pl.BlockSpec((c, dr), lambda b, n: (b * nchunk + n, 0)),
                   pl.BlockSpec((1, nh, RWKV_HEAD_DIM, RWKV_HEAD_DIM), lambda b, n: (b, 0, 0, 0))],
        out_shape=[jax.ShapeDtypeStruct((nb * l, dr), BF16),
                   jax.ShapeDtypeStruct((nb, nh, RWKV_HEAD_DIM, RWKV_HEAD_DIM), F32)],
        scratch_shapes=[pltpu.VMEM((SUBLANES, dr), F32), pltpu.VMEM((SUBLANES, dr), F32),
                        pltpu.VMEM((SUBLANES, dr), F32), pltpu.VMEM((SUBLANES, lp), F32),
                        pltpu.VMEM((dr // gw, gw, gw), F32)],
        compiler_params=_params(("parallel", "arbitrary")),
        name="rwkv_chunked",
    )(proj, proj, proj, proj, lora, mu_r, mu_k, mu_v, mu_l, w0, w2p, a0, a2p, k_k, k_a, r_k, ln_w, ln_b)


def _rwkv_sample_body(*refs, n_tok):
    ins, s0_ref, y_ref, so_ref = refs[:18], refs[18], refs[19], refs[20]
    hd = RWKV_HEAD_DIM
    zero = jnp.zeros((hd, hd), F32)

    def load_state(g):
        out = []
        for q in range(s0_ref.shape[0]):
            rows = [jnp.concatenate([s0_ref[q, g * RWKV_GROUP + j] if i == j else zero
                                     for i in range(RWKV_GROUP)], axis=1) for j in range(RWKV_GROUP)]
            out.append(jnp.concatenate(rows, axis=0))
        return out

    def store_state(g, q, s):
        for j in range(RWKV_GROUP):
            so_ref[q, g * RWKV_GROUP + j] = s[j * hd:(j + 1) * hd, j * hd:(j + 1) * hd]

    _rwkv_tile(*ins, y_ref, (None, None, None, None), SAMPLE_SLOTS, (1, n_tok), load_state, store_state)


def _rwkv_sample(proj, lora, row0, s0, n_tok, dr, mu_r, mu_k, mu_v, mu_l, w0, w2p, a0, a2p, k_k, k_a,
                 r_k, ln_w, ln_b):
    c = RWKV_CHUNK
    rows = s0.shape[0] * SAMPLE_SLOTS
    assert rows % c == 0 and row0 % c == 0 and c == RWKV_HEAD_DIM
    t0 = row0 // c
    nseq = c // SAMPLE_SLOTS
    lp = lora.shape[1]
    hd = RWKV_HEAD_DIM
    dw = _tile(dr, 1024, RWKV_GROUP * hd)
    nw = dr // dw
    vec = lambda: pl.BlockSpec((1, dw), lambda i, hf: (0, hf))
    sect = lambda s: pl.BlockSpec((c, dw), lambda i, hf, s=s: (t0 + i, (4 + s) * nw + hf))
    lora_w = lambda: pl.BlockSpec((lp, dw), lambda i, hf: (0, hf))
    st = pl.BlockSpec((nseq, dw // hd, hd, hd), lambda i, hf: (i, hf, 0, 0))
    return pl.pallas_call(
        functools.partial(_rwkv_sample_body, n_tok=n_tok),
        grid=(rows // c, nw),
        in_specs=[sect(0), sect(1), sect(2), sect(3),
                  pl.BlockSpec((c, lp), lambda i, hf: (t0 + i, 0)),
                  vec(), vec(), vec(),
                  pl.BlockSpec((1, lp), lambda i, hf: (0, 0)),
                  vec(), lora_w(), vec(), lora_w(),
                  vec(), vec(), vec(), vec(), vec(), st],
        out_specs=[pl.BlockSpec((c, dw), lambda i, hf: (i, hf)), st],
        out_shape=[jax.ShapeDtypeStruct((rows, dr), BF16),
                   jax.ShapeDtypeStruct(s0.shape, F32)],
        compiler_params=_params(("parallel", "parallel")),
        name="rwkv_chunked_sample",
    )(proj, proj, proj, proj, lora, mu_r, mu_k, mu_v, mu_l, w0, w2p, a0, a2p, k_k, k_a, r_k, ln_w, ln_b, s0)


def _rotary(x, cos, sin):
    half = x.shape[-1] // 2
    x1 = x[:, :half]
    x2 = x[:, half:]
    return jnp.concatenate([x1 * cos - x2 * sin, x1 * sin + x2 * cos], axis=-1)


def _group_norm_rows(o, eps):
    mean = jnp.mean(o, axis=-1, keepdims=True)
    oc = o - mean
    var = jnp.mean(oc * oc, axis=-1, keepdims=True)
    return oc * lax.rsqrt(var + eps)


def _ret_prompt_body(lg_ref, q_ref, k_ref, v_ref, g_ref, cos_ref, sin_ref, y_ref, s_ref):
    n = pl.program_id(1)
    c = q_ref.shape[0]
    hd = RET_HEAD_DIM
    heads = range(q_ref.shape[1] // hd)
    cols = [slice(h * hd, (h + 1) * hd) for h in heads]

    @pl.when(n == 0)
    def _():
        s_ref[...] = jnp.zeros_like(s_ref)

    cos = cos_ref[...]
    sin = sin_ref[...]
    ii = lax.broadcasted_iota(jnp.int32, (c, c), 0)
    jj = lax.broadcasted_iota(jnp.int32, (c, c), 1)
    diff = (ii - jj).astype(F32)
    causal = diff >= 0.0
    dist = jnp.maximum(diff, 0.0)
    ic = lax.broadcasted_iota(jnp.int32, (c, 1), 0).astype(F32)

    lg = [jnp.full((1, 1), lg_ref[h], F32) for h in heads]
    q = [_rotary(q_ref[:, cs], cos, sin).astype(BF16) for cs in cols]
    k = [_rotary(k_ref[:, cs], cos, sin) * (hd ** -0.5) for cs in cols]
    vb = [v_ref[:, cs].astype(BF16) for cs in cols]
    s = [s_ref[0, h] for h in heads]
    scores = [_dot_nt(q[h], k[h].astype(BF16)) * jnp.where(causal, jnp.exp(dist * lg[h]), 0.0) for h in heads]
    cross = [_dot(q[h], s[h].astype(BF16)) * jnp.exp((ic + 1.0) * lg[h]) for h in heads]
    o = [_dot(scores[h].astype(BF16), vb[h]) + cross[h] for h in heads]
    for h in heads:
        kd = (k[h] * jnp.exp((c - 1.0 - ic) * lg[h])).astype(BF16)
        s_ref[0, h] = s[h] * jnp.exp(float(c) * lg[h]) + _dot_tn(kd, vb[h])
    for h in heads:
        y_ref[:, cols[h]] = (_group_norm_rows(o[h], GN_EPS) * _silu(g_ref[:, cols[h]])).astype(BF16)


def _ret_prompt(proj, lg, cos, sin, nb, l, nh):
    hd = RET_HEAD_DIM
    dr = nh * hd
    c = math.gcd(l, RET_CHUNK)
    nchunk = l // c
    sect = lambda s: pl.BlockSpec((c, dr), lambda b, n, s=s: (b * nchunk + n, s))
    tab = pl.BlockSpec((c, hd // 2), lambda b, n: (n, 0))
    return pl.pallas_call(
        _ret_prompt_body,
        grid=(nb, nchunk),
        in_specs=[pl.BlockSpec(memory_space=pltpu.SMEM), sect(0), sect(1), sect(2), sect(3), tab, tab],
        out_specs=[pl.BlockSpec((c, dr), lambda b, n: (b * nchunk + n, 0)),
                   pl.BlockSpec((1, nh, hd, hd), lambda b, n: (b, 0, 0, 0))],
        out_shape=[jax.ShapeDtypeStruct((nb * l, dr), BF16),
                   jax.ShapeDtypeStruct((nb, nh, hd, hd), F32)],
        compiler_params=_params(("parallel", "arbitrary")),
        name="retention_prompt",
    )(lg, proj, proj, proj, proj, cos, sin)


def _ret_sample_body(lg_ref, q_ref, k_ref, v_ref, g_ref, cos_ref, sin_ref, s0_ref, y_ref, s_ref, *, nh, n_tok):
    hd = RET_HEAD_DIM
    bs = s0_ref.shape[0]
    sl = SAMPLE_SLOTS
    cos = cos_ref[...]
    sin = sin_ref[...]
    slot_r = lax.broadcasted_iota(jnp.int32, (sl, 1), 0)
    tok_r = (slot_r - 1).astype(F32)
    valid_r = (slot_r >= 1) & (slot_r <= n_tok)
    ii = lax.broadcasted_iota(jnp.int32, (sl, sl), 0)
    jj = lax.broadcasted_iota(jnp.int32, (sl, sl), 1)
    diff = (ii - jj).astype(F32)
    pair_ok = (diff >= 0.0) & (jj >= 1) & (jj <= n_tok)
    for h in range(nh):
        lg = jnp.full((1, 1), lg_ref[h], F32)
        dmask = jnp.where(pair_ok, jnp.exp(jnp.maximum(diff, 0.0) * lg), 0.0)
        cross_decay = jnp.exp((tok_r + 1.0) * lg)
        key_decay = jnp.where(valid_r, jnp.exp((n_tok - 1.0 - tok_r) * lg), 0.0)
        chunk_decay = jnp.exp(float(n_tok) * lg)
        for b in range(bs):
            rows = slice(b * sl, (b + 1) * sl)
            cols = slice(h * hd, (h + 1) * hd)
            q = _rotary(q_ref[rows, cols], cos, sin)
            k = _rotary(k_ref[rows, cols], cos, sin) * (hd ** -0.5)
            vb = v_ref[rows, cols].astype(BF16)
            s = s0_ref[b, h]
            qb = q.astype(BF16)
            scores = _dot_nt(qb, k.astype(BF16)) * dmask
            inner = _dot(scores.astype(BF16), vb)
            cross = _dot(qb, s.astype(BF16)) * cross_decay
            kd_t = (k * key_decay).T.astype(BF16)
            s_ref[b, h] = s * chunk_decay + _dot(kd_t, vb)
            o = inner + cross
            y_ref[rows, cols] = (_group_norm_rows(o, GN_EPS) * _silu(g_ref[rows, cols])).astype(BF16)


def _ret_sample(proj, row0, lg, cos, sin, s0, nh, n_tok):
    bsz = s0.shape[0]
    hd = RET_HEAD_DIM
    dr = nh * hd
    bs = _tile(bsz, 4, 2)
    rows = bs * SAMPLE_SLOTS
    assert row0 % rows == 0
    t0 = row0 // rows
    sect = lambda s: pl.BlockSpec((rows, dr), lambda i, s=s: (t0 + i, s))
    tab = pl.BlockSpec((SAMPLE_SLOTS, hd // 2), lambda i: (0, 0))
    st = pl.BlockSpec((bs, nh, hd, hd), lambda i: (i, 0, 0, 0))
    return pl.pallas_call(
        functools.partial(_ret_sample_body, nh=nh, n_tok=n_tok),
        grid=(bsz // bs,),
        in_specs=[pl.BlockSpec(memory_space=pltpu.SMEM), sect(0), sect(1), sect(2), sect(3), tab, tab, st],
        out_specs=[pl.BlockSpec((rows, dr), lambda i: (i, 0)), st],
        out_shape=[jax.ShapeDtypeStruct((bsz * SAMPLE_SLOTS, dr), BF16),
                   jax.ShapeDtypeStruct(s0.shape, F32)],
        compiler_params=_params(("parallel",)),
        name="retention_sample",
    )(lg, proj, proj, proj, proj, cos, sin, s0)


def _out_core(yr_ref, yw_ref, w1_ref, w2_ref, x, gate, fw):
    y = _dot(yr_ref[...], w1_ref[...]) + _dot(yw_ref[...], w2_ref[...])
    xn = x + gate * y
    ms = jnp.mean(xn * xn, axis=-1, keepdims=True)
    return xn * lax.rsqrt(ms + NORM_EPS) * fw


def _out_prompt_body(yr_ref, yw_ref, w1_ref, w2_ref, x_ref, mod_ref, fw_ref, o_ref):
    o_ref[0] = _out_core(yr_ref, yw_ref, w1_ref, w2_ref, x_ref[0], mod_ref[0][2:3], fw_ref[...])


def _out_prompt(y_ret, y_rw, w1, w2, x, mod3, fw):
    b, l, d = x.shape
    dr = w1.shape[0]
    tl = _tile(l, 512, 16)
    nt = l // tl
    wspec = pl.BlockSpec((dr, d), lambda bi, i: (0, 0), pipeline_mode=pl.Buffered(1))
    yspec = pl.BlockSpec((tl, dr), lambda bi, i: (bi * nt + i, 0))
    return pl.pallas_call(
        _out_prompt_body,
        grid=(b, nt),
        in_specs=[yspec, yspec, wspec, wspec,
                  pl.BlockSpec((1, tl, d), lambda bi, i: (bi, i, 0)),
                  pl.BlockSpec((1, 3, d), lambda bi, i: (bi, 0, 0)),
                  pl.BlockSpec((1, d), lambda bi, i: (0, 0))],
        out_specs=pl.BlockSpec((1, tl, d), lambda bi, i: (bi, i, 0)),
        out_shape=jax.ShapeDtypeStruct((b, l, d), F32),
        compiler_params=_params(("parallel", "parallel")),
        name="out_proj_prompt",
    )(y_ret, y_rw, w1, w2, x, mod3, fw.reshape(1, d))


def _out_sample_body(yr_ref, yw_ref, w1_ref, w2_ref, x_ref, mod_ref, fw_ref, o_ref):
    bs, sl, d = x_ref.shape
    gate = jnp.broadcast_to(mod_ref[...][:, 2:3, :], (bs, sl, d)).reshape(bs * sl, d)
    x = x_ref[...].reshape(bs * sl, d)
    o = _out_core(yr_ref, yw_ref, w1_ref, w2_ref, x, gate, fw_ref[...])
    o_ref[...] = o.reshape(bs, sl, d)


def _out_sample(y_ret, y_rw, w1, w2, x_slots, mod3, fw):
    bsz, sl, d = x_slots.shape
    dr = w1.shape[0]
    bs = _tile(bsz, 32, SUBLANES)
    wspec = pl.BlockSpec((dr, d), lambda i: (0, 0))
    yspec = pl.BlockSpec((bs * sl, dr), lambda i: (i, 0))
    xspec = pl.BlockSpec((bs, sl, d), lambda i: (i, 0, 0))
    return pl.pallas_call(
        _out_sample_body,
        grid=(bsz // bs,),
        in_specs=[yspec, yspec, wspec, wspec, xspec,
                  pl.BlockSpec((bs, 3, d), lambda i: (i, 0, 0)),
                  pl.BlockSpec((1, d), lambda i: (0, 0))],
        out_specs=xspec,
        out_shape=jax.ShapeDtypeStruct((bsz, sl, d), F32),
        compiler_params=_params(("parallel",)),
        name="out_proj_sample",
    )(y_ret, y_rw, w1, w2, x_slots, mod3, fw.reshape(1, d))


def _rope_tables(pos, half):
    inv_freq = ROPE_THETA ** (-jnp.arange(half, dtype=F32) / half)
    ang = pos[:, None] * inv_freq[None, :]
    return jnp.cos(ang), jnp.sin(ang)


def kernel(x_prompt, x_sample, c_prompt, c_sample, state_ret, state_rwkv, state_shift, norm_w, w_ada,
           b_ada, w_in, mu_shift, w0_decay, w2_decay, a0, a2, k_k, k_a, r_k, ln_x_w, ln_x_b, w_out,
           final_norm_w):
    depth = w_in.shape[0]
    assert depth == 1, "single-layer trunk"
    bp, lp, d = x_prompt.shape
    bsz, ls, _ = x_sample.shape
    assert ls == 4, "sample path packs 4 tokens into slots 1..4"
    dr = d
    nh_ret = dr // RET_HEAD_DIM
    n_main = 8 * dr
    lora = w2_decay.shape[1]
    lora_pad = -(-2 * lora // LANES) * LANES

    w_t = jnp.swapaxes(w_in[0], 0, 1)
    w_t_lora = jnp.pad(w_t[n_main:], ((0, lora_pad - 2 * lora), (0, 0)))
    w_o = w_out[0].astype(BF16)
    w_o_ret, w_o_rw = w_o[:dr], w_o[dr:]
    mu = mu_shift[0]
    row = lambda p: p.reshape(1, -1)
    mu_r, mu_k, mu_v = row(mu[0:dr]), row(mu[dr:2 * dr]), row(mu[2 * dr:3 * dr])
    mu_l = jnp.pad(mu[3 * dr:], (0, lora_pad - 2 * lora)).reshape(1, lora_pad)
    w2p = jnp.pad(w2_decay[0], ((0, lora_pad - lora), (0, 0))).astype(BF16)
    a2p = jnp.pad(a2[0], ((lora, lora_pad - 2 * lora), (0, 0))).astype(BF16)
    lg = jnp.log1p(-jnp.exp2(-5.0 - jnp.arange(nh_ret, dtype=F32)))
    rw_params = (mu_r, mu_k, mu_v, mu_l, row(w0_decay[0]), w2p, row(a0[0]), a2p, row(k_k[0]), row(k_a[0]),
                 row(r_k[0]), row(ln_x_w[0]), row(ln_x_b[0]))

    n_c = bp + bsz
    n_c_pad = -(-n_c // SUBLANES) * SUBLANES
    c_all = jnp.pad(jnp.concatenate([c_prompt, c_sample], axis=0), ((0, n_c_pad - n_c), (0, 0)))
    mod3 = _adaln(c_all, w_ada[0], b_ada[0]).reshape(n_c_pad, 3, d)
    mod_p, mod_s = mod3[:bp], mod3[bp:bp + bsz]

    sl = SAMPLE_SLOTS
    row_s = bp * lp
    x_slots = jnp.pad(x_sample, ((0, 0), (1, sl - 1 - ls), (0, 0)))
    h_all, last_p, new_shift_s = _modulate(x_prompt, mod_p, x_slots, state_shift[0], mod_s, norm_w[0])
    new_shift_p = last_p[:, SUBLANES - 1, :]
    proj = _in_proj(h_all, w_t, n_main)
    lora_proj = _in_proj(h_all, w_t_lora, lora_pad, name="in_proj_lora")

    cos_p, sin_p = _rope_tables(jnp.arange(lp, dtype=F32), RET_HEAD_DIM // 2)
    y_ret_p, s_ret_p = _ret_prompt(proj, lg, cos_p, sin_p, bp, lp, nh_ret)
    y_rw_p, s_rw_p = _rwkv_prompt(proj, lora_proj, bp, lp, dr, *rw_params)
    y_prompt = _out_prompt(y_ret_p, y_rw_p, w_o_ret, w_o_rw, x_prompt, mod_p, final_norm_w)

    slot_pos = jnp.arange(sl, dtype=F32) - 1.0
    pos_s = jnp.where((slot_pos >= 0) & (slot_pos < ls), float(PAST_LEN) + slot_pos, 0.0)
    cos_s, sin_s = _rope_tables(pos_s, RET_HEAD_DIM // 2)
    y_ret_s, s_ret_s = _ret_sample(proj, row_s, lg, cos_s, sin_s, state_ret[0], nh_ret, ls)
    y_rw_s, s_rw_s = _rwkv_sample(proj, lora_proj, row_s, state_rwkv[0], ls, dr, *rw_params)
    y_slots = _out_sample(y_ret_s, y_rw_s, w_o_ret, w_o_rw, x_slots, mod_s, final_norm_w)
    y_sample = y_slots[:, 1:1 + ls, :]

    return (y_prompt, y_sample, s_ret_p[None], s_rw_p[None], new_shift_p[None],
            s_ret_s[None], s_rw_s[None], new_shift_s[None])
```

```python
import functools
import math

import jax
import jax.numpy as jnp
from jax import lax
from jax.experimental import pallas as pl
from jax.experimental.pallas import tpu as pltpu

F32 = jnp.float32
BF16 = jnp.bfloat16

RET_HEAD_DIM = 256
RWKV_HEAD_DIM = 64
RET_CHUNK = 128
RWKV_CHUNK = 64
RWKV_GROUP = 2
RWKV_INTERLEAVE = 16
PAST_LEN = 16384
ROPE_THETA = 10000.0
NORM_EPS = 1e-6
GN_EPS = 1e-5
RWKV_GN_EPS = 64e-5
SAMPLE_SLOTS = 8
LANES = 128
SUBLANES = 8
VMEM_LIMIT_BYTES = 56 * 1024 * 1024


def _params(sem):
    return pltpu.CompilerParams(dimension_semantics=sem, vmem_limit_bytes=VMEM_LIMIT_BYTES)


def _tile(n, cap, align):
    if n <= cap:
        return n
    t = (cap // align) * align
    while t >= align:
        if n % t == 0:
            return t
        t -= align
    return n


def _silu(x):
    return x * jax.nn.sigmoid(x)


def _dot(a, b):
    return jnp.dot(a, b, preferred_element_type=F32)


def _dot_nt(a, b):
    return lax.dot_general(a, b, (((1,), (1,)), ((), ())), preferred_element_type=F32)


def _dot_tn(a, b):
    return lax.dot_general(a, b, (((0,), (0,)), ((), ())), preferred_element_type=F32)


def _adaln_body(c_ref, w_ref, b_ref, o_ref):
    s = _silu(c_ref[...]).astype(BF16)
    o_ref[...] = _dot(s, w_ref[...].astype(BF16)) + b_ref[...]


def _adaln(c, w_ada, b_ada):
    rows, d = c.shape
    n = w_ada.shape[1]
    tn = _tile(n, 768, LANES)
    return pl.pallas_call(
        _adaln_body,
        grid=(n // tn,),
        in_specs=[pl.BlockSpec((rows, d), lambda j: (0, 0)),
                  pl.BlockSpec((d, tn), lambda j: (0, j)),
                  pl.BlockSpec((1, tn), lambda j: (0, j))],
        out_specs=pl.BlockSpec((rows, tn), lambda j: (0, j)),
        out_shape=jax.ShapeDtypeStruct((rows, n), F32),
        compiler_params=_params(("parallel",)),
        name="adaln",
    )(c, w_ada, b_ada.reshape(1, n))


def _modulated(x, nw, shift, scale):
    ms = jnp.mean(x * x, axis=-1, keepdims=True)
    return x * lax.rsqrt(ms + NORM_EPS) * (nw * (1.0 + scale)) + shift


def _modulate_body(xp_ref, modp_ref, xs_ref, prev_ref, mods_ref, nw_ref, h_ref, last_ref, new_ref, *, n_prompt):
    s = pl.program_id(0)

    @pl.when(s < n_prompt)
    def _():
        m = modp_ref[0]
        h = _modulated(xp_ref[0], nw_ref[...], m[0:1], m[1:2])
        h_ref[...] = h.astype(BF16)
        tl = h.shape[0]
        last_ref[0] = h[tl - SUBLANES:tl]

    @pl.when(s >= n_prompt)
    def _():
        x = xs_ref[...]
        m = mods_ref[...]
        h = _modulated(x, nw_ref[...], m[:, 0:1, :], m[:, 1:2, :])
        slot = lax.broadcasted_iota(jnp.int32, h.shape, 1)
        full = jnp.where(slot == 0, prev_ref[...][:, None, :], h)
        bs, sl, d = x.shape
        h_ref[...] = full.reshape(bs * sl, d).astype(BF16)
        new_ref[...] = h[:, 4, :]


def _modulate(x_prompt, mod_p, x_slots, prev, mod_s, nw):
    b, l, d = x_prompt.shape
    bsz, sl, _ = x_slots.shape
    tl = _tile(l, min(512, bsz * sl), 16)
    nt = l // tl
    bs = tl // sl
    assert bsz % bs == 0 and bs % SUBLANES == 0
    n_prompt = b * nt
    n_sample = bsz // bs
    samp = lambda s: jnp.maximum(s - n_prompt, 0)
    pb = lambda s: jnp.minimum(s // nt, b - 1)
    return pl.pallas_call(
        functools.partial(_modulate_body, n_prompt=n_prompt),
        grid=(n_prompt + n_sample,),
        in_specs=[pl.BlockSpec((1, tl, d), lambda s: (pb(s), jnp.where(s < n_prompt, s % nt, nt - 1), 0)),
                  pl.BlockSpec((1, 3, d), lambda s: (pb(s), 0, 0)),
                  pl.BlockSpec((bs, sl, d), lambda s: (samp(s), 0, 0)),
                  pl.BlockSpec((bs, d), lambda s: (samp(s), 0)),
                  pl.BlockSpec((bs, 3, d), lambda s: (samp(s), 0, 0)),
                  pl.BlockSpec((1, d), lambda s: (0, 0))],
        out_specs=[pl.BlockSpec((tl, d), lambda s: (s, 0)),
                   pl.BlockSpec((1, SUBLANES, d), lambda s: (pb(s), 0, 0)),
                   pl.BlockSpec((bs, d), lambda s: (samp(s), 0))],
        out_shape=[jax.ShapeDtypeStruct((b * l + bsz * sl, d), BF16),
                   jax.ShapeDtypeStruct((b, SUBLANES, d), F32),
                   jax.ShapeDtypeStruct((bsz, d), F32)],
        compiler_params=_params(("arbitrary",)),
        name="modulate",
    )(x_prompt, mod_p, x_slots, prev, mod_s, nw.reshape(1, d))


def _in_proj_body(x_ref, wt_ref, o_ref, w_scr):
    @pl.when(pl.program_id(1) == 0)
    def _():
        w_scr[...] = wt_ref[...].astype(BF16)

    o_ref[...] = _dot_nt(x_ref[...], w_scr[...])


def _in_proj(x, wt, n, name="in_proj"):
    m, k = x.shape
    tm = _tile(m, 1536, 16)
    tn = _tile(n, 1024, LANES)
    return pl.pallas_call(
        _in_proj_body,
        grid=(n // tn, m // tm),
        in_specs=[pl.BlockSpec((tm, k), lambda j, i: (i, 0)),
                  pl.BlockSpec((tn, k), lambda j, i: (j, 0))],
        out_specs=pl.BlockSpec((tm, tn), lambda j, i: (i, j)),
        out_shape=jax.ShapeDtypeStruct((m, n), F32),
        scratch_shapes=[pltpu.VMEM((tn, k), BF16)],
        compiler_params=_params(("arbitrary", "arbitrary")),
        name=name,
    )(x, wt)


def _token_shift(cur_ref, carry_ref, mu_ref):
    cur = cur_ref[...]
    tl = cur.shape[0]
    prev = pltpu.roll(cur, 1, 0)
    if carry_ref is not None:
        row = lax.broadcasted_iota(jnp.int32, cur.shape, 0)
        prev = jnp.where(row == 0, carry_ref[0:1, :], prev)
        carry_ref[0:1, :] = cur[tl - 1:tl, :]
    return cur + (prev - cur) * mu_ref[...]


def _decay_and_rate(lo, w0_ref, w2_ref, a0_ref, a2_ref):
    dec = _dot(jnp.tanh(lo).astype(BF16), w2_ref[...])
    logw = -math.exp(-0.5) * jax.nn.sigmoid(w0_ref[...] + dec)
    a = jax.nn.sigmoid(a0_ref[...] + _dot(lo.astype(BF16), a2_ref[...]))
    return logw, a


def _zero_refs(*refs):
    for r in refs:
        r[...] = jnp.zeros_like(r)


def _split3(x):
    hi = x.astype(BF16)
    r1 = x - hi.astype(F32)
    mid = r1.astype(BF16)
    lo = (r1 - mid.astype(F32)).astype(BF16)
    return hi, mid, lo


def _dot_split3(m, parts):
    return _dot(m, parts[0]) + _dot(m, parts[1]) + _dot(m, parts[2])


def _rwkv_tile(pr_ref, pk_ref, pv_ref, pg_ref, pl_ref, mu_r, mu_k, mu_v, mu_l, w0_ref, w2_ref,
               a0_ref, a2_ref, kk_ref, ka_ref, rk_ref, lnw_ref, lnb_ref, y_ref,
               carries, seq, valid_slots, load_state, store_state):
    c = pr_ref.shape[0]
    hd = RWKV_HEAD_DIM
    gw = RWKV_GROUP * hd
    groups = range(pr_ref.shape[1] // gw)
    nseq = c // seq
    seqs = range(nseq)
    rows_of = [slice(q * seq, (q + 1) * seq) for q in seqs]
    c_r, c_k, c_v, c_l = carries

    r_all = _token_shift(pr_ref, c_r, mu_r)
    kw_all = _token_shift(pk_ref, c_k, mu_k)
    v_all = _token_shift(pv_ref, c_v, mu_v)
    lo = _token_shift(pl_ref, c_l, mu_l)
    logw_all, a_all = _decay_and_rate(lo, w0_ref, w2_ref, a0_ref, a2_ref)
    k_all = kw_all * (1.0 + (a_all - 1.0) * ka_ref[...])
    kkr_all = kw_all * kk_ref[...]
    if valid_slots is not None:
        slot = lax.broadcasted_iota(jnp.int32, (c, 1), 0) % seq
        valid = (slot >= valid_slots[0]) & (slot <= valid_slots[1])
        logw_all = jnp.where(valid, logw_all, 0.0)
        kkr_all = jnp.where(valid, kkr_all, 0.0)
        k_all = jnp.where(valid, k_all, 0.0)

    ti = lax.broadcasted_iota(jnp.int32, (c, c), 0)
    tj = lax.broadcasted_iota(jnp.int32, (c, c), 1)
    same_seq = (ti // seq) == (tj // seq)
    logw_parts = _split3(logw_all)
    cum_all = _dot_split3(((ti >= tj) & same_seq).astype(BF16), logw_parts)
    if nseq > 1:
        cum_end_all = _dot_split3(same_seq.astype(BF16), logw_parts)

    hr = lax.broadcasted_iota(jnp.int32, (gw, gw), 0) // hd
    hc = lax.broadcasted_iota(jnp.int32, (gw, gw), 1) // hd
    headmask = hr == hc
    tok = lax.broadcasted_iota(jnp.int32, (c, gw), 0)
    src = lax.broadcasted_iota(jnp.int32, (c, gw), 1) % hd
    same = (tok // seq) == (src // seq)
    strict = (tok > src) & same
    incl = (tok >= src) & same

    def bd(x):
        t = jnp.concatenate([x.astype(BF16)] * RWKV_GROUP, axis=0)
        return jnp.where(headmask, t, jnp.zeros_like(t))

    lane_head = lax.broadcasted_iota(jnp.int32, (c, gw), 1) // hd

    def head_sum(x):
        out = jnp.zeros_like(x)
        for j in range(RWKV_GROUP):
            mine = lane_head == j
            out = jnp.where(mine, jnp.sum(jnp.where(mine, x, 0.0), axis=-1, keepdims=True), out)
        return out

    def stages(gids):
        groups = range(len(gids))
        sls = [slice(g * gw, (g + 1) * gw) for g in gids]
        r = [r_all[:, sl] for sl in sls]
        k = [k_all[:, sl] for sl in sls]
        v = [v_all[:, sl] for sl in sls]
        cum = [cum_all[:, sl] for sl in sls]
        if nseq > 1:
            cum_end = [cum_end_all[:, sl] for sl in sls]
        else:
            cum_end = [cm[c - 1:c, :] for cm in cum]

        kk_sq = [head_sum(jnp.square(kkr_all[:, sl])) for sl in sls]
        rk_sum = [head_sum(r[g] * k[g] * rk_ref[:, sls[g]]) for g in groups]
        kkn = [kkr_all[:, sls[g]] * lax.rsqrt(jnp.maximum(kk_sq[g], 1e-24)) for g in groups]
        b = [kkn[g] * a_all[:, sls[g]] for g in groups]
        e_inv = [jnp.exp(-cum[g]) for g in groups]
        at = [-kkn[g] * jnp.exp(cum[g] - logw_all[:, sls[g]]) for g in groups]
        rt = [r[g] * jnp.exp(cum[g]) for g in groups]
        ar = [jnp.concatenate([at[g], rt[g]], axis=0).astype(BF16) for g in groups]
        states = [load_state(g) for g in gids]

        def from_state(g):
            if nseq == 1:
                fs = _dot_nt(ar[g], states[g][0].astype(BF16))
                return fs[:c], fs[c:]
            fa, fr = [], []
            for q in seqs:
                arq = jnp.concatenate([at[g][rows_of[q]], rt[g][rows_of[q]]], axis=0).astype(BF16)
                fs = _dot_nt(arq, states[g][q].astype(BF16))
                fa.append(fs[:seq])
                fr.append(fs[seq:])
            return jnp.concatenate(fa, axis=0), jnp.concatenate(fr, axis=0)

        fstate = [from_state(g) for g in groups]
        s_b = [_dot_nt(ar[g], bd(b[g] * e_inv[g])) for g in groups]
        s_k = [_dot_nt(ar[g], bd(k[g] * e_inv[g])) for g in groups]
        a_ab = [jnp.where(strict, x[:c], 0.0) for x in s_b]
        a_ak = [jnp.where(strict, x[:c], 0.0).astype(BF16) for x in s_k]
        a_r = [jnp.concatenate([jnp.where(incl, s_b[g][c:], 0.0), jnp.where(incl, s_k[g][c:], 0.0)],
                               axis=1).astype(BF16) for g in groups]

        steps = max(1, int(math.ceil(math.log2(seq))))
        pw = a_ab
        pm = a_ab
        pw_next = [_dot(pw[g].astype(BF16), bd(pw[g])) for g in groups] if steps > 1 else None
        for s in range(1, steps):
            pw = pw_next
            if s < steps - 1:
                res = [_dot(jnp.concatenate([pw[g], pm[g]], axis=0).astype(BF16), bd(pw[g])) for g in groups]
                pw_next = [x[:c] for x in res]
                pm = [pm[g] + pw[g] + res[g][c:] for g in groups]
            else:
                pm = [pm[g] + pw[g] + _dot(pm[g].astype(BF16), bd(pw[g])) for g in groups]

        bd_v = [bd(x) for x in v]
        w_rhs = [fstate[g][0] + _dot(a_ak[g], bd_v[g]) for g in groups]
        u = [w_rhs[g] + _dot(pm[g].astype(BF16), bd(w_rhs[g])) for g in groups]
        o = [fstate[g][1] + _dot(a_r[g], jnp.concatenate([bd(u[g]), bd_v[g]], axis=0)) for g in groups]
        for g in groups:
            e_end = jnp.exp(cum_end[g] - cum[g])
            b_end = b[g] * e_end
            k_end = k[g] * e_end
            for q in seqs:
                rq = rows_of[q]
                uv = jnp.concatenate([u[g][rq], v[g][rq]], axis=0).astype(BF16)
                bk = jnp.concatenate([b_end[rq], k_end[rq]], axis=0).astype(BF16)
                keep = jnp.exp(cum_end[g][q * seq:q * seq + 1, :]) if nseq > 1 else jnp.exp(cum_end[g])
                store_state(gids[g], q, states[g][q] * keep + _dot_tn(uv, bk))

        mean = [head_sum(x) * (1.0 / hd) for x in o]
        oc = [o[g] - mean[g] for g in groups]
        var = [head_sum(jnp.square(x)) * (1.0 / hd) for x in oc]
        for g in groups:
            sl = sls[g]
            og = oc[g] * lax.rsqrt(var[g] + RWKV_GN_EPS) * lnw_ref[:, sl] + lnb_ref[:, sl]
            y_ref[:, sl] = ((og + rk_sum[g] * v[g]) * _silu(pg_ref[:, sl])).astype(BF16)

    for start in range(0, len(groups), RWKV_INTERLEAVE):
        stages(list(groups)[start:start + RWKV_INTERLEAVE])


def _head_mask(gw, hd):
    hr = lax.broadcasted_iota(jnp.int32, (gw, gw), 0) // hd
    hc = lax.broadcasted_iota(jnp.int32, (gw, gw), 1) // hd
    return hr == hc


def _rwkv_prompt_body(*refs):
    ins, (y_ref, so_ref), (c_r, c_k, c_v, c_l, s_scr) = refs[:18], refs[18:20], refs[20:]
    n = pl.program_id(1)
    hd = RWKV_HEAD_DIM
    gw = RWKV_GROUP * hd
    headmask = _head_mask(gw, hd)

    @pl.when(n == 0)
    def _():
        _zero_refs(c_r, c_k, c_v, c_l, s_scr)

    def load_state(g):
        return [s_scr[g]]

    def store_state(g, q, s):
        s_scr[g] = jnp.where(headmask, s, 0.0)

    _rwkv_tile(*ins, y_ref, (c_r, c_k, c_v, c_l), ins[0].shape[0], None, load_state, store_state)

    @pl.when(n == pl.num_programs(1) - 1)
    def _():
        for g in range(s_scr.shape[0]):
            sg = s_scr[g]
            for j in range(RWKV_GROUP):
                so_ref[0, g * RWKV_GROUP + j] = sg[j * hd:(j + 1) * hd, j * hd:(j + 1) * hd]


def _rwkv_prompt(proj, lora, nb, l, dr, mu_r, mu_k, mu_v, mu_l, w0, w2p, a0, a2p, k_k, k_a, r_k, ln_w, ln_b):
    c = RWKV_CHUNK
    assert l % c == 0 and c == RWKV_HEAD_DIM
    nchunk = l // c
    lp = lora.shape[1]
    nh = dr // RWKV_HEAD_DIM
    gw = RWKV_GROUP * RWKV_HEAD_DIM
    vec = lambda: pl.BlockSpec((1, dr), lambda b, n: (0, 0))
    sect = lambda s: pl.BlockSpec((c, dr), lambda b, n, s=s: (b * nchunk + n, 4 + s))
    return pl.pallas_call(
        _rwkv_prompt_body,
        grid=(nb, nchunk),
        in_specs=[sect(0), sect(1), sect(2), sect(3),
                  pl.BlockSpec((c, lp), lambda b, n: (b * nchunk + n, 0)),
                  vec(), vec(), vec(),
                  pl.BlockSpec((1, lp), lambda b, n: (0, 0)),
                  vec(),
                  pl.BlockSpec((lp, dr), lambda b, n: (0, 0)),
                  vec(),
                  pl.BlockSpec((lp, dr), lambda b, n: (0, 0)),
                  vec(), vec(), vec(), vec(), vec()],
        out_specs=[pl.BlockSpec((c, dr), lambda b, n: (b * nchunk + n, 0)),
                   pl.BlockSpec((1, nh, RWKV_HEAD_DIM, RWKV_HEAD_DIM), lambda b, n: (b, 0, 0, 0))],
        out_shape=[jax.ShapeDtypeStruct((nb * l, dr), BF16),
                   jax.ShapeDtypeStruct((nb, nh, RWKV_HEAD_DIM, RWKV_HEAD_DIM), F32)],
        scratch_shapes=[pltpu.VMEM((SUBLANES, dr), F32), pltpu.VMEM((SUBLANES, dr), F32),
                        pltpu.VMEM((SUBLANES, dr), F32), pltpu.VMEM((SUBLANES, lp), F32),
                        pltpu.VMEM((dr // gw, gw, gw), F32)],
        compiler_params=_params(("parallel", "arbitrary")),
        name="rwkv_chunked",
    )(proj, proj, proj, proj, lora, mu_r, mu_k, mu_v, mu_l, w0, w2p, a0, a2p, k_k, k_a, r_k, ln_w, ln_b)


def _rwkv_sample_body(*refs, n_tok):
    ins, s0_ref, y_ref, so_ref = refs[:18], refs[18], refs[19], refs[20]
    hd = RWKV_HEAD_DIM
    zero = jnp.zeros((hd, hd), F32)

    def load_state(g):
        out = []
        for q in range(s0_ref.shape[0]):
            rows = [jnp.concatenate([s0_ref[q, g * RWKV_GROUP + j] if i == j else zero
                                     for i in range(RWKV_GROUP)], axis=1) for j in range(RWKV_GROUP)]
            out.append(jnp.concatenate(rows, axis=0))
        return out

    def store_state(g, q, s):
        for j in range(RWKV_GROUP):
            so_ref[q, g * RWKV_GROUP + j] = s[j * hd:(j + 1) * hd, j * hd:(j + 1) * hd]

    _rwkv_tile(*ins, y_ref, (None, None, None, None), SAMPLE_SLOTS, (1, n_tok), load_state, store_state)


def _rwkv_sample(proj, lora, row0, s0, n_tok, dr, mu_r, mu_k, mu_v, mu_l, w0, w2p, a0, a2p, k_k, k_a,
                 r_k, ln_w, ln_b):
    c = RWKV_CHUNK
    rows = s0.shape[0] * SAMPLE_SLOTS
    assert rows % c == 0 and row0 % c == 0 and c == RWKV_HEAD_DIM
    t0 = row0 // c
    nseq = c // SAMPLE_SLOTS
    lp = lora.shape[1]
    hd = RWKV_HEAD_DIM
    dw = _tile(dr, 1024, RWKV_GROUP * hd)
    nw = dr // dw
    vec = lambda: pl.BlockSpec((1, dw), lambda i, hf: (0, hf))
    sect = lambda s: pl.BlockSpec((c, dw), lambda i, hf, s=s: (t0 + i, (4 + s) * nw + hf))
    lora_w = lambda: pl.BlockSpec((lp, dw), lambda i, hf: (0, hf))
    st = pl.BlockSpec((nseq, dw // hd, hd, hd), lambda i, hf: (i, hf, 0, 0))
    return pl.pallas_call(
        functools.partial(_rwkv_sample_body, n_tok=n_tok),
        grid=(rows // c, nw),
        in_specs=[sect(0), sect(1), sect(2), sect(3),
                  pl.BlockSpec((c, lp), lambda i, hf: (t0 + i, 0)),
                  vec(), vec(), vec(),
                  pl.BlockSpec((1, lp), lambda i, hf: (0, 0)),
                  vec(), lora_w(), vec(), lora_w(),
                  vec(), vec(), vec(), vec(), vec(), st],
        out_specs=[pl.BlockSpec((c, dw), lambda i, hf: (i, hf)), st],
        out_shape=[jax.ShapeDtypeStruct((rows, dr), BF16),
                   jax.ShapeDtypeStruct(s0.shape, F32)],
        compiler_params=_params(("parallel", "parallel")),
        name="rwkv_chunked_sample",
    )(proj, proj, proj, proj, lora, mu_r, mu_k, mu_v, mu_l, w0, w2p, a0, a2p, k_k, k_a, r_k, ln_w, ln_b, s0)


def _rotary(x, cos, sin):
    half = x.shape[-1] // 2
    x1 = x[:, :half]
    x2 = x[:, half:]
    return jnp.concatenate([x1 * cos - x2 * sin, x1 * sin + x2 * cos], axis=-1)


def _group_norm_rows(o, eps):
    mean = jnp.mean(o, axis=-1, keepdims=True)
    oc = o - mean
    var = jnp.mean(oc * oc, axis=-1, keepdims=True)
    return oc * lax.rsqrt(var + eps)


def _ret_prompt_body(lg_ref, q_ref, k_ref, v_ref, g_ref, cos_ref, sin_ref, y_ref, s_ref):
    n = pl.program_id(1)
    c = q_ref.shape[0]
    hd = RET_HEAD_DIM
    heads = range(q_ref.shape[1] // hd)
    cols = [slice(h * hd, (h + 1) * hd) for h in heads]

    @pl.when(n == 0)
    def _():
        s_ref[...] = jnp.zeros_like(s_ref)

    cos = cos_ref[...]
    sin = sin_ref[...]
    ii = lax.broadcasted_iota(jnp.int32, (c, c), 0)
    jj = lax.broadcasted_iota(jnp.int32, (c, c), 1)
    diff = (ii - jj).astype(F32)
    causal = diff >= 0.0
    dist = jnp.maximum(diff, 0.0)
    ic = lax.broadcasted_iota(jnp.int32, (c, 1), 0).astype(F32)

    lg = [jnp.full((1, 1), lg_ref[h], F32) for h in heads]
    q = [_rotary(q_ref[:, cs], cos, sin).astype(BF16) for cs in cols]
    k = [_rotary(k_ref[:, cs], cos, sin) * (hd ** -0.5) for cs in cols]
    vb = [v_ref[:, cs].astype(BF16) for cs in cols]
    s = [s_ref[0, h] for h in heads]
    scores = [_dot_nt(q[h], k[h].astype(BF16)) * jnp.where(causal, jnp.exp(dist * lg[h]), 0.0) for h in heads]
    cross = [_dot(q[h], s[h].astype(BF16)) * jnp.exp((ic + 1.0) * lg[h]) for h in heads]
    o = [_dot(scores[h].astype(BF16), vb[h]) + cross[h] for h in heads]
    for h in heads:
        kd = (k[h] * jnp.exp((c - 1.0 - ic) * lg[h])).astype(BF16)
        s_ref[0, h] = s[h] * jnp.exp(float(c) * lg[h]) + _dot_tn(kd, vb[h])
    for h in heads:
        y_ref[:, cols[h]] = (_group_norm_rows(o[h], GN_EPS) * _silu(g_ref[:, cols[h]])).astype(BF16)


def _ret_prompt(proj, lg, cos, sin, nb, l, nh):
    hd = RET_HEAD_DIM
    dr = nh * hd
    c = math.gcd(l, RET_CHUNK)
    nchunk = l // c
    sect = lambda s: pl.BlockSpec((c, dr), lambda b, n, s=s: (b * nchunk + n, s))
    tab = pl.BlockSpec((c, hd // 2), lambda b, n: (n, 0))
    return pl.pallas_call(
        _ret_prompt_body,
        grid=(nb, nchunk),
        in_specs=[pl.BlockSpec(memory_space=pltpu.SMEM), sect(0), sect(1), sect(2), sect(3), tab, tab],
        out_specs=[pl.BlockSpec((c, dr), lambda b, n: (b * nchunk + n, 0)),
                   pl.BlockSpec((1, nh, hd, hd), lambda b, n: (b, 0, 0, 0))],
        out_shape=[jax.ShapeDtypeStruct((nb * l, dr), BF16),
                   jax.ShapeDtypeStruct((nb, nh, hd, hd), F32)],
        compiler_params=_params(("parallel", "arbitrary")),
        name="retention_prompt",
    )(lg, proj, proj, proj, proj, cos, sin)


def _ret_sample_body(lg_ref, q_ref, k_ref, v_ref, g_ref, cos_ref, sin_ref, s0_ref, y_ref, s_ref, *, nh, n_tok):
    hd = RET_HEAD_DIM
    bs = s0_ref.shape[0]
    sl = SAMPLE_SLOTS
    cos = cos_ref[...]
    sin = sin_ref[...]
    slot_r = lax.broadcasted_iota(jnp.int32, (sl, 1), 0)
    tok_r = (slot_r - 1).astype(F32)
    valid_r = (slot_r >= 1) & (slot_r <= n_tok)
    ii = lax.broadcasted_iota(jnp.int32, (sl, sl), 0)
    jj = lax.broadcasted_iota(jnp.int32, (sl, sl), 1)
    diff = (ii - jj).astype(F32)
    pair_ok = (diff >= 0.0) & (jj >= 1) & (jj <= n_tok)
    for h in range(nh):
        lg = jnp.full((1, 1), lg_ref[h], F32)
        dmask = jnp.where(pair_ok, jnp.exp(jnp.maximum(diff, 0.0) * lg), 0.0)
        cross_decay = jnp.exp((tok_r + 1.0) * lg)
        key_decay = jnp.where(valid_r, jnp.exp((n_tok - 1.0 - tok_r) * lg), 0.0)
        chunk_decay = jnp.exp(float(n_tok) * lg)
        for b in range(bs):
            rows = slice(b * sl, (b + 1) * sl)
            cols = slice(h * hd, (h + 1) * hd)
            q = _rotary(q_ref[rows, cols], cos, sin)
            k = _rotary(k_ref[rows, cols], cos, sin) * (hd ** -0.5)
            vb = v_ref[rows, cols].astype(BF16)
            s = s0_ref[b, h]
            qb = q.astype(BF16)
            scores = _dot_nt(qb, k.astype(BF16)) * dmask
            inner = _dot(scores.astype(BF16), vb)
            cross = _dot(qb, s.astype(BF16)) * cross_decay
            kd_t = (k * key_decay).T.astype(BF16)
            s_ref[b, h] = s * chunk_decay + _dot(kd_t, vb)
            o = inner + cross
            y_ref[rows, cols] = (_group_norm_rows(o, GN_EPS) * _silu(g_ref[rows, cols])).astype(BF16)


def _ret_sample(proj, row0, lg, cos, sin, s0, nh, n_tok):
    bsz = s0.shape[0]
    hd = RET_HEAD_DIM
    dr = nh * hd
    bs = _tile(bsz, 4, 2)
    rows = bs * SAMPLE_SLOTS
    assert row0 % rows == 0
    t0 = row0 // rows
    sect = lambda s: pl.BlockSpec((rows, dr), lambda i, s=s: (t0 + i, s))
    tab = pl.BlockSpec((SAMPLE_SLOTS, hd // 2), lambda i: (0, 0))
    st = pl.BlockSpec((bs, nh, hd, hd), lambda i: (i, 0, 0, 0))
    return pl.pallas_call(
        functools.partial(_ret_sample_body, nh=nh, n_tok=n_tok),
        grid=(bsz // bs,),
        in_specs=[pl.BlockSpec(memory_space=pltpu.SMEM), sect(0), sect(1), sect(2), sect(3), tab, tab, st],
        out_specs=[pl.BlockSpec((rows, dr), lambda i: (i, 0)), st],
        out_shape=[jax.ShapeDtypeStruct((bsz * SAMPLE_SLOTS, dr), BF16),
                   jax.ShapeDtypeStruct(s0.shape, F32)],
        compiler_params=_params(("parallel",)),
        name="retention_sample",
    )(lg, proj, proj, proj, proj, cos, sin, s0)


def _out_core(yr_ref, yw_ref, w1_ref, w2_ref, x, gate, fw):
    y = _dot(yr_ref[...], w1_ref[...]) + _dot(yw_ref[...], w2_ref[...])
    xn = x + gate * y
    ms = jnp.mean(xn * xn, axis=-1, keepdims=True)
    return xn * lax.rsqrt(ms + NORM_EPS) * fw


def _out_prompt_body(yr_ref, yw_ref, w1_ref, w2_ref, x_ref, mod_ref, fw_ref, o_ref):
    o_ref[0] = _out_core(yr_ref, yw_ref, w1_ref, w2_ref, x_ref[0], mod_ref[0][2:3], fw_ref[...])


def _out_prompt(y_ret, y_rw, w, x, mod3, fw):
    b, l, d = x.shape
    dr = w.shape[0] // 2
    tl = _tile(l, 512, 16)
    nt = l // tl
    wspec = lambda half: pl.BlockSpec((dr, d), lambda bi, i: (half, 0), pipeline_mode=pl.Buffered(1))
    yspec = pl.BlockSpec((tl, dr), lambda bi, i: (bi * nt + i, 0))
    return pl.pallas_call(
        _out_prompt_body,
        grid=(b, nt),
        in_specs=[yspec, yspec, wspec(0), wspec(1),
                  pl.BlockSpec((1, tl, d), lambda bi, i: (bi, i, 0)),
                  pl.BlockSpec((1, 3, d), lambda bi, i: (bi, 0, 0)),
                  pl.BlockSpec((1, d), lambda bi, i: (0, 0))],
        out_specs=pl.BlockSpec((1, tl, d), lambda bi, i: (bi, i, 0)),
        out_shape=jax.ShapeDtypeStruct((b, l, d), F32),
        compiler_params=_params(("parallel", "parallel")),
        name="out_proj_prompt",
    )(y_ret, y_rw, w, w, x, mod3, fw.reshape(1, d))


def _out_sample_body(yr_ref, yw_ref, w1_ref, w2_ref, x_ref, mod_ref, fw_ref, o_ref):
    bs, sl, d = x_ref.shape
    gate = jnp.broadcast_to(mod_ref[...][:, 2:3, :], (bs, sl, d)).reshape(bs * sl, d)
    x = x_ref[...].reshape(bs * sl, d)
    o = _out_core(yr_ref, yw_ref, w1_ref, w2_ref, x, gate, fw_ref[...])
    o_ref[...] = o.reshape(bs, sl, d)


def _out_sample(y_ret, y_rw, w, x_slots, mod3, fw):
    bsz, sl, d = x_slots.shape
    dr = w.shape[0] // 2
    bs = _tile(bsz, 32, SUBLANES)
    wspec = lambda half: pl.BlockSpec((dr, d), lambda i: (half, 0), pipeline_mode=pl.Buffered(1))
    yspec = pl.BlockSpec((bs * sl, dr), lambda i: (i, 0))
    xspec = pl.BlockSpec((bs, sl, d), lambda i: (i, 0, 0))
    return pl.pallas_call(
        _out_sample_body,
        grid=(bsz // bs,),
        in_specs=[yspec, yspec, wspec(0), wspec(1), xspec,
                  pl.BlockSpec((bs, 3, d), lambda i: (i, 0, 0)),
                  pl.BlockSpec((1, d), lambda i: (0, 0))],
        out_specs=xspec,
        out_shape=jax.ShapeDtypeStruct((bsz, sl, d), F32),
        compiler_params=_params(("parallel",)),
        name="out_proj_sample",
    )(y_ret, y_rw, w, w, x_slots, mod3, fw.reshape(1, d))


def _rope_tables(pos, half):
    inv_freq = ROPE_THETA ** (-jnp.arange(half, dtype=F32) / half)
    ang = pos[:, None] * inv_freq[None, :]
    return jnp.cos(ang), jnp.sin(ang)


def kernel(x_prompt, x_sample, c_prompt, c_sample, state_ret, state_rwkv, state_shift, norm_w, w_ada,
           b_ada, w_in, mu_shift, w0_decay, w2_decay, a0, a2, k_k, k_a, r_k, ln_x_w, ln_x_b, w_out,
           final_norm_w):
    depth = w_in.shape[0]
    assert depth == 1, "single-layer trunk"
    bp, lp, d = x_prompt.shape
    bsz, ls, _ = x_sample.shape
    assert ls == 4, "sample path packs 4 tokens into slots 1..4"
    dr = d
    nh_ret = dr // RET_HEAD_DIM
    n_main = 8 * dr
    lora = w2_decay.shape[1]
    lora_pad = -(-2 * lora // LANES) * LANES

    w_t = jnp.swapaxes(w_in[0], 0, 1)
    w_t_lora = jnp.pad(w_t[n_main:], ((0, lora_pad - 2 * lora), (0, 0)))
    w_o = w_out[0].astype(BF16)
    mu = mu_shift[0]
    row = lambda p: p.reshape(1, -1)
    mu_r, mu_k, mu_v = row(mu[0:dr]), row(mu[dr:2 * dr]), row(mu[2 * dr:3 * dr])
    mu_l = jnp.pad(mu[3 * dr:], (0, lora_pad - 2 * lora)).reshape(1, lora_pad)
    w2p = jnp.pad(w2_decay[0], ((0, lora_pad - lora), (0, 0))).astype(BF16)
    a2p = jnp.pad(a2[0], ((lora, lora_pad - 2 * lora), (0, 0))).astype(BF16)
    lg = jnp.log1p(-jnp.exp2(-5.0 - jnp.arange(nh_ret, dtype=F32)))
    rw_params = (mu_r, mu_k, mu_v, mu_l, row(w0_decay[0]), w2p, row(a0[0]), a2p, row(k_k[0]), row(k_a[0]),
                 row(r_k[0]), row(ln_x_w[0]), row(ln_x_b[0]))

    n_c = bp + bsz
    n_c_pad = -(-n_c // SUBLANES) * SUBLANES
    c_all = jnp.pad(jnp.concatenate([c_prompt, c_sample], axis=0), ((0, n_c_pad - n_c), (0, 0)))
    mod3 = _adaln(c_all, w_ada[0], b_ada[0]).reshape(n_c_pad, 3, d)
    mod_p, mod_s = mod3[:bp], mod3[bp:bp + bsz]

    sl = SAMPLE_SLOTS
    row_s = bp * lp
    x_slots = jnp.pad(x_sample, ((0, 0), (1, sl - 1 - ls), (0, 0)))
    h_all, last_p, new_shift_s = _modulate(x_prompt, mod_p, x_slots, state_shift[0], mod_s, norm_w[0])
    new_shift_p = last_p[:, SUBLANES - 1, :]
    proj = _in_proj(h_all, w_t, n_main)
    lora_proj = _in_proj(h_all, w_t_lora, lora_pad, name="in_proj_lora")

    cos_p, sin_p = _rope_tables(jnp.arange(lp, dtype=F32), RET_HEAD_DIM // 2)
    y_ret_p, s_ret_p = _ret_prompt(proj, lg, cos_p, sin_p, bp, lp, nh_ret)
    y_rw_p, s_rw_p = _rwkv_prompt(proj, lora_proj, bp, lp, dr, *rw_params)
    y_prompt = _out_prompt(y_ret_p, y_rw_p, w_o, x_prompt, mod_p, final_norm_w)

    slot_pos = jnp.arange(sl, dtype=F32) - 1.0
    pos_s = jnp.where((slot_pos >= 0) & (slot_pos < ls), float(PAST_LEN) + slot_pos, 0.0)
    cos_s, sin_s = _rope_tables(pos_s, RET_HEAD_DIM // 2)
    y_ret_s, s_ret_s = _ret_sample(proj, row_s, lg, cos_s, sin_s, state_ret[0], nh_ret, ls)
    y_rw_s, s_rw_s = _rwkv_sample(proj, lora_proj, row_s, state_rwkv[0], ls, dr, *rw_params)
    y_slots = _out_sample(y_ret_s, y_rw_s, w_o, x_slots, mod_s, final_norm_w)
    y_sample = y_slots[:, 1:1 + ls, :]

    return (y_prompt, y_sample, s_ret_p[None], s_rw_p[None], new_shift_p[None],
            s_ret_s[None], s_rw_s[None], new_shift_s[None])
```

```python
import functools
import math

import jax
import jax.numpy as jnp
from jax import lax
from jax.experimental import pallas as pl
from jax.experimental.pallas import tpu as pltpu

F32 = jnp.float32
BF16 = jnp.bfloat16

RET_HEAD_DIM = 256
RWKV_HEAD_DIM = 64
RET_CHUNK = 128
RWKV_CHUNK = 64
RWKV_GROUP = 2
PAST_LEN = 16384
ROPE_THETA = 10000.0
NORM_EPS = 1e-6
GN_EPS = 1e-5
RWKV_GN_EPS = 64e-5
SAMPLE_SLOTS = 8
LANES = 128
SUBLANES = 8
VMEM_LIMIT_BYTES = 56 * 1024 * 1024

ADALN_COL_TILE = 768
MODULATE_ROW_TILE = 512
IN_PROJ_ROW_TILE = 1536
IN_PROJ_COL_TILE = 1024
OUT_PROJ_ROW_TILE = 512
OUT_SAMPLE_SEQ_TILE = 32
RET_SAMPLE_SEQ_TILE = 4
RWKV_SAMPLE_LANE_TILE = 1024


def _params(sem):
    return pltpu.CompilerParams(dimension_semantics=sem, vmem_limit_bytes=VMEM_LIMIT_BYTES)


def _tile(n, cap, align):
    if n <= cap:
        return n
    t = (cap // align) * align
    while t >= align:
        if n % t == 0:
            return t
        t -= align
    return n


def _silu(x):
    return x * jax.nn.sigmoid(x)


def _dot(a, b):
    return jnp.dot(a, b, preferred_element_type=F32)


def _dot_nt(a, b):
    return lax.dot_general(a, b, (((1,), (1,)), ((), ())), preferred_element_type=F32)


def _dot_tn(a, b):
    return lax.dot_general(a, b, (((0,), (0,)), ((), ())), preferred_element_type=F32)


def _adaln_body(c_ref, w_ref, b_ref, o_ref):
    s = _silu(c_ref[...]).astype(BF16)
    o_ref[...] = _dot(s, w_ref[...].astype(BF16)) + b_ref[...]


def _adaln(c, w_ada, b_ada):
    rows, d = c.shape
    n = w_ada.shape[1]
    tn = _tile(n, ADALN_COL_TILE, LANES)
    return pl.pallas_call(
        _adaln_body,
        grid=(n // tn,),
        in_specs=[pl.BlockSpec((rows, d), lambda j: (0, 0)),
                  pl.BlockSpec((d, tn), lambda j: (0, j)),
                  pl.BlockSpec((1, tn), lambda j: (0, j))],
        out_specs=pl.BlockSpec((rows, tn), lambda j: (0, j)),
        out_shape=jax.ShapeDtypeStruct((rows, n), F32),
        compiler_params=_params(("parallel",)),
        name="adaln",
    )(c, w_ada, b_ada.reshape(1, n))


def _modulated(x, nw, shift, scale):
    ms = jnp.mean(x * x, axis=-1, keepdims=True)
    return x * lax.rsqrt(ms + NORM_EPS) * (nw * (1.0 + scale)) + shift


def _modulate_body(xp_ref, modp_ref, xs_ref, prev_ref, mods_ref, nw_ref, h_ref, last_ref, new_ref, *, n_prompt):
    s = pl.program_id(0)

    @pl.when(s < n_prompt)
    def _():
        m = modp_ref[0]
        h = _modulated(xp_ref[0], nw_ref[...], m[0:1], m[1:2])
        h_ref[...] = h.astype(BF16)
        tl = h.shape[0]
        last_ref[0] = h[tl - SUBLANES:tl]

    @pl.when(s >= n_prompt)
    def _():
        x = xs_ref[...]
        m = mods_ref[...]
        h = _modulated(x, nw_ref[...], m[:, 0:1, :], m[:, 1:2, :])
        slot = lax.broadcasted_iota(jnp.int32, h.shape, 1)
        full = jnp.where(slot == 0, prev_ref[...][:, None, :], h)
        bs, sl, d = x.shape
        h_ref[...] = full.reshape(bs * sl, d).astype(BF16)
        new_ref[...] = h[:, 4, :]


def _modulate(x_prompt, mod_p, x_slots, prev, mod_s, nw):
    b, l, d = x_prompt.shape
    bsz, sl, _ = x_slots.shape
    tl = _tile(l, min(MODULATE_ROW_TILE, bsz * sl), 16)
    nt = l // tl
    bs = tl // sl
    assert bsz % bs == 0 and bs % SUBLANES == 0
    n_prompt = b * nt
    n_sample = bsz // bs
    samp = lambda s: jnp.maximum(s - n_prompt, 0)
    pb = lambda s: jnp.minimum(s // nt, b - 1)
    return pl.pallas_call(
        functools.partial(_modulate_body, n_prompt=n_prompt),
        grid=(n_prompt + n_sample,),
        in_specs=[pl.BlockSpec((1, tl, d), lambda s: (pb(s), jnp.where(s < n_prompt, s % nt, nt - 1), 0)),
                  pl.BlockSpec((1, 3, d), lambda s: (pb(s), 0, 0)),
                  pl.BlockSpec((bs, sl, d), lambda s: (samp(s), 0, 0)),
                  pl.BlockSpec((bs, d), lambda s: (samp(s), 0)),
                  pl.BlockSpec((bs, 3, d), lambda s: (samp(s), 0, 0)),
                  pl.BlockSpec((1, d), lambda s: (0, 0))],
        out_specs=[pl.BlockSpec((tl, d), lambda s: (s, 0)),
                   pl.BlockSpec((1, SUBLANES, d), lambda s: (pb(s), 0, 0)),
                   pl.BlockSpec((bs, d), lambda s: (samp(s), 0))],
        out_shape=[jax.ShapeDtypeStruct((b * l + bsz * sl, d), BF16),
                   jax.ShapeDtypeStruct((b, SUBLANES, d), F32),
                   jax.ShapeDtypeStruct((bsz, d), F32)],
        compiler_params=_params(("arbitrary",)),
        name="modulate",
    )(x_prompt, mod_p, x_slots, prev, mod_s, nw.reshape(1, d))


def _in_proj_body(x_ref, wt_ref, o_ref, w_scr):
    @pl.when(pl.program_id(1) == 0)
    def _():
        w_scr[...] = wt_ref[...].astype(BF16)

    o_ref[...] = _dot_nt(x_ref[...], w_scr[...])


def _in_proj(x, wt, n, name="in_proj"):
    m, k = x.shape
    tm = _tile(m, IN_PROJ_ROW_TILE, 16)
    tn = _tile(n, IN_PROJ_COL_TILE, LANES)
    return pl.pallas_call(
        _in_proj_body,
        grid=(n // tn, m // tm),
        in_specs=[pl.BlockSpec((tm, k), lambda j, i: (i, 0)),
                  pl.BlockSpec((tn, k), lambda j, i: (j, 0))],
        out_specs=pl.BlockSpec((tm, tn), lambda j, i: (i, j)),
        out_shape=jax.ShapeDtypeStruct((m, n), F32),
        scratch_shapes=[pltpu.VMEM((tn, k), BF16)],
        compiler_params=_params(("arbitrary", "arbitrary")),
        name=name,
    )(x, wt)


def _token_shift(cur_ref, carry_ref, mu_ref):
    cur = cur_ref[...]
    tl = cur.shape[0]
    prev = pltpu.roll(cur, 1, 0)
    if carry_ref is not None:
        row = lax.broadcasted_iota(jnp.int32, cur.shape, 0)
        prev = jnp.where(row == 0, carry_ref[0:1, :], prev)
        carry_ref[0:1, :] = cur[tl - 1:tl, :]
    return cur + (prev - cur) * mu_ref[...]


def _decay_and_rate(lo, w0_ref, w2_ref, a0_ref, a2_ref):
    dec = _dot(jnp.tanh(lo).astype(BF16), w2_ref[...])
    logw = -math.exp(-0.5) * jax.nn.sigmoid(w0_ref[...] + dec)
    a = jax.nn.sigmoid(a0_ref[...] + _dot(lo.astype(BF16), a2_ref[...]))
    return logw, a


def _zero_refs(*refs):
    for r in refs:
        r[...] = jnp.zeros_like(r)


def _split3(x):
    hi = x.astype(BF16)
    r1 = x - hi.astype(F32)
    mid = r1.astype(BF16)
    lo = (r1 - mid.astype(F32)).astype(BF16)
    return hi, mid, lo


def _dot_split3(m, parts):
    return _dot(m, parts[0]) + _dot(m, parts[1]) + _dot(m, parts[2])


def _rwkv_tile(pr_ref, pk_ref, pv_ref, pg_ref, pl_ref, mu_r, mu_k, mu_v, mu_l, w0_ref, w2_ref,
               a0_ref, a2_ref, kk_ref, ka_ref, rk_ref, lnw_ref, lnb_ref, y_ref,
               carries, seq, valid_slots, load_state, store_state):
    c = pr_ref.shape[0]
    hd = RWKV_HEAD_DIM
    gw = RWKV_GROUP * hd
    groups = range(pr_ref.shape[1] // gw)
    nseq = c // seq
    seqs = range(nseq)
    rows_of = [slice(q * seq, (q + 1) * seq) for q in seqs]
    c_r, c_k, c_v, c_l = carries

    r_all = _token_shift(pr_ref, c_r, mu_r)
    kw_all = _token_shift(pk_ref, c_k, mu_k)
    v_all = _token_shift(pv_ref, c_v, mu_v)
    lo = _token_shift(pl_ref, c_l, mu_l)
    logw_all, a_all = _decay_and_rate(lo, w0_ref, w2_ref, a0_ref, a2_ref)
    k_all = kw_all * (1.0 + (a_all - 1.0) * ka_ref[...])
    kkr_all = kw_all * kk_ref[...]
    if valid_slots is not None:
        slot = lax.broadcasted_iota(jnp.int32, (c, 1), 0) % seq
        valid = (slot >= valid_slots[0]) & (slot <= valid_slots[1])
        logw_all = jnp.where(valid, logw_all, 0.0)
        kkr_all = jnp.where(valid, kkr_all, 0.0)
        k_all = jnp.where(valid, k_all, 0.0)

    ti = lax.broadcasted_iota(jnp.int32, (c, c), 0)
    tj = lax.broadcasted_iota(jnp.int32, (c, c), 1)
    same_seq = (ti // seq) == (tj // seq)
    logw_parts = _split3(logw_all)
    cum_all = _dot_split3(((ti >= tj) & same_seq).astype(BF16), logw_parts)
    if nseq > 1:
        cum_end_all = _dot_split3(same_seq.astype(BF16), logw_parts)

    hr = lax.broadcasted_iota(jnp.int32, (gw, gw), 0) // hd
    hc = lax.broadcasted_iota(jnp.int32, (gw, gw), 1) // hd
    headmask = hr == hc
    tok = lax.broadcasted_iota(jnp.int32, (c, gw), 0)
    src = lax.broadcasted_iota(jnp.int32, (c, gw), 1) % hd
    same = (tok // seq) == (src // seq)
    strict = (tok > src) & same
    incl = (tok >= src) & same

    def bd(x):
        t = jnp.concatenate([x] * RWKV_GROUP, axis=0)
        return jnp.where(headmask, t, 0.0).astype(BF16)

    lane_head = lax.broadcasted_iota(jnp.int32, (c, gw), 1) // hd

    def head_sum(x):
        out = jnp.zeros_like(x)
        for j in range(RWKV_GROUP):
            mine = lane_head == j
            out = jnp.where(mine, jnp.sum(jnp.where(mine, x, 0.0), axis=-1, keepdims=True), out)
        return out

    def stages(gids):
        groups = range(len(gids))
        sls = [slice(g * gw, (g + 1) * gw) for g in gids]
        r = [r_all[:, sl] for sl in sls]
        k = [k_all[:, sl] for sl in sls]
        v = [v_all[:, sl] for sl in sls]
        cum = [cum_all[:, sl] for sl in sls]
        if nseq > 1:
            cum_end = [cum_end_all[:, sl] for sl in sls]
        else:
            cum_end = [cm[c - 1:c, :] for cm in cum]

        kk_sq = [head_sum(jnp.square(kkr_all[:, sl])) for sl in sls]
        rk_sum = [head_sum(r[g] * k[g] * rk_ref[:, sls[g]]) for g in groups]
        kkn = [kkr_all[:, sls[g]] * lax.rsqrt(jnp.maximum(kk_sq[g], 1e-24)) for g in groups]
        b = [kkn[g] * a_all[:, sls[g]] for g in groups]
        e_inv = [jnp.exp(-cum[g]) for g in groups]
        at = [-kkn[g] * jnp.exp(cum[g] - logw_all[:, sls[g]]) for g in groups]
        rt = [r[g] * jnp.exp(cum[g]) for g in groups]
        ar = [jnp.concatenate([at[g], rt[g]], axis=0).astype(BF16) for g in groups]
        states = [load_state(g) for g in gids]

        def from_state(g):
            if nseq == 1:
                fs = _dot_nt(ar[g], states[g][0].astype(BF16))
                return fs[:c], fs[c:]
            fa, fr = [], []
            for q in seqs:
                arq = jnp.concatenate([at[g][rows_of[q]], rt[g][rows_of[q]]], axis=0).astype(BF16)
                fs = _dot_nt(arq, states[g][q].astype(BF16))
                fa.append(fs[:seq])
                fr.append(fs[seq:])
            return jnp.concatenate(fa, axis=0), jnp.concatenate(fr, axis=0)

        fstate = [from_state(g) for g in groups]
        s_b = [_dot_nt(ar[g], bd(b[g] * e_inv[g])) for g in groups]
        s_k = [_dot_nt(ar[g], bd(k[g] * e_inv[g])) for g in groups]
        a_ab = [jnp.where(strict, x[:c], 0.0) for x in s_b]
        a_ak = [jnp.where(strict, x[:c], 0.0).astype(BF16) for x in s_k]
        a_r = [jnp.concatenate([jnp.where(incl, s_b[g][c:], 0.0), jnp.where(incl, s_k[g][c:], 0.0)],
                               axis=1).astype(BF16) for g in groups]

        steps = max(1, int(math.ceil(math.log2(seq))))
        pw = a_ab
        pm = a_ab
        pw_next = [_dot(pw[g].astype(BF16), bd(pw[g])) for g in groups] if steps > 1 else None
        for s in range(1, steps):
            pw = pw_next
            if s < steps - 1:
                res = [_dot(jnp.concatenate([pw[g], pm[g]], axis=0).astype(BF16), bd(pw[g])) for g in groups]
                pw_next = [x[:c] for x in res]
                pm = [pm[g] + pw[g] + res[g][c:] for g in groups]
            else:
                pm = [pm[g] + pw[g] + _dot(pm[g].astype(BF16), bd(pw[g])) for g in groups]

        bd_v = [bd(x) for x in v]
        w_rhs = [fstate[g][0] + _dot(a_ak[g], bd_v[g]) for g in groups]
        u = [w_rhs[g] + _dot(pm[g].astype(BF16), bd(w_rhs[g])) for g in groups]
        o = [fstate[g][1] + _dot(a_r[g], jnp.concatenate([bd(u[g]), bd_v[g]], axis=0)) for g in groups]
        for g in groups:
            e_end = jnp.exp(cum_end[g] - cum[g])
            b_end = b[g] * e_end
            k_end = k[g] * e_end
            for q in seqs:
                rq = rows_of[q]
                uv = jnp.concatenate([u[g][rq], v[g][rq]], axis=0).astype(BF16)
                bk = jnp.concatenate([b_end[rq], k_end[rq]], axis=0).astype(BF16)
                keep = jnp.exp(cum_end[g][q * seq:q * seq + 1, :]) if nseq > 1 else jnp.exp(cum_end[g])
                store_state(gids[g], q, states[g][q] * keep + _dot_tn(uv, bk))

        mean = [head_sum(x) * (1.0 / hd) for x in o]
        oc = [o[g] - mean[g] for g in groups]
        var = [head_sum(jnp.square(x)) * (1.0 / hd) for x in oc]
        for g in groups:
            sl = sls[g]
            og = oc[g] * lax.rsqrt(var[g] + RWKV_GN_EPS) * lnw_ref[:, sl] + lnb_ref[:, sl]
            y_ref[:, sl] = ((og + rk_sum[g] * v[g]) * _silu(pg_ref[:, sl])).astype(BF16)

    stages(list(groups))


def _head_mask(gw, hd):
    hr = lax.broadcasted_iota(jnp.int32, (gw, gw), 0) // hd
    hc = lax.broadcasted_iota(jnp.int32, (gw, gw), 1) // hd
    return hr == hc


def _rwkv_prompt_body(*refs):
    ins, (y_ref, so_ref), (c_r, c_k, c_v, c_l, s_scr) = refs[:18], refs[18:20], refs[20:]
    n = pl.program_id(1)
    hd = RWKV_HEAD_DIM
    gw = RWKV_GROUP * hd
    headmask = _head_mask(gw, hd)

    @pl.when(n == 0)
    def _():
        _zero_refs(c_r, c_k, c_v, c_l, s_scr)

    def load_state(g):
        return [s_scr[g]]

    def store_state(g, q, s):
        s_scr[g] = jnp.where(headmask, s, 0.0)

    _rwkv_tile(*ins, y_ref, (c_r, c_k, c_v, c_l), ins[0].shape[0], None, load_state, store_state)

    @pl.when(n == pl.num_programs(1) - 1)
    def _():
        for g in range(s_scr.shape[0]):
            sg = s_scr[g]
            for j in range(RWKV_GROUP):
                so_ref[0, g * RWKV_GROUP + j] = sg[j * hd:(j + 1) * hd, j * hd:(j + 1) * hd]


def _rwkv_prompt(proj, lora, nb, l, dr, mu_r, mu_k, mu_v, mu_l, w0, w2p, a0, a2p, k_k, k_a, r_k, ln_w, ln_b):
    c = RWKV_CHUNK
    assert l % c == 0 and c == RWKV_HEAD_DIM
    nchunk = l // c
    lp = lora.shape[1]
    nh = dr // RWKV_HEAD_DIM
    gw = RWKV_GROUP * RWKV_HEAD_DIM
    vec = lambda: pl.BlockSpec((1, dr), lambda b, n: (0, 0))
    sect = lambda s: pl.BlockSpec((c, dr), lambda b, n, s=s: (b * nchunk + n, 4 + s))
    return pl.pallas_call(
        _rwkv_prompt_body,
        grid=(nb, nchunk),
        in_specs=[sect(0), sect(1), sect(2), sect(3),
                  pl.BlockSpec((c, lp), lambda b, n: (b * nchunk + n, 0)),
                  vec(), vec(), vec(),
                  pl.BlockSpec((1, lp), lambda b, n: (0, 0)),
                  vec(),
                  pl.BlockSpec((lp, dr), lambda b, n: (0, 0)),
                  vec(),
                  pl.BlockSpec((lp, dr), lambda b, n: (0, 0)),
                  vec(), vec(), vec(), vec(), vec()],
        out_specs=[pl.BlockSpec((c, dr), lambda b, n: (b * nchunk + n, 0)),
                   pl.BlockSpec((1, nh, RWKV_HEAD_DIM, RWKV_HEAD_DIM), lambda b, n: (b, 0, 0, 0))],
        out_shape=[jax.ShapeDtypeStruct((nb * l, dr), BF16),
                   jax.ShapeDtypeStruct((nb, nh, RWKV_HEAD_DIM, RWKV_HEAD_DIM), F32)],
        scratch_shapes=[pltpu.VMEM((SUBLANES, dr), F32), pltpu.VMEM((SUBLANES, dr), F32),
                        pltpu.VMEM((SUBLANES, dr), F32), pltpu.VMEM((SUBLANES, lp), F32),
                        pltpu.VMEM((dr // gw, gw, gw), F32)],
        compiler_params=_params(("parallel", "arbitrary")),
        name="rwkv_chunked",
    )(proj, proj, proj, proj, lora, mu_r, mu_k, mu_v, mu_l, w0, w2p, a0, a2p, k_k, k_a, r_k, ln_w, ln_b)


def _rwkv_sample_body(*refs, n_tok):
    ins, s0_ref, y_ref, so_ref = refs[:18], refs[18], refs[19], refs[20]
    hd = RWKV_HEAD_DIM
    zero = jnp.zeros((hd, hd), F32)

    def load_state(g):
        out = []
        per_head = [pltpu.einshape("vbk->bvk", s0_ref[g * RWKV_GROUP + j]) for j in range(RWKV_GROUP)]
        for q in range(s0_ref.shape[2]):
            rows = [jnp.concatenate([per_head[j][q] if i == j else zero
                                     for i in range(RWKV_GROUP)], axis=1) for j in range(RWKV_GROUP)]
            out.append(jnp.concatenate(rows, axis=0))
        return out

    def store_state(g, q, s):
        for j in range(RWKV_GROUP):
            so_ref[q, g * RWKV_GROUP + j] = s[j * hd:(j + 1) * hd, j * hd:(j + 1) * hd]

    _rwkv_tile(*ins, y_ref, (None, None, None, None), SAMPLE_SLOTS, (1, n_tok), load_state, store_state)


def _rwkv_sample(proj, lora, row0, s0, n_tok, dr, mu_r, mu_k, mu_v, mu_l, w0, w2p, a0, a2p, k_k, k_a,
                 r_k, ln_w, ln_b):
    c = RWKV_CHUNK
    nh_all, _, bsz, _ = s0.shape
    rows = bsz * SAMPLE_SLOTS
    assert rows % c == 0 and row0 % c == 0 and c == RWKV_HEAD_DIM
    t0 = row0 // c
    nseq = c // SAMPLE_SLOTS
    lp = lora.shape[1]
    hd = RWKV_HEAD_DIM
    dw = _tile(dr, RWKV_SAMPLE_LANE_TILE, RWKV_GROUP * hd)
    nw = dr // dw
    vec = lambda: pl.BlockSpec((1, dw), lambda i, hf: (0, hf))
    sect = lambda s: pl.BlockSpec((c, dw), lambda i, hf, s=s: (t0 + i, (4 + s) * nw + hf))
    lora_w = lambda: pl.BlockSpec((lp, dw), lambda i, hf: (0, hf))
    st_in = pl.BlockSpec((dw // hd, hd, nseq, hd), lambda i, hf: (hf, 0, i, 0))
    st_out = pl.BlockSpec((nseq, dw // hd, hd, hd), lambda i, hf: (i, hf, 0, 0))
    return pl.pallas_call(
        functools.partial(_rwkv_sample_body, n_tok=n_tok),
        grid=(rows // c, nw),
        in_specs=[sect(0), sect(1), sect(2), sect(3),
                  pl.BlockSpec((c, lp), lambda i, hf: (t0 + i, 0)),
                  vec(), vec(), vec(),
                  pl.BlockSpec((1, lp), lambda i, hf: (0, 0)),
                  vec(), lora_w(), vec(), lora_w(),
                  vec(), vec(), vec(), vec(), vec(), st_in],
        out_specs=[pl.BlockSpec((c, dw), lambda i, hf: (i, hf)), st_out],
        out_shape=[jax.ShapeDtypeStruct((rows, dr), BF16),
                   jax.ShapeDtypeStruct((bsz, nh_all, hd, hd), F32)],
        compiler_params=_params(("parallel", "parallel")),
        name="rwkv_chunked_sample",
    )(proj, proj, proj, proj, lora, mu_r, mu_k, mu_v, mu_l, w0, w2p, a0, a2p, k_k, k_a, r_k, ln_w, ln_b, s0)


def _rotary(x, cos, sin):
    half = x.shape[-1] // 2
    x1 = x[:, :half]
    x2 = x[:, half:]
    return jnp.concatenate([x1 * cos - x2 * sin, x1 * sin + x2 * cos], axis=-1)


def _group_norm_rows(o, eps):
    mean = jnp.mean(o, axis=-1, keepdims=True)
    oc = o - mean
    var = jnp.mean(oc * oc, axis=-1, keepdims=True)
    return oc * lax.rsqrt(var + eps)


def _ret_prompt_body(lg_ref, q_ref, k_ref, v_ref, g_ref, cos_ref, sin_ref, y_ref, s_ref):
    n = pl.program_id(1)
    c = q_ref.shape[0]
    hd = RET_HEAD_DIM
    heads = range(q_ref.shape[1] // hd)
    cols = [slice(h * hd, (h + 1) * hd) for h in heads]

    @pl.when(n == 0)
    def _():
        s_ref[...] = jnp.zeros_like(s_ref)

    cos = cos_ref[...]
    sin = sin_ref[...]
    ii = lax.broadcasted_iota(jnp.int32, (c, c), 0)
    jj = lax.broadcasted_iota(jnp.int32, (c, c), 1)
    diff = (ii - jj).astype(F32)
    causal = diff >= 0.0
    dist = jnp.maximum(diff, 0.0)
    ic = lax.broadcasted_iota(jnp.int32, (c, 1), 0).astype(F32)

    lg = [jnp.full((1, 1), lg_ref[h], F32) for h in heads]
    q = [_rotary(q_ref[:, cs], cos, sin).astype(BF16) for cs in cols]
    k = [_rotary(k_ref[:, cs], cos, sin) * (hd ** -0.5) for cs in cols]
    vb = [v_ref[:, cs].astype(BF16) for cs in cols]
    s = [s_ref[0, h] for h in heads]
    scores = [_dot_nt(q[h], k[h].astype(BF16)) * jnp.where(causal, jnp.exp(dist * lg[h]), 0.0) for h in heads]
    cross = [_dot(q[h], s[h].astype(BF16)) * jnp.exp((ic + 1.0) * lg[h]) for h in heads]
    o = [_dot(scores[h].astype(BF16), vb[h]) + cross[h] for h in heads]
    for h in heads:
        kd = (k[h] * jnp.exp((c - 1.0 - ic) * lg[h])).astype(BF16)
        s_ref[0, h] = s[h] * jnp.exp(float(c) * lg[h]) + _dot_tn(kd, vb[h])
    for h in heads:
        y_ref[:, cols[h]] = (_group_norm_rows(o[h], GN_EPS) * _silu(g_ref[:, cols[h]])).astype(BF16)


def _ret_prompt(proj, lg, cos, sin, nb, l, nh):
    hd = RET_HEAD_DIM
    dr = nh * hd
    c = math.gcd(l, RET_CHUNK)
    nchunk = l // c
    sect = lambda s: pl.BlockSpec((c, dr), lambda b, n, s=s: (b * nchunk + n, s))
    tab = pl.BlockSpec((c, hd // 2), lambda b, n: (n, 0))
    return pl.pallas_call(
        _ret_prompt_body,
        grid=(nb, nchunk),
        in_specs=[pl.BlockSpec(memory_space=pltpu.SMEM), sect(0), sect(1), sect(2), sect(3), tab, tab],
        out_specs=[pl.BlockSpec((c, dr), lambda b, n: (b * nchunk + n, 0)),
                   pl.BlockSpec((1, nh, hd, hd), lambda b, n: (b, 0, 0, 0))],
        out_shape=[jax.ShapeDtypeStruct((nb * l, dr), BF16),
                   jax.ShapeDtypeStruct((nb, nh, hd, hd), F32)],
        compiler_params=_params(("parallel", "arbitrary")),
        name="retention_prompt",
    )(lg, proj, proj, proj, proj, cos, sin)


def _ret_sample_body(lg_ref, q_ref, k_ref, v_ref, g_ref, cos_ref, sin_ref, s0_ref, y_ref, s_ref, *, nh, n_tok):
    hd = RET_HEAD_DIM
    bs = s0_ref.shape[0]
    sl = SAMPLE_SLOTS
    cos = cos_ref[...]
    sin = sin_ref[...]
    slot_r = lax.broadcasted_iota(jnp.int32, (sl, 1), 0)
    tok_r = (slot_r - 1).astype(F32)
    valid_r = (slot_r >= 1) & (slot_r <= n_tok)
    ii = lax.broadcasted_iota(jnp.int32, (sl, sl), 0)
    jj = lax.broadcasted_iota(jnp.int32, (sl, sl), 1)
    diff = (ii - jj).astype(F32)
    pair_ok = (diff >= 0.0) & (jj >= 1) & (jj <= n_tok)
    for h in range(nh):
        lg = jnp.full((1, 1), lg_ref[h], F32)
        dmask = jnp.where(pair_ok, jnp.exp(jnp.maximum(diff, 0.0) * lg), 0.0)
        cross_decay = jnp.exp((tok_r + 1.0) * lg)
        key_decay = jnp.where(valid_r, jnp.exp((n_tok - 1.0 - tok_r) * lg), 0.0)
        chunk_decay = jnp.exp(float(n_tok) * lg)
        for b in range(bs):
            rows = slice(b * sl, (b + 1) * sl)
            cols = slice(h * hd, (h + 1) * hd)
            q = _rotary(q_ref[rows, cols], cos, sin)
            k = _rotary(k_ref[rows, cols], cos, sin) * (hd ** -0.5)
            vb = v_ref[rows, cols].astype(BF16)
            s = s0_ref[b, h]
            qb = q.astype(BF16)
            scores = _dot_nt(qb, k.astype(BF16)) * dmask
            inner = _dot(scores.astype(BF16), vb)
            cross = _dot(qb, s.astype(BF16)) * cross_decay
            kd_t = (k * key_decay).T.astype(BF16)
            s_ref[b, h] = s * chunk_decay + _dot(kd_t, vb)
            o = inner + cross
            y_ref[rows, cols] = (_group_norm_rows(o, GN_EPS) * _silu(g_ref[rows, cols])).astype(BF16)


def _ret_sample(proj, row0, lg, cos, sin, s0, nh, n_tok):
    bsz = s0.shape[0]
    hd = RET_HEAD_DIM
    dr = nh * hd
    bs = _tile(bsz, RET_SAMPLE_SEQ_TILE, 2)
    rows = bs * SAMPLE_SLOTS
    assert row0 % rows == 0
    t0 = row0 // rows
    sect = lambda s: pl.BlockSpec((rows, dr), lambda i, s=s: (t0 + i, s))
    tab = pl.BlockSpec((SAMPLE_SLOTS, hd // 2), lambda i: (0, 0))
    st = pl.BlockSpec((bs, nh, hd, hd), lambda i: (i, 0, 0, 0))
    return pl.pallas_call(
        functools.partial(_ret_sample_body, nh=nh, n_tok=n_tok),
        grid=(bsz // bs,),
        in_specs=[pl.BlockSpec(memory_space=pltpu.SMEM), sect(0), sect(1), sect(2), sect(3), tab, tab, st],
        out_specs=[pl.BlockSpec((rows, dr), lambda i: (i, 0)), st],
        out_shape=[jax.ShapeDtypeStruct((bsz * SAMPLE_SLOTS, dr), BF16),
                   jax.ShapeDtypeStruct(s0.shape, F32)],
        compiler_params=_params(("parallel",)),
        name="retention_sample",
    )(lg, proj, proj, proj, proj, cos, sin, s0)


def _out_core(yr_ref, yw_ref, w1_ref, w2_ref, x, gate, fw):
    y = _dot(yr_ref[...], w1_ref[...]) + _dot(yw_ref[...], w2_ref[...])
    xn = x + gate * y
    ms = jnp.mean(xn * xn, axis=-1, keepdims=True)
    return xn * lax.rsqrt(ms + NORM_EPS) * fw


def _out_prompt_body(yr_ref, yw_ref, w1_ref, w2_ref, x_ref, mod_ref, fw_ref, o_ref):
    o_ref[0] = _out_core(yr_ref, yw_ref, w1_ref, w2_ref, x_ref[0], mod_ref[0][2:3], fw_ref[...])


def _out_prompt(y_ret, y_rw, w, x, mod3, fw):
    b, l, d = x.shape
    dr = w.shape[0] // 2
    tl = _tile(l, OUT_PROJ_ROW_TILE, 16)
    nt = l // tl
    wspec = lambda half: pl.BlockSpec((dr, d), lambda bi, i: (half, 0), pipeline_mode=pl.Buffered(1))
    yspec = pl.BlockSpec((tl, dr), lambda bi, i: (bi * nt + i, 0))
    return pl.pallas_call(
        _out_prompt_body,
        grid=(b, nt),
        in_specs=[yspec, yspec, wspec(0), wspec(1),
                  pl.BlockSpec((1, tl, d), lambda bi, i: (bi, i, 0)),
                  pl.BlockSpec((1, 3, d), lambda bi, i: (bi, 0, 0)),
                  pl.BlockSpec((1, d), lambda bi, i: (0, 0))],
        out_specs=pl.BlockSpec((1, tl, d), lambda bi, i: (bi, i, 0)),
        out_shape=jax.ShapeDtypeStruct((b, l, d), F32),
        compiler_params=_params(("parallel", "parallel")),
        name="out_proj_prompt",
    )(y_ret, y_rw, w, w, x, mod3, fw.reshape(1, d))


def _out_sample_body(yr_ref, yw_ref, w1_ref, w2_ref, x_ref, mod_ref, fw_ref, o_ref):
    bs, sl, d = x_ref.shape
    gate = jnp.broadcast_to(mod_ref[...][:, 2:3, :], (bs, sl, d)).reshape(bs * sl, d)
    x = x_ref[...].reshape(bs * sl, d)
    o = _out_core(yr_ref, yw_ref, w1_ref, w2_ref, x, gate, fw_ref[...])
    o_ref[...] = o.reshape(bs, sl, d)


def _out_sample(y_ret, y_rw, w, x_slots, mod3, fw):
    bsz, sl, d = x_slots.shape
    dr = w.shape[0] // 2
    bs = _tile(bsz, OUT_SAMPLE_SEQ_TILE, SUBLANES)
    wspec = lambda half: pl.BlockSpec((dr, d), lambda i: (half, 0), pipeline_mode=pl.Buffered(1))
    yspec = pl.BlockSpec((bs * sl, dr), lambda i: (i, 0))
    xspec = pl.BlockSpec((bs, sl, d), lambda i: (i, 0, 0))
    return pl.pallas_call(
        _out_sample_body,
        grid=(bsz // bs,),
        in_specs=[yspec, yspec, wspec(0), wspec(1), xspec,
                  pl.BlockSpec((bs, 3, d), lambda i: (i, 0, 0)),
                  pl.BlockSpec((1, d), lambda i: (0, 0))],
        out_specs=xspec,
        out_shape=jax.ShapeDtypeStruct((bsz, sl, d), F32),
        compiler_params=_params(("parallel",)),
        name="out_proj_sample",
    )(y_ret, y_rw, w, w, x_slots, mod3, fw.reshape(1, d))


def _rope_tables(pos, half):
    inv_freq = ROPE_THETA ** (-jnp.arange(half, dtype=F32) / half)
    ang = pos[:, None] * inv_freq[None, :]
    return jnp.cos(ang), jnp.sin(ang)


def kernel(x_prompt, x_sample, c_prompt, c_sample, state_ret, state_rwkv, state_shift, norm_w, w_ada,
           b_ada, w_in, mu_shift, w0_decay, w2_decay, a0, a2, k_k, k_a, r_k, ln_x_w, ln_x_b, w_out,
           final_norm_w):
    depth = w_in.shape[0]
    assert depth == 1, "single-layer trunk"
    bp, lp, d = x_prompt.shape
    bsz, ls, _ = x_sample.shape
    assert ls == 4, "sample path packs 4 tokens into slots 1..4"
    dr = d
    nh_ret = dr // RET_HEAD_DIM
    n_main = 8 * dr
    lora = w2_decay.shape[1]
    lora_pad = -(-2 * lora // LANES) * LANES

    w_t = jnp.swapaxes(w_in[0], 0, 1)
    w_t_lora = jnp.pad(w_t[n_main:], ((0, lora_pad - 2 * lora), (0, 0)))
    w_o = w_out[0].astype(BF16)
    mu = mu_shift[0]
    row = lambda p: p.reshape(1, -1)
    mu_r, mu_k, mu_v = row(mu[0:dr]), row(mu[dr:2 * dr]), row(mu[2 * dr:3 * dr])
    mu_l = jnp.pad(mu[3 * dr:], (0, lora_pad - 2 * lora)).reshape(1, lora_pad)
    w2p = jnp.pad(w2_decay[0], ((0, lora_pad - lora), (0, 0))).astype(BF16)
    a2p = jnp.pad(a2[0], ((lora, lora_pad - 2 * lora), (0, 0))).astype(BF16)
    lg = jnp.log1p(-jnp.exp2(-5.0 - jnp.arange(nh_ret, dtype=F32)))
    rw_params = (mu_r, mu_k, mu_v, mu_l, row(w0_decay[0]), w2p, row(a0[0]), a2p, row(k_k[0]), row(k_a[0]),
                 row(r_k[0]), row(ln_x_w[0]), row(ln_x_b[0]))

    n_c = bp + bsz
    n_c_pad = -(-n_c // SUBLANES) * SUBLANES
    c_all = jnp.pad(jnp.concatenate([c_prompt, c_sample], axis=0), ((0, n_c_pad - n_c), (0, 0)))
    mod3 = _adaln(c_all, w_ada[0], b_ada[0]).reshape(n_c_pad, 3, d)
    mod_p, mod_s = mod3[:bp], mod3[bp:bp + bsz]

    sl = SAMPLE_SLOTS
    row_s = bp * lp
    x_slots = jnp.pad(x_sample, ((0, 0), (1, sl - 1 - ls), (0, 0)))
    h_all, last_p, new_shift_s = _modulate(x_prompt, mod_p, x_slots, state_shift[0], mod_s, norm_w[0])
    new_shift_p = last_p[:, SUBLANES - 1, :]
    proj = _in_proj(h_all, w_t, n_main)
    lora_proj = _in_proj(h_all, w_t_lora, lora_pad, name="in_proj_lora")

    cos_p, sin_p = _rope_tables(jnp.arange(lp, dtype=F32), RET_HEAD_DIM // 2)
    y_ret_p, s_ret_p = _ret_prompt(proj, lg, cos_p, sin_p, bp, lp, nh_ret)
    y_rw_p, s_rw_p = _rwkv_prompt(proj, lora_proj, bp, lp, dr, *rw_params)
    y_prompt = _out_prompt(y_ret_p, y_rw_p, w_o, x_prompt, mod_p, final_norm_w)

    slot_pos = jnp.arange(sl, dtype=F32) - 1.0
    pos_s = jnp.where((slot_pos >= 0) & (slot_pos < ls), float(PAST_LEN) + slot_pos, 0.0)
    cos_s, sin_s = _rope_tables(pos_s, RET_HEAD_DIM // 2)
    y_ret_s, s_ret_s = _ret_sample(proj, row_s, lg, cos_s, sin_s, state_ret[0], nh_ret, ls)
    s0_rw = jnp.transpose(state_rwkv[0], (1, 2, 0, 3))
    y_rw_s, s_rw_s = _rwkv_sample(proj, lora_proj, row_s, s0_rw, ls, dr, *rw_params)
    y_slots = _out_sample(y_ret_s, y_rw_s, w_o, x_slots, mod_s, final_norm_w)
    y_sample = y_slots[:, 1:1 + ls, :]

    return (y_prompt, y_sample, s_ret_p[None], s_rw_p[None], new_shift_p[None],
            s_ret_s[None], s_rw_s[None], new_shift_s[None])
```

```python
import functools
import math

import jax
import jax.numpy as jnp
from jax import lax
from jax.experimental import pallas as pl
from jax.experimental.pallas import tpu as pltpu

F32 = jnp.float32
BF16 = jnp.bfloat16

RET_HEAD_DIM = 256
RWKV_HEAD_DIM = 64
RET_CHUNK = 128
RWKV_CHUNK = 64
RWKV_GROUP = 2
PAST_LEN = 16384
ROPE_THETA = 10000.0
NORM_EPS = 1e-6
GN_EPS = 1e-5
RWKV_GN_EPS = 64e-5
SAMPLE_SLOTS = 8
LANES = 128
SUBLANES = 8
VMEM_LIMIT_BYTES = 56 * 1024 * 1024

ADALN_COL_TILE = 768
MODULATE_ROW_TILE = 512
IN_PROJ_ROW_TILE = 1536
IN_PROJ_COL_TILE = 1024
OUT_PROJ_ROW_TILE = 512
OUT_SAMPLE_SEQ_TILE = 32
RET_SAMPLE_SEQ_TILE = 4
RWKV_SAMPLE_LANE_TILE = 1024


def _params(sem):
    return pltpu.CompilerParams(dimension_semantics=sem, vmem_limit_bytes=VMEM_LIMIT_BYTES)


def _tile(n, cap, align):
    if n <= cap:
        return n
    t = (cap // align) * align
    while t >= align:
        if n % t == 0:
            return t
        t -= align
    return n


def _silu(x):
    return x * jax.nn.sigmoid(x)


def _dot(a, b):
    return jnp.dot(a, b, preferred_element_type=F32)


def _dot_nt(a, b):
    return lax.dot_general(a, b, (((1,), (1,)), ((), ())), preferred_element_type=F32)


def _dot_tn(a, b):
    return lax.dot_general(a, b, (((0,), (0,)), ((), ())), preferred_element_type=F32)


def _adaln_body(c_ref, w_ref, b_ref, o_ref):
    s = _silu(c_ref[...]).astype(BF16)
    o_ref[...] = _dot(s, w_ref[...].astype(BF16)) + b_ref[...]


def _adaln(c, w_ada, b_ada):
    rows, d = c.shape
    n = w_ada.shape[1]
    tn = _tile(n, ADALN_COL_TILE, LANES)
    return pl.pallas_call(
        _adaln_body,
        grid=(n // tn,),
        in_specs=[pl.BlockSpec((rows, d), lambda j: (0, 0)),
                  pl.BlockSpec((d, tn), lambda j: (0, j)),
                  pl.BlockSpec((1, tn), lambda j: (0, j))],
        out_specs=pl.BlockSpec((rows, tn), lambda j: (0, j)),
        out_shape=jax.ShapeDtypeStruct((rows, n), F32),
        compiler_params=_params(("parallel",)),
        name="adaln",
    )(c, w_ada, b_ada.reshape(1, n))


def _modulated(x, nw, shift, scale):
    ms = jnp.mean(x * x, axis=-1, keepdims=True)
    return x * lax.rsqrt(ms + NORM_EPS) * (nw * (1.0 + scale)) + shift


def _modulate_body(xp_ref, modp_ref, xs_ref, prev_ref, mods_ref, nw_ref, h_ref, last_ref, new_ref, *, n_prompt):
    s = pl.program_id(0)

    @pl.when(s < n_prompt)
    def _():
        m = modp_ref[0]
        h = _modulated(xp_ref[0], nw_ref[...], m[0:1], m[1:2])
        h_ref[...] = h.astype(BF16)
        tl = h.shape[0]
        last_ref[0] = h[tl - SUBLANES:tl]

    @pl.when(s >= n_prompt)
    def _():
        x = xs_ref[...]
        m = mods_ref[...]
        h = _modulated(x, nw_ref[...], m[:, 0:1, :], m[:, 1:2, :])
        slot = lax.broadcasted_iota(jnp.int32, h.shape, 1)
        full = jnp.where(slot == 0, prev_ref[...][:, None, :], h)
        bs, sl, d = x.shape
        h_ref[...] = full.reshape(bs * sl, d).astype(BF16)
        new_ref[...] = h[:, 4, :]


def _modulate(x_prompt, mod_p, x_slots, prev, mod_s, nw):
    b, l, d = x_prompt.shape
    bsz, sl, _ = x_slots.shape
    tl = _tile(l, min(MODULATE_ROW_TILE, bsz * sl), 16)
    nt = l // tl
    bs = tl // sl
    assert bsz % bs == 0 and bs % SUBLANES == 0
    n_prompt = b * nt
    n_sample = bsz // bs
    samp = lambda s: jnp.maximum(s - n_prompt, 0)
    pb = lambda s: jnp.minimum(s // nt, b - 1)
    return pl.pallas_call(
        functools.partial(_modulate_body, n_prompt=n_prompt),
        grid=(n_prompt + n_sample,),
        in_specs=[pl.BlockSpec((1, tl, d), lambda s: (pb(s), jnp.where(s < n_prompt, s % nt, nt - 1), 0)),
                  pl.BlockSpec((1, 3, d), lambda s: (pb(s), 0, 0)),
                  pl.BlockSpec((bs, sl, d), lambda s: (samp(s), 0, 0)),
                  pl.BlockSpec((bs, d), lambda s: (samp(s), 0)),
                  pl.BlockSpec((bs, 3, d), lambda s: (samp(s), 0, 0)),
                  pl.BlockSpec((1, d), lambda s: (0, 0))],
        out_specs=[pl.BlockSpec((tl, d), lambda s: (s, 0)),
                   pl.BlockSpec((1, SUBLANES, d), lambda s: (pb(s), 0, 0)),
                   pl.BlockSpec((bs, d), lambda s: (samp(s), 0))],
        out_shape=[jax.ShapeDtypeStruct((b * l + bsz * sl, d), BF16),
                   jax.ShapeDtypeStruct((b, SUBLANES, d), F32),
                   jax.ShapeDtypeStruct((bsz, d), F32)],
        compiler_params=_params(("arbitrary",)),
        name="modulate",
    )(x_prompt, mod_p, x_slots, prev, mod_s, nw.reshape(1, d))


def _in_proj_body(x_ref, wt_ref, o_ref, w_scr):
    @pl.when(pl.program_id(1) == 0)
    def _():
        w_scr[...] = wt_ref[...].astype(BF16)

    o_ref[...] = _dot_nt(x_ref[...], w_scr[...])


def _in_proj(x, wt, n, name="in_proj"):
    m, k = x.shape
    tm = _tile(m, IN_PROJ_ROW_TILE, 16)
    tn = _tile(n, IN_PROJ_COL_TILE, LANES)
    return pl.pallas_call(
        _in_proj_body,
        grid=(n // tn, m // tm),
        in_specs=[pl.BlockSpec((tm, k), lambda j, i: (i, 0)),
                  pl.BlockSpec((tn, k), lambda j, i: (j, 0))],
        out_specs=pl.BlockSpec((tm, tn), lambda j, i: (i, j)),
        out_shape=jax.ShapeDtypeStruct((m, n), F32),
        scratch_shapes=[pltpu.VMEM((tn, k), BF16)],
        compiler_params=_params(("arbitrary", "arbitrary")),
        name=name,
    )(x, wt)


def _token_shift(cur_ref, carry_ref, mu_ref):
    cur = cur_ref[...]
    tl = cur.shape[0]
    prev = pltpu.roll(cur, 1, 0)
    if carry_ref is not None:
        row = lax.broadcasted_iota(jnp.int32, cur.shape, 0)
        prev = jnp.where(row == 0, carry_ref[0:1, :], prev)
        carry_ref[0:1, :] = cur[tl - 1:tl, :]
    return cur + (prev - cur) * mu_ref[...]


def _decay_and_rate(lo, w0_ref, w2_ref, a0_ref, a2_ref):
    dec = _dot(jnp.tanh(lo).astype(BF16), w2_ref[...])
    logw = -math.exp(-0.5) * jax.nn.sigmoid(w0_ref[...] + dec)
    a = jax.nn.sigmoid(a0_ref[...] + _dot(lo.astype(BF16), a2_ref[...]))
    return logw, a


def _zero_refs(*refs):
    for r in refs:
        r[...] = jnp.zeros_like(r)


def _split3(x):
    hi = x.astype(BF16)
    r1 = x - hi.astype(F32)
    mid = r1.astype(BF16)
    lo = (r1 - mid.astype(F32)).astype(BF16)
    return hi, mid, lo


def _dot_split3(m, parts):
    return _dot(m, parts[0]) + _dot(m, parts[1]) + _dot(m, parts[2])


def _rwkv_tile(pr_ref, pk_ref, pv_ref, pg_ref, pl_ref, mu_r, mu_k, mu_v, mu_l, w0_ref, w2_ref,
               a0_ref, a2_ref, kk_ref, ka_ref, rk_ref, lnw_ref, lnb_ref, y_ref,
               carries, seq, valid_slots, load_state, store_state):
    c = pr_ref.shape[0]
    hd = RWKV_HEAD_DIM
    gw = RWKV_GROUP * hd
    groups = range(pr_ref.shape[1] // gw)
    nseq = c // seq
    seqs = range(nseq)
    rows_of = [slice(q * seq, (q + 1) * seq) for q in seqs]
    c_r, c_k, c_v, c_l = carries

    r_all = _token_shift(pr_ref, c_r, mu_r)
    kw_all = _token_shift(pk_ref, c_k, mu_k)
    v_all = _token_shift(pv_ref, c_v, mu_v)
    lo = _token_shift(pl_ref, c_l, mu_l)
    logw_all, a_all = _decay_and_rate(lo, w0_ref, w2_ref, a0_ref, a2_ref)
    k_all = kw_all * (1.0 + (a_all - 1.0) * ka_ref[...])
    kkr_all = kw_all * kk_ref[...]
    if valid_slots is not None:
        slot = lax.broadcasted_iota(jnp.int32, (c, 1), 0) % seq
        valid = (slot >= valid_slots[0]) & (slot <= valid_slots[1])
        logw_all = jnp.where(valid, logw_all, 0.0)
        kkr_all = jnp.where(valid, kkr_all, 0.0)
        k_all = jnp.where(valid, k_all, 0.0)

    ti = lax.broadcasted_iota(jnp.int32, (c, c), 0)
    tj = lax.broadcasted_iota(jnp.int32, (c, c), 1)
    same_seq = (ti // seq) == (tj // seq)
    logw_parts = _split3(logw_all)
    cum_all = _dot_split3(((ti >= tj) & same_seq).astype(BF16), logw_parts)
    if nseq > 1:
        cum_end_all = _dot_split3(same_seq.astype(BF16), logw_parts)

    hr = lax.broadcasted_iota(jnp.int32, (gw, gw), 0) // hd
    hc = lax.broadcasted_iota(jnp.int32, (gw, gw), 1) // hd
    headmask = hr == hc
    tok = lax.broadcasted_iota(jnp.int32, (c, gw), 0)
    src = lax.broadcasted_iota(jnp.int32, (c, gw), 1) % hd
    same = (tok // seq) == (src // seq)
    strict = (tok > src) & same
    incl = (tok >= src) & same

    def bd(x):
        t = jnp.concatenate([x] * RWKV_GROUP, axis=0)
        return jnp.where(headmask, t, 0.0).astype(BF16)

    lane_head = lax.broadcasted_iota(jnp.int32, (c, gw), 1) // hd

    def head_sum(x):
        out = jnp.zeros_like(x)
        for j in range(RWKV_GROUP):
            mine = lane_head == j
            out = jnp.where(mine, jnp.sum(jnp.where(mine, x, 0.0), axis=-1, keepdims=True), out)
        return out

    def stages(gids):
        groups = range(len(gids))
        sls = [slice(g * gw, (g + 1) * gw) for g in gids]
        r = [r_all[:, sl] for sl in sls]
        k = [k_all[:, sl] for sl in sls]
        v = [v_all[:, sl] for sl in sls]
        cum = [cum_all[:, sl] for sl in sls]
        if nseq > 1:
            cum_end = [cum_end_all[:, sl] for sl in sls]
        else:
            cum_end = [cm[c - 1:c, :] for cm in cum]

        kk_sq = [head_sum(jnp.square(kkr_all[:, sl])) for sl in sls]
        rk_sum = [head_sum(r[g] * k[g] * rk_ref[:, sls[g]]) for g in groups]
        kkn = [kkr_all[:, sls[g]] * lax.rsqrt(jnp.maximum(kk_sq[g], 1e-24)) for g in groups]
        b = [kkn[g] * a_all[:, sls[g]] for g in groups]
        e_inv = [jnp.exp(-cum[g]) for g in groups]
        at = [-kkn[g] * jnp.exp(cum[g] - logw_all[:, sls[g]]) for g in groups]
        rt = [r[g] * jnp.exp(cum[g]) for g in groups]
        ar = [jnp.concatenate([at[g], rt[g]], axis=0).astype(BF16) for g in groups]
        states = [load_state(g) for g in gids]

        def from_state(g):
            if nseq == 1:
                fs = _dot_nt(ar[g], states[g][0].astype(BF16))
                return fs[:c], fs[c:]
            fa, fr = [], []
            for q in seqs:
                arq = jnp.concatenate([at[g][rows_of[q]], rt[g][rows_of[q]]], axis=0).astype(BF16)
                fs = _dot_nt(arq, states[g][q].astype(BF16))
                fa.append(fs[:seq])
                fr.append(fs[seq:])
            return jnp.concatenate(fa, axis=0), jnp.concatenate(fr, axis=0)

        fstate = [from_state(g) for g in groups]
        s_b = [_dot_nt(ar[g], bd(b[g] * e_inv[g])) for g in groups]
        s_k = [_dot_nt(ar[g], bd(k[g] * e_inv[g])) for g in groups]
        a_ab = [jnp.where(strict, x[:c], 0.0) for x in s_b]
        a_ak = [jnp.where(strict, x[:c], 0.0).astype(BF16) for x in s_k]
        a_r = [jnp.concatenate([jnp.where(incl, s_b[g][c:], 0.0), jnp.where(incl, s_k[g][c:], 0.0)],
                               axis=1).astype(BF16) for g in groups]

        steps = max(1, int(math.ceil(math.log2(seq))))
        pw = a_ab
        pm = a_ab
        pw_next = [_dot(pw[g].astype(BF16), bd(pw[g])) for g in groups] if steps > 1 else None
        for s in range(1, steps):
            pw = pw_next
            if s < steps - 1:
                res = [_dot(jnp.concatenate([pw[g], pm[g]], axis=0).astype(BF16), bd(pw[g])) for g in groups]
                pw_next = [x[:c] for x in res]
                pm = [pm[g] + pw[g] + res[g][c:] for g in groups]
            else:
                pm = [pm[g] + pw[g] + _dot(pm[g].astype(BF16), bd(pw[g])) for g in groups]

        bd_v = [bd(x) for x in v]
        w_rhs = [fstate[g][0] + _dot(a_ak[g], bd_v[g]) for g in groups]
        u = [w_rhs[g] + _dot(pm[g].astype(BF16), bd(w_rhs[g])) for g in groups]
        o = [fstate[g][1] + _dot(a_r[g], jnp.concatenate([bd(u[g]), bd_v[g]], axis=0)) for g in groups]
        for g in groups:
            e_end = jnp.exp(cum_end[g] - cum[g])
            b_end = b[g] * e_end
            k_end = k[g] * e_end
            for q in seqs:
                rq = rows_of[q]
                uv = jnp.concatenate([u[g][rq], v[g][rq]], axis=0).astype(BF16)
                bk = jnp.concatenate([b_end[rq], k_end[rq]], axis=0).astype(BF16)
                keep = jnp.exp(cum_end[g][q * seq:q * seq + 1, :]) if nseq > 1 else jnp.exp(cum_end[g])
                store_state(gids[g], q, states[g][q] * keep + _dot_tn(uv, bk))

        mean = [head_sum(x) * (1.0 / hd) for x in o]
        oc = [o[g] - mean[g] for g in groups]
        var = [head_sum(jnp.square(x)) * (1.0 / hd) for x in oc]
        for g in groups:
            sl = sls[g]
            og = oc[g] * lax.rsqrt(var[g] + RWKV_GN_EPS) * lnw_ref[:, sl] + lnb_ref[:, sl]
            y_ref[:, sl] = ((og + rk_sum[g] * v[g]) * _silu(pg_ref[:, sl])).astype(BF16)

    stages(list(groups))


def _head_mask(gw, hd):
    hr = lax.broadcasted_iota(jnp.int32, (gw, gw), 0) // hd
    hc = lax.broadcasted_iota(jnp.int32, (gw, gw), 1) // hd
    return hr == hc


def _rwkv_prompt_body(*refs):
    ins, (y_ref, so_ref), (c_r, c_k, c_v, c_l, s_scr) = refs[:18], refs[18:20], refs[20:]
    n = pl.program_id(1)
    hd = RWKV_HEAD_DIM
    gw = RWKV_GROUP * hd
    headmask = _head_mask(gw, hd)

    @pl.when(n == 0)
    def _():
        _zero_refs(c_r, c_k, c_v, c_l, s_scr)

    def load_state(g):
        return [s_scr[g]]

    def store_state(g, q, s):
        s_scr[g] = jnp.where(headmask, s, 0.0)

    _rwkv_tile(*ins, y_ref, (c_r, c_k, c_v, c_l), ins[0].shape[0], None, load_state, store_state)

    @pl.when(n == pl.num_programs(1) - 1)
    def _():
        for g in range(s_scr.shape[0]):
            sg = s_scr[g]
            for j in range(RWKV_GROUP):
                so_ref[0, g * RWKV_GROUP + j] = sg[j * hd:(j + 1) * hd, j * hd:(j + 1) * hd]


def _rwkv_prompt(proj, lora, nb, l, dr, mu_r, mu_k, mu_v, mu_l, w0, w2p, a0, a2p, k_k, k_a, r_k, ln_w, ln_b):
    c = RWKV_CHUNK
    assert l % c == 0 and c == RWKV_HEAD_DIM
    nchunk = l // c
    lp = lora.shape[1]
    nh = dr // RWKV_HEAD_DIM
    gw = RWKV_GROUP * RWKV_HEAD_DIM
    vec = lambda: pl.BlockSpec((1, dr), lambda b, n: (0, 0))
    sect = lambda s: pl.BlockSpec((c, dr), lambda b, n, s=s: (b * nchunk + n, 4 + s))
    return pl.pallas_call(
        _rwkv_prompt_body,
        grid=(nb, nchunk),
        in_specs=[sect(0), sect(1), sect(2), sect(3),
                  pl.BlockSpec((c, lp), lambda b, n: (b * nchunk + n, 0)),
                  vec(), vec(), vec(),
                  pl.BlockSpec((1, lp), lambda b, n: (0, 0)),
                  vec(),
                  pl.BlockSpec((lp, dr), lambda b, n: (0, 0)),
                  vec(),
                  pl.BlockSpec((lp, dr), lambda b, n: (0, 0)),
                  vec(), vec(), vec(), vec(), vec()],
        out_specs=[pl.BlockSpec((c, dr), lambda b, n: (b * nchunk + n, 0)),
                   pl.BlockSpec((1, nh, RWKV_HEAD_DIM, RWKV_HEAD_DIM), lambda b, n: (b, 0, 0, 0))],
        out_shape=[jax.ShapeDtypeStruct((nb * l, dr), BF16),
                   jax.ShapeDtypeStruct((nb, nh, RWKV_HEAD_DIM, RWKV_HEAD_DIM), F32)],
        scratch_shapes=[pltpu.VMEM((SUBLANES, dr), F32), pltpu.VMEM((SUBLANES, dr), F32),
                        pltpu.VMEM((SUBLANES, dr), F32), pltpu.VMEM((SUBLANES, lp), F32),
                        pltpu.VMEM((dr // gw, gw, gw), F32)],
        compiler_params=_params(("parallel", "arbitrary")),
        name="rwkv_chunked",
    )(proj, proj, proj, proj, lora, mu_r, mu_k, mu_v, mu_l, w0, w2p, a0, a2p, k_k, k_a, r_k, ln_w, ln_b)


def _rwkv_sample_body(*refs, n_tok):
    ins, s0_ref, y_ref, so_ref = refs[:18], refs[18], refs[19], refs[20]
    hd = RWKV_HEAD_DIM
    zero = jnp.zeros((hd, hd), F32)

    def load_state(g):
        out = []
        per_head = [jnp.swapaxes(s0_ref[g * RWKV_GROUP + j], 0, 1) for j in range(RWKV_GROUP)]
        for q in range(s0_ref.shape[2]):
            rows = [jnp.concatenate([per_head[j][q] if i == j else zero
                                     for i in range(RWKV_GROUP)], axis=1) for j in range(RWKV_GROUP)]
            out.append(jnp.concatenate(rows, axis=0))
        return out

    def store_state(g, q, s):
        for j in range(RWKV_GROUP):
            so_ref[q, g * RWKV_GROUP + j] = s[j * hd:(j + 1) * hd, j * hd:(j + 1) * hd]

    _rwkv_tile(*ins, y_ref, (None, None, None, None), SAMPLE_SLOTS, (1, n_tok), load_state, store_state)


def _rwkv_sample(proj, lora, row0, s0, n_tok, dr, mu_r, mu_k, mu_v, mu_l, w0, w2p, a0, a2p, k_k, k_a,
                 r_k, ln_w, ln_b):
    c = RWKV_CHUNK
    nh_all, _, bsz, _ = s0.shape
    rows = bsz * SAMPLE_SLOTS
    assert rows % c == 0 and row0 % c == 0 and c == RWKV_HEAD_DIM
    t0 = row0 // c
    nseq = c // SAMPLE_SLOTS
    lp = lora.shape[1]
    hd = RWKV_HEAD_DIM
    dw = _tile(dr, RWKV_SAMPLE_LANE_TILE, RWKV_GROUP * hd)
    nw = dr // dw
    vec = lambda: pl.BlockSpec((1, dw), lambda i, hf: (0, hf))
    sect = lambda s: pl.BlockSpec((c, dw), lambda i, hf, s=s: (t0 + i, (4 + s) * nw + hf))
    lora_w = lambda: pl.BlockSpec((lp, dw), lambda i, hf: (0, hf))
    st_in = pl.BlockSpec((dw // hd, hd, nseq, hd), lambda i, hf: (hf, 0, i, 0))
    st_out = pl.BlockSpec((nseq, dw // hd, hd, hd), lambda i, hf: (i, hf, 0, 0))
    return pl.pallas_call(
        functools.partial(_rwkv_sample_body, n_tok=n_tok),
        grid=(rows // c, nw),
        in_specs=[sect(0), sect(1), sect(2), sect(3),
                  pl.BlockSpec((c, lp), lambda i, hf: (t0 + i, 0)),
                  vec(), vec(), vec(),
                  pl.BlockSpec((1, lp), lambda i, hf: (0, 0)),
                  vec(), lora_w(), vec(), lora_w(),
                  vec(), vec(), vec(), vec(), vec(), st_in],
        out_specs=[pl.BlockSpec((c, dw), lambda i, hf: (i, hf)), st_out],
        out_shape=[jax.ShapeDtypeStruct((rows, dr), BF16),
                   jax.ShapeDtypeStruct((bsz, nh_all, hd, hd), F32)],
        compiler_params=_params(("parallel", "parallel")),
        name="rwkv_chunked_sample",
    )(proj, proj, proj, proj, lora, mu_r, mu_k, mu_v, mu_l, w0, w2p, a0, a2p, k_k, k_a, r_k, ln_w, ln_b, s0)


def _rotary(x, cos, sin):
    half = x.shape[-1] // 2
    x1 = x[:, :half]
    x2 = x[:, half:]
    return jnp.concatenate([x1 * cos - x2 * sin, x1 * sin + x2 * cos], axis=-1)


def _group_norm_rows(o, eps):
    mean = jnp.mean(o, axis=-1, keepdims=True)
    oc = o - mean
    var = jnp.mean(oc * oc, axis=-1, keepdims=True)
    return oc * lax.rsqrt(var + eps)


def _ret_prompt_body(lg_ref, q_ref, k_ref, v_ref, g_ref, cos_ref, sin_ref, y_ref, s_ref):
    n = pl.program_id(1)
    c = q_ref.shape[0]
    hd = RET_HEAD_DIM
    heads = range(q_ref.shape[1] // hd)
    cols = [slice(h * hd, (h + 1) * hd) for h in heads]

    @pl.when(n == 0)
    def _():
        s_ref[...] = jnp.zeros_like(s_ref)

    cos = cos_ref[...]
    sin = sin_ref[...]
    ii = lax.broadcasted_iota(jnp.int32, (c, c), 0)
    jj = lax.broadcasted_iota(jnp.int32, (c, c), 1)
    diff = (ii - jj).astype(F32)
    causal = diff >= 0.0
    dist = jnp.maximum(diff, 0.0)
    ic = lax.broadcasted_iota(jnp.int32, (c, 1), 0).astype(F32)

    lg = [jnp.full((1, 1), lg_ref[h], F32) for h in heads]
    q = [_rotary(q_ref[:, cs], cos, sin).astype(BF16) for cs in cols]
    k = [_rotary(k_ref[:, cs], cos, sin) * (hd ** -0.5) for cs in cols]
    vb = [v_ref[:, cs].astype(BF16) for cs in cols]
    s = [s_ref[0, h] for h in heads]
    scores = [_dot_nt(q[h], k[h].astype(BF16)) * jnp.where(causal, jnp.exp(dist * lg[h]), 0.0) for h in heads]
    cross = [_dot(q[h], s[h].astype(BF16)) * jnp.exp((ic + 1.0) * lg[h]) for h in heads]
    o = [_dot(scores[h].astype(BF16), vb[h]) + cross[h] for h in heads]
    for h in heads:
        kd = (k[h] * jnp.exp((c - 1.0 - ic) * lg[h])).astype(BF16)
        s_ref[0, h] = s[h] * jnp.exp(float(c) * lg[h]) + _dot_tn(kd, vb[h])
    for h in heads:
        y_ref[:, cols[h]] = (_group_norm_rows(o[h], GN_EPS) * _silu(g_ref[:, cols[h]])).astype(BF16)


def _ret_prompt(proj, lg, cos, sin, nb, l, nh):
    hd = RET_HEAD_DIM
    dr = nh * hd
    c = math.gcd(l, RET_CHUNK)
    nchunk = l // c
    sect = lambda s: pl.BlockSpec((c, dr), lambda b, n, s=s: (b * nchunk + n, s))
    tab = pl.BlockSpec((c, hd // 2), lambda b, n: (n, 0))
    return pl.pallas_call(
        _ret_prompt_body,
        grid=(nb, nchunk),
        in_specs=[pl.BlockSpec(memory_space=pltpu.SMEM), sect(0), sect(1), sect(2), sect(3), tab, tab],
        out_specs=[pl.BlockSpec((c, dr), lambda b, n: (b * nchunk + n, 0)),
                   pl.BlockSpec((1, nh, hd, hd), lambda b, n: (b, 0, 0, 0))],
        out_shape=[jax.ShapeDtypeStruct((nb * l, dr), BF16),
                   jax.ShapeDtypeStruct((nb, nh, hd, hd), F32)],
        compiler_params=_params(("parallel", "arbitrary")),
        name="retention_prompt",
    )(lg, proj, proj, proj, proj, cos, sin)


def _ret_sample_body(lg_ref, q_ref, k_ref, v_ref, g_ref, cos_ref, sin_ref, s0_ref, y_ref, s_ref, *, nh, n_tok):
    hd = RET_HEAD_DIM
    bs = s0_ref.shape[0]
    sl = SAMPLE_SLOTS
    cos = cos_ref[...]
    sin = sin_ref[...]
    slot_r = lax.broadcasted_iota(jnp.int32, (sl, 1), 0)
    tok_r = (slot_r - 1).astype(F32)
    valid_r = (slot_r >= 1) & (slot_r <= n_tok)
    ii = lax.broadcasted_iota(jnp.int32, (sl, sl), 0)
    jj = lax.broadcasted_iota(jnp.int32, (sl, sl), 1)
    diff = (ii - jj).astype(F32)
    pair_ok = (diff >= 0.0) & (jj >= 1) & (jj <= n_tok)
    for h in range(nh):
        lg = jnp.full((1, 1), lg_ref[h], F32)
        dmask = jnp.where(pair_ok, jnp.exp(jnp.maximum(diff, 0.0) * lg), 0.0)
        cross_decay = jnp.exp((tok_r + 1.0) * lg)
        key_decay = jnp.where(valid_r, jnp.exp((n_tok - 1.0 - tok_r) * lg), 0.0)
        chunk_decay = jnp.exp(float(n_tok) * lg)
        for b in range(bs):
            rows = slice(b * sl, (b + 1) * sl)
            cols = slice(h * hd, (h + 1) * hd)
            q = _rotary(q_ref[rows, cols], cos, sin)
            k = _rotary(k_ref[rows, cols], cos, sin) * (hd ** -0.5)
            vb = v_ref[rows, cols].astype(BF16)
            s = s0_ref[b, h]
            qb = q.astype(BF16)
            scores = _dot_nt(qb, k.astype(BF16)) * dmask
            inner = _dot(scores.astype(BF16), vb)
            cross = _dot(qb, s.astype(BF16)) * cross_decay
            kd_t = (k * key_decay).T.astype(BF16)
            s_ref[b, h] = s * chunk_decay + _dot(kd_t, vb)
            o = inner + cross
            y_ref[rows, cols] = (_group_norm_rows(o, GN_EPS) * _silu(g_ref[rows, cols])).astype(BF16)


def _ret_sample(proj, row0, lg, cos, sin, s0, nh, n_tok):
    bsz = s0.shape[0]
    hd = RET_HEAD_DIM
    dr = nh * hd
    bs = _tile(bsz, RET_SAMPLE_SEQ_TILE, 2)
    rows = bs * SAMPLE_SLOTS
    assert row0 % rows == 0
    t0 = row0 // rows
    sect = lambda s: pl.BlockSpec((rows, dr), lambda i, s=s: (t0 + i, s))
    tab = pl.BlockSpec((SAMPLE_SLOTS, hd // 2), lambda i: (0, 0))
    st = pl.BlockSpec((bs, nh, hd, hd), lambda i: (i, 0, 0, 0))
    return pl.pallas_call(
        functools.partial(_ret_sample_body, nh=nh, n_tok=n_tok),
        grid=(bsz // bs,),
        in_specs=[pl.BlockSpec(memory_space=pltpu.SMEM), sect(0), sect(1), sect(2), sect(3), tab, tab, st],
        out_specs=[pl.BlockSpec((rows, dr), lambda i: (i, 0)), st],
        out_shape=[jax.ShapeDtypeStruct((bsz * SAMPLE_SLOTS, dr), BF16),
                   jax.ShapeDtypeStruct(s0.shape, F32)],
        compiler_params=_params(("parallel",)),
        name="retention_sample",
    )(lg, proj, proj, proj, proj, cos, sin, s0)


def _out_core(yr_ref, yw_ref, w1_ref, w2_ref, x, gate, fw):
    y = _dot(yr_ref[...], w1_ref[...]) + _dot(yw_ref[...], w2_ref[...])
    xn = x + gate * y
    ms = jnp.mean(xn * xn, axis=-1, keepdims=True)
    return xn * lax.rsqrt(ms + NORM_EPS) * fw


def _out_prompt_body(yr_ref, yw_ref, w1_ref, w2_ref, x_ref, mod_ref, fw_ref, o_ref):
    o_ref[0] = _out_core(yr_ref, yw_ref, w1_ref, w2_ref, x_ref[0], mod_ref[0][2:3], fw_ref[...])


def _out_prompt(y_ret, y_rw, w, x, mod3, fw):
    b, l, d = x.shape
    dr = w.shape[0] // 2
    tl = _tile(l, OUT_PROJ_ROW_TILE, 16)
    nt = l // tl
    wspec = lambda half: pl.BlockSpec((dr, d), lambda bi, i: (half, 0), pipeline_mode=pl.Buffered(1))
    yspec = pl.BlockSpec((tl, dr), lambda bi, i: (bi * nt + i, 0))
    return pl.pallas_call(
        _out_prompt_body,
        grid=(b, nt),
        in_specs=[yspec, yspec, wspec(0), wspec(1),
                  pl.BlockSpec((1, tl, d), lambda bi, i: (bi, i, 0)),
                  pl.BlockSpec((1, 3, d), lambda bi, i: (bi, 0, 0)),
                  pl.BlockSpec((1, d), lambda bi, i: (0, 0))],
        out_specs=pl.BlockSpec((1, tl, d), lambda bi, i: (bi, i, 0)),
        out_shape=jax.ShapeDtypeStruct((b, l, d), F32),
        compiler_params=_params(("parallel", "parallel")),
        name="out_proj_prompt",
    )(y_ret, y_rw, w, w, x, mod3, fw.reshape(1, d))


def _out_sample_body(yr_ref, yw_ref, w1_ref, w2_ref, x_ref, mod_ref, fw_ref, o_ref):
    bs, sl, d = x_ref.shape
    gate = jnp.broadcast_to(mod_ref[...][:, 2:3, :], (bs, sl, d)).reshape(bs * sl, d)
    x = x_ref[...].reshape(bs * sl, d)
    o = _out_core(yr_ref, yw_ref, w1_ref, w2_ref, x, gate, fw_ref[...])
    o_ref[...] = o.reshape(bs, sl, d)


def _out_sample(y_ret, y_rw, w, x_slots, mod3, fw):
    bsz, sl, d = x_slots.shape
    dr = w.shape[0] // 2
    bs = _tile(bsz, OUT_SAMPLE_SEQ_TILE, SUBLANES)
    wspec = lambda half: pl.BlockSpec((dr, d), lambda i: (half, 0), pipeline_mode=pl.Buffered(1))
    yspec = pl.BlockSpec((bs * sl, dr), lambda i: (i, 0))
    xspec = pl.BlockSpec((bs, sl, d), lambda i: (i, 0, 0))
    return pl.pallas_call(
        _out_sample_body,
        grid=(bsz // bs,),
        in_specs=[yspec, yspec, wspec(0), wspec(1), xspec,
                  pl.BlockSpec((bs, 3, d), lambda i: (i, 0, 0)),
                  pl.BlockSpec((1, d), lambda i: (0, 0))],
        out_specs=xspec,
        out_shape=jax.ShapeDtypeStruct((bsz, sl, d), F32),
        compiler_params=_params(("parallel",)),
        name="out_proj_sample",
    )(y_ret, y_rw, w, w, x_slots, mod3, fw.reshape(1, d))


def _rope_tables(pos, half):
    inv_freq = ROPE_THETA ** (-jnp.arange(half, dtype=F32) / half)
    ang = pos[:, None] * inv_freq[None, :]
    return jnp.cos(ang), jnp.sin(ang)


def kernel(x_prompt, x_sample, c_prompt, c_sample, state_ret, state_rwkv, state_shift, norm_w, w_ada,
           b_ada, w_in, mu_shift, w0_decay, w2_decay, a0, a2, k_k, k_a, r_k, ln_x_w, ln_x_b, w_out,
           final_norm_w):
    depth = w_in.shape[0]
    assert depth == 1, "single-layer trunk"
    bp, lp, d = x_prompt.shape
    bsz, ls, _ = x_sample.shape
    assert ls == 4, "sample path packs 4 tokens into slots 1..4"
    dr = d
    nh_ret = dr // RET_HEAD_DIM
    n_main = 8 * dr
    lora = w2_decay.shape[1]
    lora_pad = -(-2 * lora // LANES) * LANES

    w_t = jnp.swapaxes(w_in[0], 0, 1)
    w_t_lora = jnp.pad(w_t[n_main:], ((0, lora_pad - 2 * lora), (0, 0)))
    w_o = w_out[0].astype(BF16)
    mu = mu_shift[0]
    row = lambda p: p.reshape(1, -1)
    mu_r, mu_k, mu_v = row(mu[0:dr]), row(mu[dr:2 * dr]), row(mu[2 * dr:3 * dr])
    mu_l = jnp.pad(mu[3 * dr:], (0, lora_pad - 2 * lora)).reshape(1, lora_pad)
    w2p = jnp.pad(w2_decay[0], ((0, lora_pad - lora), (0, 0))).astype(BF16)
    a2p = jnp.pad(a2[0], ((lora, lora_pad - 2 * lora), (0, 0))).astype(BF16)
    lg = jnp.log1p(-jnp.exp2(-5.0 - jnp.arange(nh_ret, dtype=F32)))
    rw_params = (mu_r, mu_k, mu_v, mu_l, row(w0_decay[0]), w2p, row(a0[0]), a2p, row(k_k[0]), row(k_a[0]),
                 row(r_k[0]), row(ln_x_w[0]), row(ln_x_b[0]))

    n_c = bp + bsz
    n_c_pad = -(-n_c // SUBLANES) * SUBLANES
    c_all = jnp.pad(jnp.concatenate([c_prompt, c_sample], axis=0), ((0, n_c_pad - n_c), (0, 0)))
    mod3 = _adaln(c_all, w_ada[0], b_ada[0]).reshape(n_c_pad, 3, d)
    mod_p, mod_s = mod3[:bp], mod3[bp:bp + bsz]

    sl = SAMPLE_SLOTS
    row_s = bp * lp
    x_slots = jnp.pad(x_sample, ((0, 0), (1, sl - 1 - ls), (0, 0)))
    h_all, last_p, new_shift_s = _modulate(x_prompt, mod_p, x_slots, state_shift[0], mod_s, norm_w[0])
    new_shift_p = last_p[:, SUBLANES - 1, :]
    proj = _in_proj(h_all, w_t, n_main)
    lora_proj = _in_proj(h_all, w_t_lora, lora_pad, name="in_proj_lora")

    cos_p, sin_p = _rope_tables(jnp.arange(lp, dtype=F32), RET_HEAD_DIM // 2)
    y_ret_p, s_ret_p = _ret_prompt(proj, lg, cos_p, sin_p, bp, lp, nh_ret)
    y_rw_p, s_rw_p = _rwkv_prompt(proj, lora_proj, bp, lp, dr, *rw_params)
    y_prompt = _out_prompt(y_ret_p, y_rw_p, w_o, x_prompt, mod_p, final_norm_w)

    slot_pos = jnp.arange(sl, dtype=F32) - 1.0
    pos_s = jnp.where((slot_pos >= 0) & (slot_pos < ls), float(PAST_LEN) + slot_pos, 0.0)
    cos_s, sin_s = _rope_tables(pos_s, RET_HEAD_DIM // 2)
    y_ret_s, s_ret_s = _ret_sample(proj, row_s, lg, cos_s, sin_s, state_ret[0], nh_ret, ls)
    s0_rw = jnp.transpose(state_rwkv[0], (1, 2, 0, 3))
    y_rw_s, s_rw_s = _rwkv_sample(proj, lora_proj, row_s, s0_rw, ls, dr, *rw_params)
    y_slots = _out_sample(y_ret_s, y_rw_s, w_o, x_slots, mod_s, final_norm_w)
    y_sample = y_slots[:, 1:1 + ls, :]

    return (y_prompt, y_sample, s_ret_p[None], s_rw_p[None], new_shift_p[None],
            s_ret_s[None], s_rw_s[None], new_shift_s[None])
```

```python
import functools
import math

import jax
import jax.numpy as jnp
from jax import lax
from jax.experimental import pallas as pl
from jax.experimental.pallas import tpu as pltpu

F32 = jnp.float32
BF16 = jnp.bfloat16

RET_HEAD_DIM = 256
RWKV_HEAD_DIM = 64
RET_CHUNK = 128
RWKV_CHUNK = 64
RWKV_GROUP = 2
PAST_LEN = 16384
ROPE_THETA = 10000.0
NORM_EPS = 1e-6
GN_EPS = 1e-5
RWKV_GN_EPS = 64e-5
SAMPLE_SLOTS = 8
LANES = 128
SUBLANES = 8
VMEM_LIMIT_BYTES = 56 * 1024 * 1024

ADALN_COL_TILE = 768
MODULATE_ROW_TILE = 512
IN_PROJ_ROW_TILE = 1536
IN_PROJ_COL_TILE = 1024
OUT_PROJ_ROW_TILE = 512
OUT_SAMPLE_SEQ_TILE = 32
RET_SAMPLE_SEQ_TILE = 4
RET_STEP_CHUNKS = 2
RWKV_STEP_CHUNKS = 2
RWKV_SAMPLE_STEP_TILES = 1
RWKV_SAMPLE_LANE_TILE = 1024


def _params(sem):
    return pltpu.CompilerParams(dimension_semantics=sem, vmem_limit_bytes=VMEM_LIMIT_BYTES)


def _tile(n, cap, align):
    if n <= cap:
        return n
    t = (cap // align) * align
    while t >= align:
        if n % t == 0:
            return t
        t -= align
    return n


def _silu(x):
    return x * jax.nn.sigmoid(x)


def _dot(a, b):
    return jnp.dot(a, b, preferred_element_type=F32)


def _dot_nt(a, b):
    return lax.dot_general(a, b, (((1,), (1,)), ((), ())), preferred_element_type=F32)


def _dot_tn(a, b):
    return lax.dot_general(a, b, (((0,), (0,)), ((), ())), preferred_element_type=F32)


def _adaln_body(c_ref, w_ref, b_ref, o_ref):
    s = _silu(c_ref[...]).astype(BF16)
    o_ref[...] = _dot(s, w_ref[...].astype(BF16)) + b_ref[...]


def _adaln(c, w_ada, b_ada):
    rows, d = c.shape
    n = w_ada.shape[1]
    tn = _tile(n, ADALN_COL_TILE, LANES)
    return pl.pallas_call(
        _adaln_body,
        grid=(n // tn,),
        in_specs=[pl.BlockSpec((rows, d), lambda j: (0, 0)),
                  pl.BlockSpec((d, tn), lambda j: (0, j)),
                  pl.BlockSpec((1, tn), lambda j: (0, j))],
        out_specs=pl.BlockSpec((rows, tn), lambda j: (0, j)),
        out_shape=jax.ShapeDtypeStruct((rows, n), F32),
        compiler_params=_params(("parallel",)),
        name="adaln",
    )(c, w_ada, b_ada.reshape(1, n))


def _modulated(x, nw, shift, scale):
    ms = jnp.mean(x * x, axis=-1, keepdims=True)
    return x * lax.rsqrt(ms + NORM_EPS) * (nw * (1.0 + scale)) + shift


def _modulate_body(xp_ref, modp_ref, xs_ref, prev_ref, mods_ref, nw_ref, h_ref, last_ref, new_ref, *, n_prompt):
    s = pl.program_id(0)

    @pl.when(s < n_prompt)
    def _():
        m = modp_ref[0]
        h = _modulated(xp_ref[0], nw_ref[...], m[0:1], m[1:2])
        h_ref[...] = h.astype(BF16)
        tl = h.shape[0]
        last_ref[0] = h[tl - SUBLANES:tl]

    @pl.when(s >= n_prompt)
    def _():
        x = xs_ref[...]
        m = mods_ref[...]
        h = _modulated(x, nw_ref[...], m[:, 0:1, :], m[:, 1:2, :])
        slot = lax.broadcasted_iota(jnp.int32, h.shape, 1)
        full = jnp.where(slot == 0, prev_ref[...][:, None, :], h)
        bs, sl, d = x.shape
        h_ref[...] = full.reshape(bs * sl, d).astype(BF16)
        new_ref[...] = h[:, 4, :]


def _modulate(x_prompt, mod_p, x_slots, prev, mod_s, nw):
    b, l, d = x_prompt.shape
    bsz, sl, _ = x_slots.shape
    tl = _tile(l, min(MODULATE_ROW_TILE, bsz * sl), 16)
    nt = l // tl
    bs = tl // sl
    assert bsz % bs == 0 and bs % SUBLANES == 0
    n_prompt = b * nt
    n_sample = bsz // bs
    samp = lambda s: jnp.maximum(s - n_prompt, 0)
    pb = lambda s: jnp.minimum(s // nt, b - 1)
    return pl.pallas_call(
        functools.partial(_modulate_body, n_prompt=n_prompt),
        grid=(n_prompt + n_sample,),
        in_specs=[pl.BlockSpec((1, tl, d), lambda s: (pb(s), jnp.where(s < n_prompt, s % nt, nt - 1), 0)),
                  pl.BlockSpec((1, 3, d), lambda s: (pb(s), 0, 0)),
                  pl.BlockSpec((bs, sl, d), lambda s: (samp(s), 0, 0)),
                  pl.BlockSpec((bs, d), lambda s: (samp(s), 0)),
                  pl.BlockSpec((bs, 3, d), lambda s: (samp(s), 0, 0)),
                  pl.BlockSpec((1, d), lambda s: (0, 0))],
        out_specs=[pl.BlockSpec((tl, d), lambda s: (s, 0)),
                   pl.BlockSpec((1, SUBLANES, d), lambda s: (pb(s), 0, 0)),
                   pl.BlockSpec((bs, d), lambda s: (samp(s), 0))],
        out_shape=[jax.ShapeDtypeStruct((b * l + bsz * sl, d), BF16),
                   jax.ShapeDtypeStruct((b, SUBLANES, d), F32),
                   jax.ShapeDtypeStruct((bsz, d), F32)],
        compiler_params=_params(("arbitrary",)),
        name="modulate",
    )(x_prompt, mod_p, x_slots, prev, mod_s, nw.reshape(1, d))


def _in_proj_body(x_ref, wt_ref, o_ref, w_scr):
    @pl.when(pl.program_id(1) == 0)
    def _():
        w_scr[...] = wt_ref[...].astype(BF16)

    o_ref[...] = _dot_nt(x_ref[...], w_scr[...])


def _in_proj(x, wt, n, name="in_proj"):
    m, k = x.shape
    tm = _tile(m, IN_PROJ_ROW_TILE, 16)
    tn = _tile(n, IN_PROJ_COL_TILE, LANES)
    return pl.pallas_call(
        _in_proj_body,
        grid=(n // tn, m // tm),
        in_specs=[pl.BlockSpec((tm, k), lambda j, i: (i, 0)),
                  pl.BlockSpec((tn, k), lambda j, i: (j, 0))],
        out_specs=pl.BlockSpec((tm, tn), lambda j, i: (i, j)),
        out_shape=jax.ShapeDtypeStruct((m, n), F32),
        scratch_shapes=[pltpu.VMEM((tn, k), BF16)],
        compiler_params=_params(("arbitrary", "arbitrary")),
        name=name,
    )(x, wt)


def _token_shift(cur_ref, carry_ref, mu_ref, cols):
    cur = cur_ref[:, cols]
    tl = cur.shape[0]
    prev = pltpu.roll(cur, 1, 0)
    if carry_ref is not None:
        row = lax.broadcasted_iota(jnp.int32, cur.shape, 0)
        prev = jnp.where(row == 0, carry_ref[0:1, cols], prev)
        carry_ref[0:1, cols] = cur[tl - 1:tl, :]
    return cur + (prev - cur) * mu_ref[:, cols]


def _decay_and_rate(lo_tanh_b, lo_b, w0_ref, w2_ref, a0_ref, a2_ref, cols):
    dec = _dot(lo_tanh_b, w2_ref[:, cols])
    logw = -math.exp(-0.5) * jax.nn.sigmoid(w0_ref[:, cols] + dec)
    a = jax.nn.sigmoid(a0_ref[:, cols] + _dot(lo_b, a2_ref[:, cols]))
    return logw, a


def _zero_refs(*refs):
    for r in refs:
        r[...] = jnp.zeros_like(r)


def _split3(x):
    hi = x.astype(BF16)
    r1 = x - hi.astype(F32)
    mid = r1.astype(BF16)
    lo = (r1 - mid.astype(F32)).astype(BF16)
    return hi, mid, lo


def _dot_split3(m, parts):
    return _dot(m, parts[0]) + _dot(m, parts[1]) + _dot(m, parts[2])


def _rwkv_tile(pr_ref, pk_ref, pv_ref, pg_ref, pl_ref, mu_r, mu_k, mu_v, mu_l, w0_ref, w2_ref,
               a0_ref, a2_ref, kk_ref, ka_ref, rk_ref, lnw_ref, lnb_ref, y_ref,
               carries, seq, valid_slots, load_state, store_state):
    c = pr_ref.shape[0]
    hd = RWKV_HEAD_DIM
    gw = RWKV_GROUP * hd
    groups = range(pr_ref.shape[1] // gw)
    nseq = c // seq
    seqs = range(nseq)
    rows_of = [slice(q * seq, (q + 1) * seq) for q in seqs]
    c_r, c_k, c_v, c_l = carries

    lo = _token_shift(pl_ref, c_l, mu_l, slice(None))
    lo_b = lo.astype(BF16)
    lo_tanh_b = jnp.tanh(lo).astype(BF16)
    if valid_slots is not None:
        slot = lax.broadcasted_iota(jnp.int32, (c, 1), 0) % seq
        valid = (slot >= valid_slots[0]) & (slot <= valid_slots[1])

    ti = lax.broadcasted_iota(jnp.int32, (c, c), 0)
    tj = lax.broadcasted_iota(jnp.int32, (c, c), 1)
    same_seq = (ti // seq) == (tj // seq)
    tri_b = ((ti >= tj) & same_seq).astype(BF16)
    same_b = same_seq.astype(BF16)

    hr = lax.broadcasted_iota(jnp.int32, (gw, gw), 0) // hd
    hc = lax.broadcasted_iota(jnp.int32, (gw, gw), 1) // hd
    headmask = hr == hc
    tok = lax.broadcasted_iota(jnp.int32, (c, gw), 0)
    src = lax.broadcasted_iota(jnp.int32, (c, gw), 1) % hd
    same = (tok // seq) == (src // seq)
    strict = (tok > src) & same
    incl = (tok >= src) & same

    def bd(x):
        t = jnp.concatenate([x] * RWKV_GROUP, axis=0)
        return jnp.where(headmask, t, 0.0).astype(BF16)

    lane_head = lax.broadcasted_iota(jnp.int32, (c, gw), 1) // hd

    def head_sum(x):
        out = jnp.zeros_like(x)
        for j in range(RWKV_GROUP):
            mine = lane_head == j
            out = jnp.where(mine, jnp.sum(jnp.where(mine, x, 0.0), axis=-1, keepdims=True), out)
        return out

    def stages(gids):
        groups = range(len(gids))
        sls = [slice(g * gw, (g + 1) * gw) for g in gids]
        r = [_token_shift(pr_ref, c_r, mu_r, sl) for sl in sls]
        kw = [_token_shift(pk_ref, c_k, mu_k, sl) for sl in sls]
        v = [_token_shift(pv_ref, c_v, mu_v, sl) for sl in sls]
        lw_a = [_decay_and_rate(lo_tanh_b, lo_b, w0_ref, w2_ref, a0_ref, a2_ref, sl) for sl in sls]
        logw = [x[0] for x in lw_a]
        rate = [x[1] for x in lw_a]
        k = [kw[g] * (1.0 + (rate[g] - 1.0) * ka_ref[:, sls[g]]) for g in groups]
        kkr = [kw[g] * kk_ref[:, sls[g]] for g in groups]
        if valid_slots is not None:
            logw = [jnp.where(valid, x, 0.0) for x in logw]
            kkr = [jnp.where(valid, x, 0.0) for x in kkr]
            k = [jnp.where(valid, x, 0.0) for x in k]

        parts = [_split3(x) for x in logw]
        cum = [_dot_split3(tri_b, p) for p in parts]
        if nseq > 1:
            cum_end = [_dot_split3(same_b, p) for p in parts]
        else:
            cum_end = [cm[c - 1:c, :] for cm in cum]

        kk_sq = [head_sum(jnp.square(x)) for x in kkr]
        rk_sum = [head_sum(r[g] * k[g] * rk_ref[:, sls[g]]) for g in groups]
        kkn = [kkr[g] * lax.rsqrt(jnp.maximum(kk_sq[g], 1e-24)) for g in groups]
        b = [kkn[g] * rate[g] for g in groups]
        e_inv = [jnp.exp(-cum[g]) for g in groups]
        at = [-kkn[g] * jnp.exp(cum[g] - logw[g]) for g in groups]
        rt = [r[g] * jnp.exp(cum[g]) for g in groups]
        ar = [jnp.concatenate([at[g], rt[g]], axis=0).astype(BF16) for g in groups]
        states = [load_state(g) for g in gids]

        def from_state(g):
            if nseq == 1:
                fs = _dot_nt(ar[g], states[g][0].astype(BF16))
                return fs[:c], fs[c:]
            fa, fr = [], []
            for q in seqs:
                arq = jnp.concatenate([at[g][rows_of[q]], rt[g][rows_of[q]]], axis=0).astype(BF16)
                fs = _dot_nt(arq, states[g][q].astype(BF16))
                fa.append(fs[:seq])
                fr.append(fs[seq:])
            return jnp.concatenate(fa, axis=0), jnp.concatenate(fr, axis=0)

        fstate = [from_state(g) for g in groups]
        s_b = [_dot_nt(ar[g], bd(b[g] * e_inv[g])) for g in groups]
        s_k = [_dot_nt(ar[g], bd(k[g] * e_inv[g])) for g in groups]
        a_ab = [jnp.where(strict, x[:c], 0.0) for x in s_b]
        a_ak = [jnp.where(strict, x[:c], 0.0).astype(BF16) for x in s_k]
        a_r = [jnp.concatenate([jnp.where(incl, s_b[g][c:], 0.0), jnp.where(incl, s_k[g][c:], 0.0)],
                               axis=1).astype(BF16) for g in groups]

        steps = max(1, int(math.ceil(math.log2(seq))))
        pw = a_ab
        pm = a_ab
        pw_next = [_dot(pw[g].astype(BF16), bd(pw[g])) for g in groups] if steps > 1 else None
        for s in range(1, steps):
            pw = pw_next
            if s < steps - 1:
                res = [_dot(jnp.concatenate([pw[g], pm[g]], axis=0).astype(BF16), bd(pw[g])) for g in groups]
                pw_next = [x[:c] for x in res]
                pm = [pm[g] + pw[g] + res[g][c:] for g in groups]
            else:
                pm = [pm[g] + pw[g] + _dot(pm[g].astype(BF16), bd(pw[g])) for g in groups]

        bd_v = [bd(x) for x in v]
        w_rhs = [fstate[g][0] + _dot(a_ak[g], bd_v[g]) for g in groups]
        u = [w_rhs[g] + _dot(pm[g].astype(BF16), bd(w_rhs[g])) for g in groups]
        o = [fstate[g][1] + _dot(a_r[g], jnp.concatenate([bd(u[g]), bd_v[g]], axis=0)) for g in groups]
        for g in groups:
            e_end = jnp.exp(cum_end[g] - cum[g])
            b_end = b[g] * e_end
            k_end = k[g] * e_end
            for q in seqs:
                rq = rows_of[q]
                uv = jnp.concatenate([u[g][rq], v[g][rq]], axis=0).astype(BF16)
                bk = jnp.concatenate([b_end[rq], k_end[rq]], axis=0).astype(BF16)
                keep = jnp.exp(cum_end[g][q * seq:q * seq + 1, :]) if nseq > 1 else jnp.exp(cum_end[g])
                store_state(gids[g], q, states[g][q] * keep + _dot_tn(uv, bk))

        mean = [head_sum(x) * (1.0 / hd) for x in o]
        oc = [o[g] - mean[g] for g in groups]
        var = [head_sum(jnp.square(x)) * (1.0 / hd) for x in oc]
        for g in groups:
            sl = sls[g]
            og = oc[g] * lax.rsqrt(var[g] + RWKV_GN_EPS) * lnw_ref[:, sl] + lnb_ref[:, sl]
            y_ref[:, sl] = ((og + rk_sum[g] * v[g]) * _silu(pg_ref[:, sl])).astype(BF16)

    stages(list(groups))


def _head_mask(gw, hd):
    hr = lax.broadcasted_iota(jnp.int32, (gw, gw), 0) // hd
    hc = lax.broadcasted_iota(jnp.int32, (gw, gw), 1) // hd
    return hr == hc


def _rwkv_prompt_body(*refs):
    ins, (y_ref, so_ref), (c_r, c_k, c_v, c_l, s_scr) = refs[:18], refs[18:20], refs[20:]
    n = pl.program_id(1)
    hd = RWKV_HEAD_DIM
    gw = RWKV_GROUP * hd
    headmask = _head_mask(gw, hd)

    @pl.when(n == 0)
    def _():
        _zero_refs(c_r, c_k, c_v, c_l, s_scr)

    def load_state(g):
        return [s_scr[g]]

    def store_state(g, q, s):
        s_scr[g] = jnp.where(headmask, s, 0.0)

    c = RWKV_CHUNK
    for sub in range(y_ref.shape[0] // c):
        rows = pl.ds(sub * c, c)
        tile_ins = [r.at[rows] for r in ins[:5]] + list(ins[5:])
        _rwkv_tile(*tile_ins, y_ref.at[rows], (c_r, c_k, c_v, c_l), c, None, load_state, store_state)

    @pl.when(n == pl.num_programs(1) - 1)
    def _():
        for g in range(s_scr.shape[0]):
            sg = s_scr[g]
            for j in range(RWKV_GROUP):
                so_ref[0, g * RWKV_GROUP + j] = sg[j * hd:(j + 1) * hd, j * hd:(j + 1) * hd]


def _rwkv_prompt(proj, lora, nb, l, dr, mu_r, mu_k, mu_v, mu_l, w0, w2p, a0, a2p, k_k, k_a, r_k, ln_w, ln_b):
    assert l % RWKV_CHUNK == 0 and RWKV_CHUNK == RWKV_HEAD_DIM
    c = _tile(l, RWKV_STEP_CHUNKS * RWKV_CHUNK, RWKV_CHUNK)
    nchunk = l // c
    lp = lora.shape[1]
    nh = dr // RWKV_HEAD_DIM
    gw = RWKV_GROUP * RWKV_HEAD_DIM
    vec = lambda: pl.BlockSpec((1, dr), lambda b, n: (0, 0))
    sect = lambda s: pl.BlockSpec((c, dr), lambda b, n, s=s: (b * nchunk + n, 4 + s))
    return pl.pallas_call(
        _rwkv_prompt_body,
        grid=(nb, nchunk),
        in_specs=[sect(0), sect(1), sect(2), sect(3),
                  pl.BlockSpec((c, lp), lambda b, n: (b * nchunk + n, 0)),
                  vec(), vec(), vec(),
                  pl.BlockSpec((1, lp), lambda b, n: (0, 0)),
                  vec(),
                  pl.BlockSpec((lp, dr), lambda b, n: (0, 0)),
                  vec(),
                  pl.BlockSpec((lp, dr), lambda b, n: (0, 0)),
                  vec(), vec(), vec(), vec(), vec()],
        out_specs=[pl.BlockSpec((c, dr), lambda b, n: (b * nchunk + n, 0)),
                   pl.BlockSpec((1, nh, RWKV_HEAD_DIM, RWKV_HEAD_DIM), lambda b, n: (b, 0, 0, 0))],
        out_shape=[jax.ShapeDtypeStruct((nb * l, dr), BF16),
                   jax.ShapeDtypeStruct((nb, nh, RWKV_HEAD_DIM, RWKV_HEAD_DIM), F32)],
        scratch_shapes=[pltpu.VMEM((SUBLANES, dr), F32), pltpu.VMEM((SUBLANES, dr), F32),
                        pltpu.VMEM((SUBLANES, dr), F32), pltpu.VMEM((SUBLANES, lp), F32),
                        pltpu.VMEM((dr // gw, gw, gw), F32)],
        compiler_params=_params(("parallel", "arbitrary")),
        name="rwkv_chunked",
    )(proj, proj, proj, proj, lora, mu_r, mu_k, mu_v, mu_l, w0, w2p, a0, a2p, k_k, k_a, r_k, ln_w, ln_b)


def _rwkv_sample_body(*refs, n_tok):
    ins, s0_ref, y_ref, so_ref = refs[:18], refs[18], refs[19], refs[20]
    hd = RWKV_HEAD_DIM
    c = RWKV_CHUNK
    nseq = c // SAMPLE_SLOTS
    zero = jnp.zeros((hd, hd), F32)

    for sub in range(y_ref.shape[0] // c):
        seq0 = sub * nseq

        def load_state(g, seq0=seq0):
            out = []
            per_head = [jnp.swapaxes(s0_ref[g * RWKV_GROUP + j, :, pl.ds(seq0, nseq), :], 0, 1)
                        for j in range(RWKV_GROUP)]
            for q in range(nseq):
                rows = [jnp.concatenate([per_head[j][q] if i == j else zero
                                         for i in range(RWKV_GROUP)], axis=1) for j in range(RWKV_GROUP)]
                out.append(jnp.concatenate(rows, axis=0))
            return out

        def store_state(g, q, s, seq0=seq0):
            for j in range(RWKV_GROUP):
                so_ref[seq0 + q, g * RWKV_GROUP + j] = s[j * hd:(j + 1) * hd, j * hd:(j + 1) * hd]

        rows = pl.ds(sub * c, c)
        tile_ins = [r.at[rows] for r in ins[:5]] + list(ins[5:])
        _rwkv_tile(*tile_ins, y_ref.at[rows], (None, None, None, None), SAMPLE_SLOTS, (1, n_tok),
                   load_state, store_state)


def _rwkv_sample(proj, lora, row0, s0, n_tok, dr, mu_r, mu_k, mu_v, mu_l, w0, w2p, a0, a2p, k_k, k_a,
                 r_k, ln_w, ln_b):
    nh_all, _, bsz, _ = s0.shape
    rows = bsz * SAMPLE_SLOTS
    assert rows % RWKV_CHUNK == 0 and RWKV_CHUNK == RWKV_HEAD_DIM
    c = RWKV_SAMPLE_STEP_TILES * RWKV_CHUNK
    assert rows % c == 0 and row0 % c == 0
    t0 = row0 // c
    nseq = c // SAMPLE_SLOTS
    lp = lora.shape[1]
    hd = RWKV_HEAD_DIM
    dw = _tile(dr, RWKV_SAMPLE_LANE_TILE, RWKV_GROUP * hd)
    nw = dr // dw
    vec = lambda: pl.BlockSpec((1, dw), lambda i, hf: (0, hf))
    sect = lambda s: pl.BlockSpec((c, dw), lambda i, hf, s=s: (t0 + i, (4 + s) * nw + hf))
    lora_w = lambda: pl.BlockSpec((lp, dw), lambda i, hf: (0, hf))
    st_in = pl.BlockSpec((dw // hd, hd, nseq, hd), lambda i, hf: (hf, 0, i, 0))
    st_out = pl.BlockSpec((nseq, dw // hd, hd, hd), lambda i, hf: (i, hf, 0, 0))
    return pl.pallas_call(
        functools.partial(_rwkv_sample_body, n_tok=n_tok),
        grid=(rows // c, nw),
        in_specs=[sect(0), sect(1), sect(2), sect(3),
                  pl.BlockSpec((c, lp), lambda i, hf: (t0 + i, 0)),
                  vec(), vec(), vec(),
                  pl.BlockSpec((1, lp), lambda i, hf: (0, 0)),
                  vec(), lora_w(), vec(), lora_w(),
                  vec(), vec(), vec(), vec(), vec(), st_in],
        out_specs=[pl.BlockSpec((c, dw), lambda i, hf: (i, hf)), st_out],
        out_shape=[jax.ShapeDtypeStruct((rows, dr), BF16),
                   jax.ShapeDtypeStruct((bsz, nh_all, hd, hd), F32)],
        compiler_params=_params(("parallel", "parallel")),
        name="rwkv_chunked_sample",
    )(proj, proj, proj, proj, lora, mu_r, mu_k, mu_v, mu_l, w0, w2p, a0, a2p, k_k, k_a, r_k, ln_w, ln_b, s0)


def _rotary(x, cos, sin):
    half = x.shape[-1] // 2
    x1 = x[:, :half]
    x2 = x[:, half:]
    return jnp.concatenate([x1 * cos - x2 * sin, x1 * sin + x2 * cos], axis=-1)


def _group_norm_rows(o, eps):
    mean = jnp.mean(o, axis=-1, keepdims=True)
    oc = o - mean
    var = jnp.mean(oc * oc, axis=-1, keepdims=True)
    return oc * lax.rsqrt(var + eps)


def _ret_prompt_body(lg_ref, q_ref, k_ref, v_ref, g_ref, cos_ref, sin_ref, y_ref, s_ref, *, c):
    n = pl.program_id(1)
    hd = RET_HEAD_DIM
    heads = range(q_ref.shape[1] // hd)
    cols = [slice(h * hd, (h + 1) * hd) for h in heads]

    @pl.when(n == 0)
    def _():
        s_ref[...] = jnp.zeros_like(s_ref)

    ii = lax.broadcasted_iota(jnp.int32, (c, c), 0)
    jj = lax.broadcasted_iota(jnp.int32, (c, c), 1)
    diff = (ii - jj).astype(F32)
    causal = diff >= 0.0
    dist = jnp.maximum(diff, 0.0)
    ic = lax.broadcasted_iota(jnp.int32, (c, 1), 0).astype(F32)
    lg = [jnp.full((1, 1), lg_ref[h], F32) for h in heads]
    dmask = [jnp.where(causal, jnp.exp(dist * lg[h]), 0.0) for h in heads]
    cross_decay = [jnp.exp((ic + 1.0) * lg[h]) for h in heads]
    key_decay = [jnp.exp((c - 1.0 - ic) * lg[h]) for h in heads]
    chunk_decay = [jnp.exp(float(c) * lg[h]) for h in heads]

    s = [s_ref[0, h] for h in heads]
    for sub in range(q_ref.shape[0] // c):
        rows = slice(sub * c, (sub + 1) * c)
        cos = cos_ref[rows, :]
        sin = sin_ref[rows, :]
        q = [_rotary(q_ref[rows, cs], cos, sin).astype(BF16) for cs in cols]
        k = [_rotary(k_ref[rows, cs], cos, sin) * (hd ** -0.5) for cs in cols]
        vb = [v_ref[rows, cs].astype(BF16) for cs in cols]
        scores = [_dot_nt(q[h], k[h].astype(BF16)) * dmask[h] for h in heads]
        cross = [_dot(q[h], s[h].astype(BF16)) * cross_decay[h] for h in heads]
        o = [_dot(scores[h].astype(BF16), vb[h]) + cross[h] for h in heads]
        s = [s[h] * chunk_decay[h] + _dot_tn((k[h] * key_decay[h]).astype(BF16), vb[h]) for h in heads]
        for h in heads:
            y_ref[rows, cols[h]] = (_group_norm_rows(o[h], GN_EPS) * _silu(g_ref[rows, cols[h]])).astype(BF16)
    for h in heads:
        s_ref[0, h] = s[h]


def _ret_prompt(proj, lg, cos, sin, nb, l, nh):
    hd = RET_HEAD_DIM
    dr = nh * hd
    chunk = math.gcd(l, RET_CHUNK)
    c = _tile(l, RET_STEP_CHUNKS * chunk, chunk)
    nchunk = l // c
    sect = lambda s: pl.BlockSpec((c, dr), lambda b, n, s=s: (b * nchunk + n, s))
    tab = pl.BlockSpec((c, hd // 2), lambda b, n: (n, 0))
    return pl.pallas_call(
        functools.partial(_ret_prompt_body, c=chunk),
        grid=(nb, nchunk),
        in_specs=[pl.BlockSpec(memory_space=pltpu.SMEM), sect(0), sect(1), sect(2), sect(3), tab, tab],
        out_specs=[pl.BlockSpec((c, dr), lambda b, n: (b * nchunk + n, 0)),
                   pl.BlockSpec((1, nh, hd, hd), lambda b, n: (b, 0, 0, 0))],
        out_shape=[jax.ShapeDtypeStruct((nb * l, dr), BF16),
                   jax.ShapeDtypeStruct((nb, nh, hd, hd), F32)],
        compiler_params=_params(("parallel", "arbitrary")),
        name="retention_prompt",
    )(lg, proj, proj, proj, proj, cos, sin)


def _ret_sample_body(lg_ref, q_ref, k_ref, v_ref, g_ref, cos_ref, sin_ref, s0_ref, y_ref, s_ref, *, nh, n_tok):
    hd = RET_HEAD_DIM
    bs = s0_ref.shape[0]
    sl = SAMPLE_SLOTS
    cos = cos_ref[...]
    sin = sin_ref[...]
    slot_r = lax.broadcasted_iota(jnp.int32, (sl, 1), 0)
    tok_r = (slot_r - 1).astype(F32)
    valid_r = (slot_r >= 1) & (slot_r <= n_tok)
    ii = lax.broadcasted_iota(jnp.int32, (sl, sl), 0)
    jj = lax.broadcasted_iota(jnp.int32, (sl, sl), 1)
    diff = (ii - jj).astype(F32)
    pair_ok = (diff >= 0.0) & (jj >= 1) & (jj <= n_tok)
    for h in range(nh):
        lg = jnp.full((1, 1), lg_ref[h], F32)
        dmask = jnp.where(pair_ok, jnp.exp(jnp.maximum(diff, 0.0) * lg), 0.0)
        cross_decay = jnp.exp((tok_r + 1.0) * lg)
        key_decay = jnp.where(valid_r, jnp.exp((n_tok - 1.0 - tok_r) * lg), 0.0)
        chunk_decay = jnp.exp(float(n_tok) * lg)
        for b in range(bs):
            rows = slice(b * sl, (b + 1) * sl)
            cols = slice(h * hd, (h + 1) * hd)
            q = _rotary(q_ref[rows, cols], cos, sin)
            k = _rotary(k_ref[rows, cols], cos, sin) * (hd ** -0.5)
            vb = v_ref[rows, cols].astype(BF16)
            s = s0_ref[b, h]
            qb = q.astype(BF16)
            scores = _dot_nt(qb, k.astype(BF16)) * dmask
            inner = _dot(scores.astype(BF16), vb)
            cross = _dot(qb, s.astype(BF16)) * cross_decay
            kd_t = (k * key_decay).T.astype(BF16)
            s_ref[b, h] = s * chunk_decay + _dot(kd_t, vb)
            o = inner + cross
            y_ref[rows, cols] = (_group_norm_rows(o, GN_EPS) * _silu(g_ref[rows, cols])).astype(BF16)


def _ret_sample(proj, row0, lg, cos, sin, s0, nh, n_tok):
    bsz = s0.shape[0]
    hd = RET_HEAD_DIM
    dr = nh * hd
    bs = _tile(bsz, RET_SAMPLE_SEQ_TILE, 2)
    rows = bs * SAMPLE_SLOTS
    assert row0 % rows == 0
    t0 = row0 // rows
    sect = lambda s: pl.BlockSpec((rows, dr), lambda i, s=s: (t0 + i, s))
    tab = pl.BlockSpec((SAMPLE_SLOTS, hd // 2), lambda i: (0, 0))
    st = pl.BlockSpec((bs, nh, hd, hd), lambda i: (i, 0, 0, 0))
    return pl.pallas_call(
        functools.partial(_ret_sample_body, nh=nh, n_tok=n_tok),
        grid=(bsz // bs,),
        in_specs=[pl.BlockSpec(memory_space=pltpu.SMEM), sect(0), sect(1), sect(2), sect(3), tab, tab, st],
        out_specs=[pl.BlockSpec((rows, dr), lambda i: (i, 0)), st],
        out_shape=[jax.ShapeDtypeStruct((bsz * SAMPLE_SLOTS, dr), BF16),
                   jax.ShapeDtypeStruct(s0.shape, F32)],
        compiler_params=_params(("parallel",)),
        name="retention_sample",
    )(lg, proj, proj, proj, proj, cos, sin, s0)


def _out_core(yr_ref, yw_ref, w1_ref, w2_ref, x, gate, fw):
    y = _dot(yr_ref[...], w1_ref[...]) + _dot(yw_ref[...], w2_ref[...])
    xn = x + gate * y
    ms = jnp.mean(xn * xn, axis=-1, keepdims=True)
    return xn * lax.rsqrt(ms + NORM_EPS) * fw


def _out_prompt_body(yr_ref, yw_ref, w1_ref, w2_ref, x_ref, mod_ref, fw_ref, o_ref):
    o_ref[0] = _out_core(yr_ref, yw_ref, w1_ref, w2_ref, x_ref[0], mod_ref[0][2:3], fw_ref[...])


def _out_prompt(y_ret, y_rw, w, x, mod3, fw):
    b, l, d = x.shape
    dr = w.shape[0] // 2
    tl = _tile(l, OUT_PROJ_ROW_TILE, 16)
    nt = l // tl
    wspec = lambda half: pl.BlockSpec((dr, d), lambda bi, i: (half, 0), pipeline_mode=pl.Buffered(1))
    yspec = pl.BlockSpec((tl, dr), lambda bi, i: (bi * nt + i, 0))
    return pl.pallas_call(
        _out_prompt_body,
        grid=(b, nt),
        in_specs=[yspec, yspec, wspec(0), wspec(1),
                  pl.BlockSpec((1, tl, d), lambda bi, i: (bi, i, 0)),
                  pl.BlockSpec((1, 3, d), lambda bi, i: (bi, 0, 0)),
                  pl.BlockSpec((1, d), lambda bi, i: (0, 0))],
        out_specs=pl.BlockSpec((1, tl, d), lambda bi, i: (bi, i, 0)),
        out_shape=jax.ShapeDtypeStruct((b, l, d), F32),
        compiler_params=_params(("parallel", "parallel")),
        name="out_proj_prompt",
    )(y_ret, y_rw, w, w, x, mod3, fw.reshape(1, d))


def _out_sample_body(yr_ref, yw_ref, w1_ref, w2_ref, x_ref, mod_ref, fw_ref, o_ref):
    bs, sl, d = x_ref.shape
    gate = jnp.broadcast_to(mod_ref[...][:, 2:3, :], (bs, sl, d)).reshape(bs * sl, d)
    x = x_ref[...].reshape(bs * sl, d)
    o = _out_core(yr_ref, yw_ref, w1_ref, w2_ref, x, gate, fw_ref[...])
    o_ref[...] = o.reshape(bs, sl, d)


def _out_sample(y_ret, y_rw, w, x_slots, mod3, fw):
    bsz, sl, d = x_slots.shape
    dr = w.shape[0] // 2
    bs = _tile(bsz, OUT_SAMPLE_SEQ_TILE, SUBLANES)
    wspec = lambda half: pl.BlockSpec((dr, d), lambda i: (half, 0), pipeline_mode=pl.Buffered(1))
    yspec = pl.BlockSpec((bs * sl, dr), lambda i: (i, 0))
    xspec = pl.BlockSpec((bs, sl, d), lambda i: (i, 0, 0))
    return pl.pallas_call(
        _out_sample_body,
        grid=(bsz // bs,),
        in_specs=[yspec, yspec, wspec(0), wspec(1), xspec,
                  pl.BlockSpec((bs, 3, d), lambda i: (i, 0, 0)),
                  pl.BlockSpec((1, d), lambda i: (0, 0))],
        out_specs=xspec,
        out_shape=jax.ShapeDtypeStruct((bsz, sl, d), F32),
        compiler_params=_params(("parallel",)),
        name="out_proj_sample",
    )(y_ret, y_rw, w, w, x_slots, mod3, fw.reshape(1, d))


def _rope_tables(pos, half):
    inv_freq = ROPE_THETA ** (-jnp.arange(half, dtype=F32) / half)
    ang = pos[:, None] * inv_freq[None, :]
    return jnp.cos(ang), jnp.sin(ang)


def kernel(x_prompt, x_sample, c_prompt, c_sample, state_ret, state_rwkv, state_shift, norm_w, w_ada,
           b_ada, w_in, mu_shift, w0_decay, w2_decay, a0, a2, k_k, k_a, r_k, ln_x_w, ln_x_b, w_out,
           final_norm_w):
    depth = w_in.shape[0]
    assert depth == 1, "single-layer trunk"
    bp, lp, d = x_prompt.shape
    bsz, ls, _ = x_sample.shape
    assert ls == 4, "sample path packs 4 tokens into slots 1..4"
    dr = d
    nh_ret = dr // RET_HEAD_DIM
    n_main = 8 * dr
    lora = w2_decay.shape[1]
    lora_pad = -(-2 * lora // LANES) * LANES

    w_t = jnp.swapaxes(w_in[0], 0, 1)
    w_t_lora = jnp.pad(w_t[n_main:], ((0, lora_pad - 2 * lora), (0, 0)))
    w_o = w_out[0].astype(BF16)
    mu = mu_shift[0]
    row = lambda p: p.reshape(1, -1)
    mu_r, mu_k, mu_v = row(mu[0:dr]), row(mu[dr:2 * dr]), row(mu[2 * dr:3 * dr])
    mu_l = jnp.pad(mu[3 * dr:], (0, lora_pad - 2 * lora)).reshape(1, lora_pad)
    w2p = jnp.pad(w2_decay[0], ((0, lora_pad - lora), (0, 0))).astype(BF16)
    a2p = jnp.pad(a2[0], ((lora, lora_pad - 2 * lora), (0, 0))).astype(BF16)
    lg = jnp.log1p(-jnp.exp2(-5.0 - jnp.arange(nh_ret, dtype=F32)))
    rw_params = (mu_r, mu_k, mu_v, mu_l, row(w0_decay[0]), w2p, row(a0[0]), a2p, row(k_k[0]), row(k_a[0]),
                 row(r_k[0]), row(ln_x_w[0]), row(ln_x_b[0]))

    n_c = bp + bsz
    n_c_pad = -(-n_c // SUBLANES) * SUBLANES
    c_all = jnp.pad(jnp.concatenate([c_prompt, c_sample], axis=0), ((0, n_c_pad - n_c), (0, 0)))
    mod3 = _adaln(c_all, w_ada[0], b_ada[0]).reshape(n_c_pad, 3, d)
    mod_p, mod_s = mod3[:bp], mod3[bp:bp + bsz]

    sl = SAMPLE_SLOTS
    row_s = bp * lp
    x_slots = jnp.pad(x_sample, ((0, 0), (1, sl - 1 - ls), (0, 0)))
    h_all, last_p, new_shift_s = _modulate(x_prompt, mod_p, x_slots, state_shift[0], mod_s, norm_w[0])
    new_shift_p = last_p[:, SUBLANES - 1, :]
    proj = _in_proj(h_all, w_t, n_main)
    lora_proj = _in_proj(h_all, w_t_lora, lora_pad, name="in_proj_lora")

    cos_p, sin_p = _rope_tables(jnp.arange(lp, dtype=F32), RET_HEAD_DIM // 2)
    y_ret_p, s_ret_p = _ret_prompt(proj, lg, cos_p, sin_p, bp, lp, nh_ret)
    y_rw_p, s_rw_p = _rwkv_prompt(proj, lora_proj, bp, lp, dr, *rw_params)
    y_prompt = _out_prompt(y_ret_p, y_rw_p, w_o, x_prompt, mod_p, final_norm_w)

    slot_pos = jnp.arange(sl, dtype=F32) - 1.0
    pos_s = jnp.where((slot_pos >= 0) & (slot_pos < ls), float(PAST_LEN) + slot_pos, 0.0)
    cos_s, sin_s = _rope_tables(pos_s, RET_HEAD_DIM // 2)
    y_ret_s, s_ret_s = _ret_sample(proj, row_s, lg, cos_s, sin_s, state_ret[0], nh_ret, ls)
    s0_rw = jnp.transpose(state_rwkv[0], (1, 2, 0, 3))
    y_rw_s, s_rw_s = _rwkv_sample(proj, lora_proj, row_s, s0_rw, ls, dr, *rw_params)
    y_slots = _out_sample(y_ret_s, y_rw_s, w_o, x_slots, mod_s, final_norm_w)
    y_sample = y_slots[:, 1:1 + ls, :]

    return (y_prompt, y_sample, s_ret_p[None], s_rw_p[None], new_shift_p[None],
            s_ret_s[None], s_rw_s[None], new_shift_s[None])
```

```python
import functools
import math

import jax
import jax.numpy as jnp
from jax import lax
from jax.experimental import pallas as pl
from jax.experimental.pallas import tpu as pltpu

F32 = jnp.float32
BF16 = jnp.bfloat16

RET_HEAD_DIM = 256
RWKV_HEAD_DIM = 64
RET_CHUNK = 128
RWKV_CHUNK = 64
RWKV_GROUP = 2
PAST_LEN = 16384
ROPE_THETA = 10000.0
NORM_EPS = 1e-6
GN_EPS = 1e-5
RWKV_GN_EPS = 64e-5
SAMPLE_SLOTS = 8
LANES = 128
SUBLANES = 8
VMEM_LIMIT_BYTES = 56 * 1024 * 1024

ADALN_COL_TILE = 768
MODULATE_ROW_TILE = 512
IN_PROJ_ROW_TILE = 1536
IN_PROJ_COL_TILE = 1024
OUT_PROJ_ROW_TILE = 512
OUT_SAMPLE_SEQ_TILE = 32
RET_SAMPLE_SEQ_TILE = 4
RET_STEP_CHUNKS = 4
RWKV_STEP_CHUNKS = 4
RWKV_SAMPLE_STEP_TILES = 1
RWKV_SAMPLE_LANE_TILE = 2048


def _params(sem):
    return pltpu.CompilerParams(dimension_semantics=sem, vmem_limit_bytes=VMEM_LIMIT_BYTES)


def _tile(n, cap, align):
    if n <= cap:
        return n
    t = (cap // align) * align
    while t >= align:
        if n % t == 0:
            return t
        t -= align
    return n


def _silu(x):
    return x * jax.nn.sigmoid(x)


def _dot(a, b):
    return jnp.dot(a, b, preferred_element_type=F32)


def _dot_nt(a, b):
    return lax.dot_general(a, b, (((1,), (1,)), ((), ())), preferred_element_type=F32)


def _dot_tn(a, b):
    return lax.dot_general(a, b, (((0,), (0,)), ((), ())), preferred_element_type=F32)


def _adaln_body(c_ref, w_ref, b_ref, o_ref):
    s = _silu(c_ref[...]).astype(BF16)
    o_ref[...] = _dot(s, w_ref[...].astype(BF16)) + b_ref[...]


def _adaln(c, w_ada, b_ada):
    rows, d = c.shape
    n = w_ada.shape[1]
    tn = _tile(n, ADALN_COL_TILE, LANES)
    return pl.pallas_call(
        _adaln_body,
        grid=(n // tn,),
        in_specs=[pl.BlockSpec((rows, d), lambda j: (0, 0)),
                  pl.BlockSpec((d, tn), lambda j: (0, j)),
                  pl.BlockSpec((1, tn), lambda j: (0, j))],
        out_specs=pl.BlockSpec((rows, tn), lambda j: (0, j)),
        out_shape=jax.ShapeDtypeStruct((rows, n), F32),
        compiler_params=_params(("parallel",)),
        name="adaln",
    )(c, w_ada, b_ada.reshape(1, n))


def _modulated(x, nw, shift, scale):
    ms = jnp.mean(x * x, axis=-1, keepdims=True)
    return x * lax.rsqrt(ms + NORM_EPS) * (nw * (1.0 + scale)) + shift


def _modulate_body(xp_ref, modp_ref, xs_ref, prev_ref, mods_ref, nw_ref, h_ref, last_ref, new_ref, *, n_prompt):
    s = pl.program_id(0)

    @pl.when(s < n_prompt)
    def _():
        m = modp_ref[0]
        h = _modulated(xp_ref[0], nw_ref[...], m[0:1], m[1:2])
        h_ref[...] = h.astype(BF16)
        tl = h.shape[0]
        last_ref[0] = h[tl - SUBLANES:tl]

    @pl.when(s >= n_prompt)
    def _():
        x = xs_ref[...]
        m = mods_ref[...]
        h = _modulated(x, nw_ref[...], m[:, 0:1, :], m[:, 1:2, :])
        slot = lax.broadcasted_iota(jnp.int32, h.shape, 1)
        full = jnp.where(slot == 0, prev_ref[...][:, None, :], h)
        bs, sl, d = x.shape
        h_ref[...] = full.reshape(bs * sl, d).astype(BF16)
        new_ref[...] = h[:, 4, :]


def _modulate(x_prompt, mod_p, x_slots, prev, mod_s, nw):
    b, l, d = x_prompt.shape
    bsz, sl, _ = x_slots.shape
    tl = _tile(l, min(MODULATE_ROW_TILE, bsz * sl), 16)
    nt = l // tl
    bs = tl // sl
    assert bsz % bs == 0 and bs % SUBLANES == 0
    n_prompt = b * nt
    n_sample = bsz // bs
    samp = lambda s: jnp.maximum(s - n_prompt, 0)
    pb = lambda s: jnp.minimum(s // nt, b - 1)
    return pl.pallas_call(
        functools.partial(_modulate_body, n_prompt=n_prompt),
        grid=(n_prompt + n_sample,),
        in_specs=[pl.BlockSpec((1, tl, d), lambda s: (pb(s), jnp.where(s < n_prompt, s % nt, nt - 1), 0)),
                  pl.BlockSpec((1, 3, d), lambda s: (pb(s), 0, 0)),
                  pl.BlockSpec((bs, sl, d), lambda s: (samp(s), 0, 0)),
                  pl.BlockSpec((bs, d), lambda s: (samp(s), 0)),
                  pl.BlockSpec((bs, 3, d), lambda s: (samp(s), 0, 0)),
                  pl.BlockSpec((1, d), lambda s: (0, 0))],
        out_specs=[pl.BlockSpec((tl, d), lambda s: (s, 0)),
                   pl.BlockSpec((1, SUBLANES, d), lambda s: (pb(s), 0, 0)),
                   pl.BlockSpec((bs, d), lambda s: (samp(s), 0))],
        out_shape=[jax.ShapeDtypeStruct((b * l + bsz * sl, d), BF16),
                   jax.ShapeDtypeStruct((b, SUBLANES, d), F32),
                   jax.ShapeDtypeStruct((bsz, d), F32)],
        compiler_params=_params(("arbitrary",)),
        name="modulate",
    )(x_prompt, mod_p, x_slots, prev, mod_s, nw.reshape(1, d))


def _in_proj_body(x_ref, wt_ref, o_ref, w_scr):
    @pl.when(pl.program_id(1) == 0)
    def _():
        w_scr[...] = wt_ref[...].astype(BF16)

    o_ref[...] = _dot_nt(x_ref[...], w_scr[...])


def _in_proj(x, wt, n, name="in_proj"):
    m, k = x.shape
    tm = _tile(m, IN_PROJ_ROW_TILE, 16)
    tn = _tile(n, IN_PROJ_COL_TILE, LANES)
    return pl.pallas_call(
        _in_proj_body,
        grid=(n // tn, m // tm),
        in_specs=[pl.BlockSpec((tm, k), lambda j, i: (i, 0)),
                  pl.BlockSpec((tn, k), lambda j, i: (j, 0))],
        out_specs=pl.BlockSpec((tm, tn), lambda j, i: (i, j)),
        out_shape=jax.ShapeDtypeStruct((m, n), F32),
        scratch_shapes=[pltpu.VMEM((tn, k), BF16)],
        compiler_params=_params(("arbitrary", "arbitrary")),
        name=name,
    )(x, wt)


def _token_shift(cur_ref, carry_ref, mu_ref, cols):
    cur = cur_ref[:, cols]
    tl = cur.shape[0]
    prev = pltpu.roll(cur, 1, 0)
    if carry_ref is not None:
        row = lax.broadcasted_iota(jnp.int32, cur.shape, 0)
        prev = jnp.where(row == 0, carry_ref[0:1, cols], prev)
        carry_ref[0:1, cols] = cur[tl - 1:tl, :]
    return cur + (prev - cur) * mu_ref[:, cols]


def _decay_and_rate(lo_tanh_b, lo_b, w0_ref, w2_ref, a0_ref, a2_ref, cols):
    dec = _dot(lo_tanh_b, w2_ref[:, cols])
    logw = -math.exp(-0.5) * jax.nn.sigmoid(w0_ref[:, cols] + dec)
    a = jax.nn.sigmoid(a0_ref[:, cols] + _dot(lo_b, a2_ref[:, cols]))
    return logw, a


def _zero_refs(*refs):
    for r in refs:
        r[...] = jnp.zeros_like(r)


def _split3(x):
    hi = x.astype(BF16)
    r1 = x - hi.astype(F32)
    mid = r1.astype(BF16)
    lo = (r1 - mid.astype(F32)).astype(BF16)
    return hi, mid, lo


def _dot_split3(m, parts):
    return _dot(m, parts[0]) + _dot(m, parts[1]) + _dot(m, parts[2])


def _rwkv_tile(pr_ref, pk_ref, pv_ref, pg_ref, pl_ref, mu_r, mu_k, mu_v, mu_l, w0_ref, w2_ref,
               a0_ref, a2_ref, kk_ref, ka_ref, rk_ref, lnw_ref, lnb_ref, y_ref,
               carries, seq, valid_slots, load_state, store_state):
    c = pr_ref.shape[0]
    hd = RWKV_HEAD_DIM
    gw = RWKV_GROUP * hd
    groups = range(pr_ref.shape[1] // gw)
    nseq = c // seq
    seqs = range(nseq)
    rows_of = [slice(q * seq, (q + 1) * seq) for q in seqs]
    c_r, c_k, c_v, c_l = carries

    lo = _token_shift(pl_ref, c_l, mu_l, slice(None))
    lo_b = lo.astype(BF16)
    lo_tanh_b = jnp.tanh(lo).astype(BF16)
    if valid_slots is not None:
        slot = lax.broadcasted_iota(jnp.int32, (c, 1), 0) % seq
        valid = (slot >= valid_slots[0]) & (slot <= valid_slots[1])

    ti = lax.broadcasted_iota(jnp.int32, (c, c), 0)
    tj = lax.broadcasted_iota(jnp.int32, (c, c), 1)
    same_seq = (ti // seq) == (tj // seq)
    tri_b = ((ti >= tj) & same_seq).astype(BF16)
    same_b = same_seq.astype(BF16)

    hr = lax.broadcasted_iota(jnp.int32, (gw, gw), 0) // hd
    hc = lax.broadcasted_iota(jnp.int32, (gw, gw), 1) // hd
    headmask = hr == hc
    tok = lax.broadcasted_iota(jnp.int32, (c, gw), 0)
    src = lax.broadcasted_iota(jnp.int32, (c, gw), 1) % hd
    same = (tok // seq) == (src // seq)
    strict = (tok > src) & same
    incl = (tok >= src) & same

    def bd(x):
        t = jnp.concatenate([x] * RWKV_GROUP, axis=0)
        return jnp.where(headmask, t, 0.0).astype(BF16)

    lane_head = lax.broadcasted_iota(jnp.int32, (c, gw), 1) // hd

    def head_sum(x):
        out = jnp.zeros_like(x)
        for j in range(RWKV_GROUP):
            mine = lane_head == j
            out = jnp.where(mine, jnp.sum(jnp.where(mine, x, 0.0), axis=-1, keepdims=True), out)
        return out

    def stages(gids):
        groups = range(len(gids))
        sls = [slice(g * gw, (g + 1) * gw) for g in gids]
        r = [_token_shift(pr_ref, c_r, mu_r, sl) for sl in sls]
        kw = [_token_shift(pk_ref, c_k, mu_k, sl) for sl in sls]
        v = [_token_shift(pv_ref, c_v, mu_v, sl) for sl in sls]
        lw_a = [_decay_and_rate(lo_tanh_b, lo_b, w0_ref, w2_ref, a0_ref, a2_ref, sl) for sl in sls]
        logw = [x[0] for x in lw_a]
        rate = [x[1] for x in lw_a]
        k = [kw[g] * (1.0 + (rate[g] - 1.0) * ka_ref[:, sls[g]]) for g in groups]
        kkr = [kw[g] * kk_ref[:, sls[g]] for g in groups]
        if valid_slots is not None:
            logw = [jnp.where(valid, x, 0.0) for x in logw]
            kkr = [jnp.where(valid, x, 0.0) for x in kkr]
            k = [jnp.where(valid, x, 0.0) for x in k]

        parts = [_split3(x) for x in logw]
        cum = [_dot_split3(tri_b, p) for p in parts]
        if nseq > 1:
            cum_end = [_dot_split3(same_b, p) for p in parts]
        else:
            cum_end = [cm[c - 1:c, :] for cm in cum]

        kk_sq = [head_sum(jnp.square(x)) for x in kkr]
        rk_sum = [head_sum(r[g] * k[g] * rk_ref[:, sls[g]]) for g in groups]
        kkn = [kkr[g] * lax.rsqrt(jnp.maximum(kk_sq[g], 1e-24)) for g in groups]
        b = [kkn[g] * rate[g] for g in groups]
        e_inv = [jnp.exp(-cum[g]) for g in groups]
        at = [-kkn[g] * jnp.exp(cum[g] - logw[g]) for g in groups]
        rt = [r[g] * jnp.exp(cum[g]) for g in groups]
        ar = [jnp.concatenate([at[g], rt[g]], axis=0).astype(BF16) for g in groups]
        states = [load_state(g) for g in gids]

        def from_state(g):
            if nseq == 1:
                fs = _dot_nt(ar[g], states[g][0].astype(BF16))
                return fs[:c], fs[c:]
            fa, fr = [], []
            for q in seqs:
                arq = jnp.concatenate([at[g][rows_of[q]], rt[g][rows_of[q]]], axis=0).astype(BF16)
                fs = _dot_nt(arq, states[g][q].astype(BF16))
                fa.append(fs[:seq])
                fr.append(fs[seq:])
            return jnp.concatenate(fa, axis=0), jnp.concatenate(fr, axis=0)

        fstate = [from_state(g) for g in groups]
        s_b = [_dot_nt(ar[g], bd(b[g] * e_inv[g])) for g in groups]
        s_k = [_dot_nt(ar[g], bd(k[g] * e_inv[g])) for g in groups]
        a_ab = [jnp.where(strict, x[:c], 0.0) for x in s_b]
        a_ak = [jnp.where(strict, x[:c], 0.0).astype(BF16) for x in s_k]
        a_r = [jnp.concatenate([jnp.where(incl, s_b[g][c:], 0.0), jnp.where(incl, s_k[g][c:], 0.0)],
                               axis=1).astype(BF16) for g in groups]

        steps = max(1, int(math.ceil(math.log2(seq))))
        pw = a_ab
        pm = a_ab
        pw_next = [_dot(pw[g].astype(BF16), bd(pw[g])) for g in groups] if steps > 1 else None
        for s in range(1, steps):
            pw = pw_next
            if s < steps - 1:
                res = [_dot(jnp.concatenate([pw[g], pm[g]], axis=0).astype(BF16), bd(pw[g])) for g in groups]
                pw_next = [x[:c] for x in res]
                pm = [pm[g] + pw[g] + res[g][c:] for g in groups]
            else:
                pm = [pm[g] + pw[g] + _dot(pm[g].astype(BF16), bd(pw[g])) for g in groups]

        bd_v = [bd(x) for x in v]
        w_rhs = [fstate[g][0] + _dot(a_ak[g], bd_v[g]) for g in groups]
        u = [w_rhs[g] + _dot(pm[g].astype(BF16), bd(w_rhs[g])) for g in groups]
        o = [fstate[g][1] + _dot(a_r[g], jnp.concatenate([bd(u[g]), bd_v[g]], axis=0)) for g in groups]
        for g in groups:
            e_end = jnp.exp(cum_end[g] - cum[g])
            b_end = b[g] * e_end
            k_end = k[g] * e_end
            for q in seqs:
                rq = rows_of[q]
                uv = jnp.concatenate([u[g][rq], v[g][rq]], axis=0).astype(BF16)
                bk = jnp.concatenate([b_end[rq], k_end[rq]], axis=0).astype(BF16)
                keep = jnp.exp(cum_end[g][q * seq:q * seq + 1, :]) if nseq > 1 else jnp.exp(cum_end[g])
                store_state(gids[g], q, states[g][q] * keep + _dot_tn(uv, bk))

        mean = [head_sum(x) * (1.0 / hd) for x in o]
        oc = [o[g] - mean[g] for g in groups]
        var = [head_sum(jnp.square(x)) * (1.0 / hd) for x in oc]
        for g in groups:
            sl = sls[g]
            og = oc[g] * lax.rsqrt(var[g] + RWKV_GN_EPS) * lnw_ref[:, sl] + lnb_ref[:, sl]
            y_ref[:, sl] = ((og + rk_sum[g] * v[g]) * _silu(pg_ref[:, sl])).astype(BF16)

    stages(list(groups))


def _head_mask(gw, hd):
    hr = lax.broadcasted_iota(jnp.int32, (gw, gw), 0) // hd
    hc = lax.broadcasted_iota(jnp.int32, (gw, gw), 1) // hd
    return hr == hc


def _rwkv_prompt_body(*refs):
    ins, (y_ref, so_ref), (c_r, c_k, c_v, c_l, s_scr) = refs[:18], refs[18:20], refs[20:]
    n = pl.program_id(1)
    hd = RWKV_HEAD_DIM
    gw = RWKV_GROUP * hd
    headmask = _head_mask(gw, hd)

    @pl.when(n == 0)
    def _():
        _zero_refs(c_r, c_k, c_v, c_l, s_scr)

    def load_state(g):
        return [s_scr[g]]

    def store_state(g, q, s):
        s_scr[g] = jnp.where(headmask, s, 0.0)

    c = RWKV_CHUNK
    for sub in range(y_ref.shape[0] // c):
        rows = pl.ds(sub * c, c)
        tile_ins = [r.at[rows] for r in ins[:5]] + list(ins[5:])
        _rwkv_tile(*tile_ins, y_ref.at[rows], (c_r, c_k, c_v, c_l), c, None, load_state, store_state)

    @pl.when(n == pl.num_programs(1) - 1)
    def _():
        for g in range(s_scr.shape[0]):
            sg = s_scr[g]
            for j in range(RWKV_GROUP):
                so_ref[0, g * RWKV_GROUP + j] = sg[j * hd:(j + 1) * hd, j * hd:(j + 1) * hd]


def _rwkv_prompt(proj, lora, nb, l, dr, mu_r, mu_k, mu_v, mu_l, w0, w2p, a0, a2p, k_k, k_a, r_k, ln_w, ln_b):
    assert l % RWKV_CHUNK == 0 and RWKV_CHUNK == RWKV_HEAD_DIM
    c = _tile(l, RWKV_STEP_CHUNKS * RWKV_CHUNK, RWKV_CHUNK)
    nchunk = l // c
    lp = lora.shape[1]
    nh = dr // RWKV_HEAD_DIM
    gw = RWKV_GROUP * RWKV_HEAD_DIM
    vec = lambda: pl.BlockSpec((1, dr), lambda b, n: (0, 0))
    sect = lambda s: pl.BlockSpec((c, dr), lambda b, n, s=s: (b * nchunk + n, 4 + s))
    return pl.pallas_call(
        _rwkv_prompt_body,
        grid=(nb, nchunk),
        in_specs=[sect(0), sect(1), sect(2), sect(3),
                  pl.BlockSpec((c, lp), lambda b, n: (b * nchunk + n, 0)),
                  vec(), vec(), vec(),
                  pl.BlockSpec((1, lp), lambda b, n: (0, 0)),
                  vec(),
                  pl.BlockSpec((lp, dr), lambda b, n: (0, 0)),
                  vec(),
                  pl.BlockSpec((lp, dr), lambda b, n: (0, 0)),
                  vec(), vec(), vec(), vec(), vec()],
        out_specs=[pl.BlockSpec((c, dr), lambda b, n: (b * nchunk + n, 0)),
                   pl.BlockSpec((1, nh, RWKV_HEAD_DIM, RWKV_HEAD_DIM), lambda b, n: (b, 0, 0, 0))],
        out_shape=[jax.ShapeDtypeStruct((nb * l, dr), BF16),
                   jax.ShapeDtypeStruct((nb, nh, RWKV_HEAD_DIM, RWKV_HEAD_DIM), F32)],
        scratch_shapes=[pltpu.VMEM((SUBLANES, dr), F32), pltpu.VMEM((SUBLANES, dr), F32),
                        pltpu.VMEM((SUBLANES, dr), F32), pltpu.VMEM((SUBLANES, lp), F32),
                        pltpu.VMEM((dr // gw, gw, gw), F32)],
        compiler_params=_params(("parallel", "arbitrary")),
        name="rwkv_chunked",
    )(proj, proj, proj, proj, lora, mu_r, mu_k, mu_v, mu_l, w0, w2p, a0, a2p, k_k, k_a, r_k, ln_w, ln_b)


def _rwkv_sample_body(*refs, n_tok):
    ins, s0_ref, y_ref, so_ref = refs[:18], refs[18], refs[19], refs[20]
    hd = RWKV_HEAD_DIM
    c = RWKV_CHUNK
    nseq = c // SAMPLE_SLOTS
    zero = jnp.zeros((hd, hd), F32)

    for sub in range(y_ref.shape[0] // c):
        seq0 = sub * nseq

        def load_state(g, seq0=seq0):
            out = []
            per_head = [jnp.swapaxes(s0_ref[g * RWKV_GROUP + j, :, pl.ds(seq0, nseq), :], 0, 1)
                        for j in range(RWKV_GROUP)]
            for q in range(nseq):
                rows = [jnp.concatenate([per_head[j][q] if i == j else zero
                                         for i in range(RWKV_GROUP)], axis=1) for j in range(RWKV_GROUP)]
                out.append(jnp.concatenate(rows, axis=0))
            return out

        def store_state(g, q, s, seq0=seq0):
            for j in range(RWKV_GROUP):
                so_ref[seq0 + q, g * RWKV_GROUP + j] = s[j * hd:(j + 1) * hd, j * hd:(j + 1) * hd]

        rows = pl.ds(sub * c, c)
        tile_ins = [r.at[rows] for r in ins[:5]] + list(ins[5:])
        _rwkv_tile(*tile_ins, y_ref.at[rows], (None, None, None, None), SAMPLE_SLOTS, (1, n_tok),
                   load_state, store_state)


def _rwkv_sample(proj, lora, row0, s0, n_tok, dr, mu_r, mu_k, mu_v, mu_l, w0, w2p, a0, a2p, k_k, k_a,
                 r_k, ln_w, ln_b):
    nh_all, _, bsz, _ = s0.shape
    rows = bsz * SAMPLE_SLOTS
    assert rows % RWKV_CHUNK == 0 and RWKV_CHUNK == RWKV_HEAD_DIM
    c = RWKV_SAMPLE_STEP_TILES * RWKV_CHUNK
    assert rows % c == 0 and row0 % c == 0
    t0 = row0 // c
    nseq = c // SAMPLE_SLOTS
    lp = lora.shape[1]
    hd = RWKV_HEAD_DIM
    dw = _tile(dr, RWKV_SAMPLE_LANE_TILE, RWKV_GROUP * hd)
    nw = dr // dw
    vec = lambda: pl.BlockSpec((1, dw), lambda i, hf: (0, hf))
    sect = lambda s: pl.BlockSpec((c, dw), lambda i, hf, s=s: (t0 + i, (4 + s) * nw + hf))
    lora_w = lambda: pl.BlockSpec((lp, dw), lambda i, hf: (0, hf))
    st_in = pl.BlockSpec((dw // hd, hd, nseq, hd), lambda i, hf: (hf, 0, i, 0))
    st_out = pl.BlockSpec((nseq, dw // hd, hd, hd), lambda i, hf: (i, hf, 0, 0))
    return pl.pallas_call(
        functools.partial(_rwkv_sample_body, n_tok=n_tok),
        grid=(rows // c, nw),
        in_specs=[sect(0), sect(1), sect(2), sect(3),
                  pl.BlockSpec((c, lp), lambda i, hf: (t0 + i, 0)),
                  vec(), vec(), vec(),
                  pl.BlockSpec((1, lp), lambda i, hf: (0, 0)),
                  vec(), lora_w(), vec(), lora_w(),
                  vec(), vec(), vec(), vec(), vec(), st_in],
        out_specs=[pl.BlockSpec((c, dw), lambda i, hf: (i, hf)), st_out],
        out_shape=[jax.ShapeDtypeStruct((rows, dr), BF16),
                   jax.ShapeDtypeStruct((bsz, nh_all, hd, hd), F32)],
        compiler_params=_params(("parallel", "parallel")),
        name="rwkv_chunked_sample",
    )(proj, proj, proj, proj, lora, mu_r, mu_k, mu_v, mu_l, w0, w2p, a0, a2p, k_k, k_a, r_k, ln_w, ln_b, s0)


def _rotary(x, cos, sin):
    half = x.shape[-1] // 2
    x1 = x[:, :half]
    x2 = x[:, half:]
    return jnp.concatenate([x1 * cos - x2 * sin, x1 * sin + x2 * cos], axis=-1)


def _group_norm_rows(o, eps):
    mean = jnp.mean(o, axis=-1, keepdims=True)
    oc = o - mean
    var = jnp.mean(oc * oc, axis=-1, keepdims=True)
    return oc * lax.rsqrt(var + eps)


def _ret_prompt_body(lg_ref, q_ref, k_ref, v_ref, g_ref, cos_ref, sin_ref, y_ref, s_ref, *, c):
    n = pl.program_id(1)
    hd = RET_HEAD_DIM
    heads = range(q_ref.shape[1] // hd)
    cols = [slice(h * hd, (h + 1) * hd) for h in heads]

    @pl.when(n == 0)
    def _():
        s_ref[...] = jnp.zeros_like(s_ref)

    ii = lax.broadcasted_iota(jnp.int32, (c, c), 0)
    jj = lax.broadcasted_iota(jnp.int32, (c, c), 1)
    diff = (ii - jj).astype(F32)
    causal = diff >= 0.0
    dist = jnp.maximum(diff, 0.0)
    ic = lax.broadcasted_iota(jnp.int32, (c, 1), 0).astype(F32)
    lg = [jnp.full((1, 1), lg_ref[h], F32) for h in heads]
    dmask = [jnp.where(causal, jnp.exp(dist * lg[h]), 0.0) for h in heads]
    cross_decay = [jnp.exp((ic + 1.0) * lg[h]) for h in heads]
    key_decay = [jnp.exp((c - 1.0 - ic) * lg[h]) for h in heads]
    chunk_decay = [jnp.exp(float(c) * lg[h]) for h in heads]

    s = [s_ref[0, h] for h in heads]
    for sub in range(q_ref.shape[0] // c):
        rows = slice(sub * c, (sub + 1) * c)
        cos = cos_ref[rows, :]
        sin = sin_ref[rows, :]
        q = [_rotary(q_ref[rows, cs], cos, sin).astype(BF16) for cs in cols]
        k = [_rotary(k_ref[rows, cs], cos, sin) * (hd ** -0.5) for cs in cols]
        vb = [v_ref[rows, cs].astype(BF16) for cs in cols]
        scores = [_dot_nt(q[h], k[h].astype(BF16)) * dmask[h] for h in heads]
        cross = [_dot(q[h], s[h].astype(BF16)) * cross_decay[h] for h in heads]
        o = [_dot(scores[h].astype(BF16), vb[h]) + cross[h] for h in heads]
        s = [s[h] * chunk_decay[h] + _dot_tn((k[h] * key_decay[h]).astype(BF16), vb[h]) for h in heads]
        for h in heads:
            y_ref[rows, cols[h]] = (_group_norm_rows(o[h], GN_EPS) * _silu(g_ref[rows, cols[h]])).astype(BF16)
    for h in heads:
        s_ref[0, h] = s[h]


def _ret_prompt(proj, lg, cos, sin, nb, l, nh):
    hd = RET_HEAD_DIM
    dr = nh * hd
    chunk = math.gcd(l, RET_CHUNK)
    c = _tile(l, RET_STEP_CHUNKS * chunk, chunk)
    nchunk = l // c
    sect = lambda s: pl.BlockSpec((c, dr), lambda b, n, s=s: (b * nchunk + n, s))
    tab = pl.BlockSpec((c, hd // 2), lambda b, n: (n, 0))
    return pl.pallas_call(
        functools.partial(_ret_prompt_body, c=chunk),
        grid=(nb, nchunk),
        in_specs=[pl.BlockSpec(memory_space=pltpu.SMEM), sect(0), sect(1), sect(2), sect(3), tab, tab],
        out_specs=[pl.BlockSpec((c, dr), lambda b, n: (b * nchunk + n, 0)),
                   pl.BlockSpec((1, nh, hd, hd), lambda b, n: (b, 0, 0, 0))],
        out_shape=[jax.ShapeDtypeStruct((nb * l, dr), BF16),
                   jax.ShapeDtypeStruct((nb, nh, hd, hd), F32)],
        compiler_params=_params(("parallel", "arbitrary")),
        name="retention_prompt",
    )(lg, proj, proj, proj, proj, cos, sin)


def _ret_sample_body(lg_ref, q_ref, k_ref, v_ref, g_ref, cos_ref, sin_ref, s0_ref, y_ref, s_ref, *, nh, n_tok):
    hd = RET_HEAD_DIM
    bs = s0_ref.shape[0]
    sl = SAMPLE_SLOTS
    cos = cos_ref[...]
    sin = sin_ref[...]
    slot_r = lax.broadcasted_iota(jnp.int32, (sl, 1), 0)
    tok_r = (slot_r - 1).astype(F32)
    valid_r = (slot_r >= 1) & (slot_r <= n_tok)
    ii = lax.broadcasted_iota(jnp.int32, (sl, sl), 0)
    jj = lax.broadcasted_iota(jnp.int32, (sl, sl), 1)
    diff = (ii - jj).astype(F32)
    pair_ok = (diff >= 0.0) & (jj >= 1) & (jj <= n_tok)
    for h in range(nh):
        lg = jnp.full((1, 1), lg_ref[h], F32)
        dmask = jnp.where(pair_ok, jnp.exp(jnp.maximum(diff, 0.0) * lg), 0.0)
        cross_decay = jnp.exp((tok_r + 1.0) * lg)
        key_decay = jnp.where(valid_r, jnp.exp((n_tok - 1.0 - tok_r) * lg), 0.0)
        chunk_decay = jnp.exp(float(n_tok) * lg)
        for b in range(bs):
            rows = slice(b * sl, (b + 1) * sl)
            cols = slice(h * hd, (h + 1) * hd)
            q = _rotary(q_ref[rows, cols], cos, sin)
            k = _rotary(k_ref[rows, cols], cos, sin) * (hd ** -0.5)
            vb = v_ref[rows, cols].astype(BF16)
            s = s0_ref[b, h]
            qb = q.astype(BF16)
            scores = _dot_nt(qb, k.astype(BF16)) * dmask
            inner = _dot(scores.astype(BF16), vb)
            cross = _dot(qb, s.astype(BF16)) * cross_decay
            kd_t = (k * key_decay).T.astype(BF16)
            s_ref[b, h] = s * chunk_decay + _dot(kd_t, vb)
            o = inner + cross
            y_ref[rows, cols] = (_group_norm_rows(o, GN_EPS) * _silu(g_ref[rows, cols])).astype(BF16)


def _ret_sample(proj, row0, lg, cos, sin, s0, nh, n_tok):
    bsz = s0.shape[0]
    hd = RET_HEAD_DIM
    dr = nh * hd
    bs = _tile(bsz, RET_SAMPLE_SEQ_TILE, 2)
    rows = bs * SAMPLE_SLOTS
    assert row0 % rows == 0
    t0 = row0 // rows
    sect = lambda s: pl.BlockSpec((rows, dr), lambda i, s=s: (t0 + i, s))
    tab = pl.BlockSpec((SAMPLE_SLOTS, hd // 2), lambda i: (0, 0))
    st = pl.BlockSpec((bs, nh, hd, hd), lambda i: (i, 0, 0, 0))
    return pl.pallas_call(
        functools.partial(_ret_sample_body, nh=nh, n_tok=n_tok),
        grid=(bsz // bs,),
        in_specs=[pl.BlockSpec(memory_space=pltpu.SMEM), sect(0), sect(1), sect(2), sect(3), tab, tab, st],
        out_specs=[pl.BlockSpec((rows, dr), lambda i: (i, 0)), st],
        out_shape=[jax.ShapeDtypeStruct((bsz * SAMPLE_SLOTS, dr), BF16),
                   jax.ShapeDtypeStruct(s0.shape, F32)],
        compiler_params=_params(("parallel",)),
        name="retention_sample",
    )(lg, proj, proj, proj, proj, cos, sin, s0)


def _out_core(yr_ref, yw_ref, w1_ref, w2_ref, x, gate, fw):
    y = _dot(yr_ref[...], w1_ref[...]) + _dot(yw_ref[...], w2_ref[...])
    xn = x + gate * y
    ms = jnp.mean(xn * xn, axis=-1, keepdims=True)
    return xn * lax.rsqrt(ms + NORM_EPS) * fw


def _out_prompt_body(yr_ref, yw_ref, w1_ref, w2_ref, x_ref, mod_ref, fw_ref, o_ref):
    o_ref[0] = _out_core(yr_ref, yw_ref, w1_ref, w2_ref, x_ref[0], mod_ref[0][2:3], fw_ref[...])


def _out_prompt(y_ret, y_rw, w, x, mod3, fw):
    b, l, d = x.shape
    dr = w.shape[0] // 2
    tl = _tile(l, OUT_PROJ_ROW_TILE, 16)
    nt = l // tl
    wspec = lambda half: pl.BlockSpec((dr, d), lambda bi, i: (half, 0), pipeline_mode=pl.Buffered(1))
    yspec = pl.BlockSpec((tl, dr), lambda bi, i: (bi * nt + i, 0))
    return pl.pallas_call(
        _out_prompt_body,
        grid=(b, nt),
        in_specs=[yspec, yspec, wspec(0), wspec(1),
                  pl.BlockSpec((1, tl, d), lambda bi, i: (bi, i, 0)),
                  pl.BlockSpec((1, 3, d), lambda bi, i: (bi, 0, 0)),
                  pl.BlockSpec((1, d), lambda bi, i: (0, 0))],
        out_specs=pl.BlockSpec((1, tl, d), lambda bi, i: (bi, i, 0)),
        out_shape=jax.ShapeDtypeStruct((b, l, d), F32),
        compiler_params=_params(("parallel", "parallel")),
        name="out_proj_prompt",
    )(y_ret, y_rw, w, w, x, mod3, fw.reshape(1, d))


def _out_sample_body(yr_ref, yw_ref, w1_ref, w2_ref, x_ref, mod_ref, fw_ref, o_ref):
    bs, sl, d = x_ref.shape
    gate = jnp.broadcast_to(mod_ref[...][:, 2:3, :], (bs, sl, d)).reshape(bs * sl, d)
    x = x_ref[...].reshape(bs * sl, d)
    o = _out_core(yr_ref, yw_ref, w1_ref, w2_ref, x, gate, fw_ref[...])
    o_ref[...] = o.reshape(bs, sl, d)


def _out_sample(y_ret, y_rw, w, x_slots, mod3, fw):
    bsz, sl, d = x_slots.shape
    dr = w.shape[0] // 2
    bs = _tile(bsz, OUT_SAMPLE_SEQ_TILE, SUBLANES)
    wspec = lambda half: pl.BlockSpec((dr, d), lambda i: (half, 0), pipeline_mode=pl.Buffered(1))
    yspec = pl.BlockSpec((bs * sl, dr), lambda i: (i, 0))
    xspec = pl.BlockSpec((bs, sl, d), lambda i: (i, 0, 0))
    return pl.pallas_call(
        _out_sample_body,
        grid=(bsz // bs,),
        in_specs=[yspec, yspec, wspec(0), wspec(1), xspec,
                  pl.BlockSpec((bs, 3, d), lambda i: (i, 0, 0)),
                  pl.BlockSpec((1, d), lambda i: (0, 0))],
        out_specs=xspec,
        out_shape=jax.ShapeDtypeStruct((bsz, sl, d), F32),
        compiler_params=_params(("parallel",)),
        name="out_proj_sample",
    )(y_ret, y_rw, w, w, x_slots, mod3, fw.reshape(1, d))


def _rope_tables(pos, half):
    inv_freq = ROPE_THETA ** (-jnp.arange(half, dtype=F32) / half)
    ang = pos[:, None] * inv_freq[None, :]
    return jnp.cos(ang), jnp.sin(ang)


def kernel(x_prompt, x_sample, c_prompt, c_sample, state_ret, state_rwkv, state_shift, norm_w, w_ada,
           b_ada, w_in, mu_shift, w0_decay, w2_decay, a0, a2, k_k, k_a, r_k, ln_x_w, ln_x_b, w_out,
           final_norm_w):
    depth = w_in.shape[0]
    assert depth == 1, "single-layer trunk"
    bp, lp, d = x_prompt.shape
    bsz, ls, _ = x_sample.shape
    assert ls == 4, "sample path packs 4 tokens into slots 1..4"
    dr = d
    nh_ret = dr // RET_HEAD_DIM
    n_main = 8 * dr
    lora = w2_decay.shape[1]
    lora_pad = -(-2 * lora // LANES) * LANES

    w_t = jnp.swapaxes(w_in[0], 0, 1)
    w_t_lora = jnp.pad(w_t[n_main:], ((0, lora_pad - 2 * lora), (0, 0)))
    w_o = w_out[0].astype(BF16)
    mu = mu_shift[0]
    row = lambda p: p.reshape(1, -1)
    mu_r, mu_k, mu_v = row(mu[0:dr]), row(mu[dr:2 * dr]), row(mu[2 * dr:3 * dr])
    mu_l = jnp.pad(mu[3 * dr:], (0, lora_pad - 2 * lora)).reshape(1, lora_pad)
    w2p = jnp.pad(w2_decay[0], ((0, lora_pad - lora), (0, 0))).astype(BF16)
    a2p = jnp.pad(a2[0], ((lora, lora_pad - 2 * lora), (0, 0))).astype(BF16)
    lg = jnp.log1p(-jnp.exp2(-5.0 - jnp.arange(nh_ret, dtype=F32)))
    rw_params = (mu_r, mu_k, mu_v, mu_l, row(w0_decay[0]), w2p, row(a0[0]), a2p, row(k_k[0]), row(k_a[0]),
                 row(r_k[0]), row(ln_x_w[0]), row(ln_x_b[0]))

    n_c = bp + bsz
    n_c_pad = -(-n_c // SUBLANES) * SUBLANES
    c_all = jnp.pad(jnp.concatenate([c_prompt, c_sample], axis=0), ((0, n_c_pad - n_c), (0, 0)))
    mod3 = _adaln(c_all, w_ada[0], b_ada[0]).reshape(n_c_pad, 3, d)
    mod_p, mod_s = mod3[:bp], mod3[bp:bp + bsz]

    sl = SAMPLE_SLOTS
    row_s = bp * lp
    x_slots = jnp.pad(x_sample, ((0, 0), (1, sl - 1 - ls), (0, 0)))
    h_all, last_p, new_shift_s = _modulate(x_prompt, mod_p, x_slots, state_shift[0], mod_s, norm_w[0])
    new_shift_p = last_p[:, SUBLANES - 1, :]
    proj = _in_proj(h_all, w_t, n_main)
    lora_proj = _in_proj(h_all, w_t_lora, lora_pad, name="in_proj_lora")

    cos_p, sin_p = _rope_tables(jnp.arange(lp, dtype=F32), RET_HEAD_DIM // 2)
    y_ret_p, s_ret_p = _ret_prompt(proj, lg, cos_p, sin_p, bp, lp, nh_ret)
    y_rw_p, s_rw_p = _rwkv_prompt(proj, lora_proj, bp, lp, dr, *rw_params)
    y_prompt = _out_prompt(y_ret_p, y_rw_p, w_o, x_prompt, mod_p, final_norm_w)

    slot_pos = jnp.arange(sl, dtype=F32) - 1.0
    pos_s = jnp.where((slot_pos >= 0) & (slot_pos < ls), float(PAST_LEN) + slot_pos, 0.0)
    cos_s, sin_s = _rope_tables(pos_s, RET_HEAD_DIM // 2)
    y_ret_s, s_ret_s = _ret_sample(proj, row_s, lg, cos_s, sin_s, state_ret[0], nh_ret, ls)
    s0_rw = jnp.transpose(state_rwkv[0], (1, 2, 0, 3))
    y_rw_s, s_rw_s = _rwkv_sample(proj, lora_proj, row_s, s0_rw, ls, dr, *rw_params)
    y_slots = _out_sample(y_ret_s, y_rw_s, w_o, x_slots, mod_s, final_norm_w)
    y_sample = y_slots[:, 1:1 + ls, :]

    return (y_prompt, y_sample, s_ret_p[None], s_rw_p[None], new_shift_p[None],
            s_ret_s[None], s_rw_s[None], new_shift_s[None])
```

```python
import functools
import math

import jax
import jax.numpy as jnp
from jax import lax
from jax.experimental import pallas as pl
from jax.experimental.pallas import tpu as pltpu

F32 = jnp.float32
BF16 = jnp.bfloat16

RET_HEAD_DIM = 256
RWKV_HEAD_DIM = 64
RET_CHUNK = 128
RWKV_CHUNK = 64
RWKV_GROUP = 2
PAST_LEN = 16384
ROPE_THETA = 10000.0
NORM_EPS = 1e-6
GN_EPS = 1e-5
RWKV_GN_EPS = 64e-5
SAMPLE_SLOTS = 8
LANES = 128
SUBLANES = 8
VMEM_LIMIT_BYTES = 56 * 1024 * 1024

ADALN_COL_TILE = 768
MODULATE_ROW_TILE = 512
IN_PROJ_ROW_TILE = 1536
IN_PROJ_COL_TILE = 1024
OUT_PROJ_ROW_TILE = 512
OUT_SAMPLE_SEQ_TILE = 32
RET_SAMPLE_SEQ_TILE = 4
RET_STEP_CHUNKS = 4
RWKV_STEP_CHUNKS = 4
RWKV_SAMPLE_STEP_TILES = 1
RWKV_SAMPLE_LANE_TILE = 2048


def _params(sem):
    return pltpu.CompilerParams(dimension_semantics=sem, vmem_limit_bytes=VMEM_LIMIT_BYTES)


def _tile(n, cap, align):
    if n <= cap:
        return n
    t = (cap // align) * align
    while t >= align:
        if n % t == 0:
            return t
        t -= align
    return n


def _silu(x):
    return x * jax.nn.sigmoid(x)


def _dot(a, b):
    return jnp.dot(a, b, preferred_element_type=F32)


def _dot_nt(a, b):
    return lax.dot_general(a, b, (((1,), (1,)), ((), ())), preferred_element_type=F32)


def _dot_tn(a, b):
    return lax.dot_general(a, b, (((0,), (0,)), ((), ())), preferred_element_type=F32)


def _adaln_body(c_ref, w_ref, b_ref, o_ref):
    s = _silu(c_ref[...]).astype(BF16)
    o_ref[...] = _dot(s, w_ref[...].astype(BF16)) + b_ref[...]


def _adaln(c, w_ada, b_ada):
    rows, d = c.shape
    n = w_ada.shape[1]
    tn = _tile(n, ADALN_COL_TILE, LANES)
    return pl.pallas_call(
        _adaln_body,
        grid=(n // tn,),
        in_specs=[pl.BlockSpec((rows, d), lambda j: (0, 0)),
                  pl.BlockSpec((d, tn), lambda j: (0, j)),
                  pl.BlockSpec((1, tn), lambda j: (0, j))],
        out_specs=pl.BlockSpec((rows, tn), lambda j: (0, j)),
        out_shape=jax.ShapeDtypeStruct((rows, n), F32),
        compiler_params=_params(("parallel",)),
        name="adaln",
    )(c, w_ada, b_ada.reshape(1, n))


def _modulated(x, nw, shift, scale):
    ms = jnp.mean(x * x, axis=-1, keepdims=True)
    return x * lax.rsqrt(ms + NORM_EPS) * (nw * (1.0 + scale)) + shift


def _modulate_body(xp_ref, modp_ref, xs_ref, prev_ref, mods_ref, nw_ref, h_ref, last_ref, new_ref, *, n_prompt):
    s = pl.program_id(0)

    @pl.when(s < n_prompt)
    def _():
        m = modp_ref[0]
        h = _modulated(xp_ref[0], nw_ref[...], m[0:1], m[1:2])
        h_ref[...] = h.astype(BF16)
        tl = h.shape[0]
        last_ref[0] = h[tl - SUBLANES:tl]

    @pl.when(s >= n_prompt)
    def _():
        x = xs_ref[...]
        m = mods_ref[...]
        h = _modulated(x, nw_ref[...], m[:, 0:1, :], m[:, 1:2, :])
        slot = lax.broadcasted_iota(jnp.int32, h.shape, 1)
        full = jnp.where(slot == 0, prev_ref[...][:, None, :], h)
        bs, sl, d = x.shape
        h_ref[...] = full.reshape(bs * sl, d).astype(BF16)
        new_ref[...] = h[:, 4, :]


def _modulate(x_prompt, mod_p, x_slots, prev, mod_s, nw):
    b, l, d = x_prompt.shape
    bsz, sl, _ = x_slots.shape
    tl = _tile(l, min(MODULATE_ROW_TILE, bsz * sl), 16)
    nt = l // tl
    bs = tl // sl
    assert bsz % bs == 0 and bs % SUBLANES == 0
    n_prompt = b * nt
    n_sample = bsz // bs
    samp = lambda s: jnp.maximum(s - n_prompt, 0)
    pb = lambda s: jnp.minimum(s // nt, b - 1)
    return pl.pallas_call(
        functools.partial(_modulate_body, n_prompt=n_prompt),
        grid=(n_prompt + n_sample,),
        in_specs=[pl.BlockSpec((1, tl, d), lambda s: (pb(s), jnp.where(s < n_prompt, s % nt, nt - 1), 0)),
                  pl.BlockSpec((1, 3, d), lambda s: (pb(s), 0, 0)),
                  pl.BlockSpec((bs, sl, d), lambda s: (samp(s), 0, 0)),
                  pl.BlockSpec((bs, d), lambda s: (samp(s), 0)),
                  pl.BlockSpec((bs, 3, d), lambda s: (samp(s), 0, 0)),
                  pl.BlockSpec((1, d), lambda s: (0, 0))],
        out_specs=[pl.BlockSpec((tl, d), lambda s: (s, 0)),
                   pl.BlockSpec((1, SUBLANES, d), lambda s: (pb(s), 0, 0)),
                   pl.BlockSpec((bs, d), lambda s: (samp(s), 0))],
        out_shape=[jax.ShapeDtypeStruct((b * l + bsz * sl, d), BF16),
                   jax.ShapeDtypeStruct((b, SUBLANES, d), F32),
                   jax.ShapeDtypeStruct((bsz, d), F32)],
        compiler_params=_params(("arbitrary",)),
        name="modulate",
    )(x_prompt, mod_p, x_slots, prev, mod_s, nw.reshape(1, d))


def _in_proj_body(x_ref, wt_ref, o_ref, w_scr):
    @pl.when(pl.program_id(1) == 0)
    def _():
        w_scr[...] = wt_ref[...].astype(BF16)

    o_ref[...] = _dot_nt(x_ref[...], w_scr[...])


def _in_proj(x, wt, n, name="in_proj"):
    m, k = x.shape
    tm = _tile(m, IN_PROJ_ROW_TILE, 16)
    tn = _tile(n, IN_PROJ_COL_TILE, LANES)
    return pl.pallas_call(
        _in_proj_body,
        grid=(n // tn, m // tm),
        in_specs=[pl.BlockSpec((tm, k), lambda j, i: (i, 0)),
                  pl.BlockSpec((tn, k), lambda j, i: (j, 0))],
        out_specs=pl.BlockSpec((tm, tn), lambda j, i: (i, j)),
        out_shape=jax.ShapeDtypeStruct((m, n), F32),
        scratch_shapes=[pltpu.VMEM((tn, k), BF16)],
        compiler_params=_params(("arbitrary", "arbitrary")),
        name=name,
    )(x, wt)


def _token_shift(cur_ref, carry_ref, mu_ref, cols):
    cur = cur_ref[:, cols]
    tl = cur.shape[0]
    prev = pltpu.roll(cur, 1, 0)
    if carry_ref is not None:
        row = lax.broadcasted_iota(jnp.int32, cur.shape, 0)
        prev = jnp.where(row == 0, carry_ref[0:1, cols], prev)
        carry_ref[0:1, cols] = cur[tl - 1:tl, :]
    return cur + (prev - cur) * mu_ref[:, cols]


def _decay_and_rate(lo_tanh_b, lo_b, w0_ref, w2_ref, a0_ref, a2_ref, cols):
    dec = _dot(lo_tanh_b, w2_ref[:, cols])
    logw = -math.exp(-0.5) * jax.nn.sigmoid(w0_ref[:, cols] + dec)
    a = jax.nn.sigmoid(a0_ref[:, cols] + _dot(lo_b, a2_ref[:, cols]))
    return logw, a


def _zero_refs(*refs):
    for r in refs:
        r[...] = jnp.zeros_like(r)


def _split3(x):
    hi = x.astype(BF16)
    r1 = x - hi.astype(F32)
    mid = r1.astype(BF16)
    lo = (r1 - mid.astype(F32)).astype(BF16)
    return hi, mid, lo


def _dot_split3(m, parts):
    return _dot(m, parts[0]) + _dot(m, parts[1]) + _dot(m, parts[2])


def _rwkv_tile(pr_ref, pk_ref, pv_ref, pg_ref, pl_ref, mu_r, mu_k, mu_v, mu_l, w0_ref, w2_ref,
               a0_ref, a2_ref, kk_ref, ka_ref, rk_ref, lnw_ref, lnb_ref, y_ref,
               carries, seq, valid_slots, load_state, store_state):
    c = pr_ref.shape[0]
    hd = RWKV_HEAD_DIM
    gw = RWKV_GROUP * hd
    groups = range(pr_ref.shape[1] // gw)
    nseq = c // seq
    seqs = range(nseq)
    rows_of = [slice(q * seq, (q + 1) * seq) for q in seqs]
    c_r, c_k, c_v, c_l = carries

    lo = _token_shift(pl_ref, c_l, mu_l, slice(None))
    lo_b = lo.astype(BF16)
    lo_tanh_b = jnp.tanh(lo).astype(BF16)
    if valid_slots is not None:
        slot = lax.broadcasted_iota(jnp.int32, (c, 1), 0) % seq
        valid = (slot >= valid_slots[0]) & (slot <= valid_slots[1])

    ti = lax.broadcasted_iota(jnp.int32, (c, c), 0)
    tj = lax.broadcasted_iota(jnp.int32, (c, c), 1)
    same_seq = (ti // seq) == (tj // seq)
    tri_b = ((ti >= tj) & same_seq).astype(BF16)
    same_b = same_seq.astype(BF16)

    hr = lax.broadcasted_iota(jnp.int32, (gw, gw), 0) // hd
    hc = lax.broadcasted_iota(jnp.int32, (gw, gw), 1) // hd
    headmask = hr == hc
    tok = lax.broadcasted_iota(jnp.int32, (c, gw), 0)
    src = lax.broadcasted_iota(jnp.int32, (c, gw), 1) % hd
    same = (tok // seq) == (src // seq)
    strict = (tok > src) & same
    incl = (tok >= src) & same

    def bd(x):
        t = jnp.concatenate([x] * RWKV_GROUP, axis=0)
        return jnp.where(headmask, t, 0.0).astype(BF16)

    lane_head = lax.broadcasted_iota(jnp.int32, (c, gw), 1) // hd

    def head_sum(x):
        out = jnp.zeros_like(x)
        for j in range(RWKV_GROUP):
            mine = lane_head == j
            out = jnp.where(mine, jnp.sum(jnp.where(mine, x, 0.0), axis=-1, keepdims=True), out)
        return out

    def stages(gids):
        groups = range(len(gids))
        sls = [slice(g * gw, (g + 1) * gw) for g in gids]
        r = [_token_shift(pr_ref, c_r, mu_r, sl) for sl in sls]
        kw = [_token_shift(pk_ref, c_k, mu_k, sl) for sl in sls]
        v = [_token_shift(pv_ref, c_v, mu_v, sl) for sl in sls]
        lw_a = [_decay_and_rate(lo_tanh_b, lo_b, w0_ref, w2_ref, a0_ref, a2_ref, sl) for sl in sls]
        logw = [x[0] for x in lw_a]
        rate = [x[1] for x in lw_a]
        k = [kw[g] * (1.0 + (rate[g] - 1.0) * ka_ref[:, sls[g]]) for g in groups]
        kkr = [kw[g] * kk_ref[:, sls[g]] for g in groups]
        if valid_slots is not None:
            logw = [jnp.where(valid, x, 0.0) for x in logw]
            kkr = [jnp.where(valid, x, 0.0) for x in kkr]
            k = [jnp.where(valid, x, 0.0) for x in k]

        parts = [_split3(x) for x in logw]
        cum = [_dot_split3(tri_b, p) for p in parts]
        if nseq > 1:
            cum_end = [_dot_split3(same_b, p) for p in parts]
        else:
            cum_end = [cm[c - 1:c, :] for cm in cum]

        kk_sq = [head_sum(jnp.square(x)) for x in kkr]
        rk_sum = [head_sum(r[g] * k[g] * rk_ref[:, sls[g]]) for g in groups]
        kkn = [kkr[g] * lax.rsqrt(jnp.maximum(kk_sq[g], 1e-24)) for g in groups]
        b = [kkn[g] * rate[g] for g in groups]
        e_inv = [jnp.exp(-cum[g]) for g in groups]
        at = [-kkn[g] * jnp.exp(cum[g] - logw[g]) for g in groups]
        rt = [r[g] * jnp.exp(cum[g]) for g in groups]
        ar = [jnp.concatenate([at[g], rt[g]], axis=0).astype(BF16) for g in groups]
        states = [load_state(g) for g in gids]

        def from_state(g):
            if nseq == 1:
                fs = _dot_nt(ar[g], states[g][0].astype(BF16))
                return fs[:c], fs[c:]
            fa, fr = [], []
            for q in seqs:
                arq = jnp.concatenate([at[g][rows_of[q]], rt[g][rows_of[q]]], axis=0).astype(BF16)
                fs = _dot_nt(arq, states[g][q].astype(BF16))
                fa.append(fs[:seq])
                fr.append(fs[seq:])
            return jnp.concatenate(fa, axis=0), jnp.concatenate(fr, axis=0)

        fstate = [from_state(g) for g in groups]
        s_b = [_dot_nt(ar[g], bd(b[g] * e_inv[g])) for g in groups]
        s_k = [_dot_nt(ar[g], bd(k[g] * e_inv[g])) for g in groups]
        a_ab = [jnp.where(strict, x[:c], 0.0) for x in s_b]
        a_ak = [jnp.where(strict, x[:c], 0.0).astype(BF16) for x in s_k]
        a_r = [jnp.concatenate([jnp.where(incl, s_b[g][c:], 0.0), jnp.where(incl, s_k[g][c:], 0.0)],
                               axis=1).astype(BF16) for g in groups]

        steps = max(1, int(math.ceil(math.log2(seq))))
        pw = a_ab
        pm = a_ab
        pw_next = [_dot(pw[g].astype(BF16), bd(pw[g])) for g in groups] if steps > 1 else None
        for s in range(1, steps):
            pw = pw_next
            if s < steps - 1:
                res = [_dot(jnp.concatenate([pw[g], pm[g]], axis=0).astype(BF16), bd(pw[g])) for g in groups]
                pw_next = [x[:c] for x in res]
                pm = [pm[g] + pw[g] + res[g][c:] for g in groups]
            else:
                pm = [pm[g] + pw[g] + _dot(pm[g].astype(BF16), bd(pw[g])) for g in groups]

        bd_v = [bd(x) for x in v]
        w_rhs = [fstate[g][0] + _dot(a_ak[g], bd_v[g]) for g in groups]
        u = [w_rhs[g] + _dot(pm[g].astype(BF16), bd(w_rhs[g])) for g in groups]
        o = [fstate[g][1] + _dot(a_r[g], jnp.concatenate([bd(u[g]), bd_v[g]], axis=0)) for g in groups]
        for g in groups:
            e_end = jnp.exp(cum_end[g] - cum[g])
            b_end = b[g] * e_end
            k_end = k[g] * e_end
            for q in seqs:
                rq = rows_of[q]
                uv = jnp.concatenate([u[g][rq], v[g][rq]], axis=0).astype(BF16)
                bk = jnp.concatenate([b_end[rq], k_end[rq]], axis=0).astype(BF16)
                keep = jnp.exp(cum_end[g][q * seq:q * seq + 1, :]) if nseq > 1 else jnp.exp(cum_end[g])
                store_state(gids[g], q, states[g][q] * keep + _dot_tn(uv, bk))

        mean = [head_sum(x) * (1.0 / hd) for x in o]
        oc = [o[g] - mean[g] for g in groups]
        var = [head_sum(jnp.square(x)) * (1.0 / hd) for x in oc]
        for g in groups:
            sl = sls[g]
            og = oc[g] * lax.rsqrt(var[g] + RWKV_GN_EPS) * lnw_ref[:, sl] + lnb_ref[:, sl]
            y_ref[:, sl] = ((og + rk_sum[g] * v[g]) * _silu(pg_ref[:, sl])).astype(BF16)

    stages(list(groups))


def _head_mask(gw, hd):
    hr = lax.broadcasted_iota(jnp.int32, (gw, gw), 0) // hd
    hc = lax.broadcasted_iota(jnp.int32, (gw, gw), 1) // hd
    return hr == hc


def _rwkv_prompt_body(*refs):
    ins, (y_ref, so_ref), (c_r, c_k, c_v, c_l, s_scr) = refs[:18], refs[18:20], refs[20:]
    n = pl.program_id(1)
    hd = RWKV_HEAD_DIM
    gw = RWKV_GROUP * hd
    headmask = _head_mask(gw, hd)

    @pl.when(n == 0)
    def _():
        _zero_refs(c_r, c_k, c_v, c_l, s_scr)

    def load_state(g):
        return [s_scr[g]]

    def store_state(g, q, s):
        s_scr[g] = jnp.where(headmask, s, 0.0)

    c = RWKV_CHUNK
    for sub in range(y_ref.shape[0] // c):
        rows = pl.ds(sub * c, c)
        tile_ins = [r.at[rows] for r in ins[:5]] + list(ins[5:])
        _rwkv_tile(*tile_ins, y_ref.at[rows], (c_r, c_k, c_v, c_l), c, None, load_state, store_state)

    @pl.when(n == pl.num_programs(1) - 1)
    def _():
        for g in range(s_scr.shape[0]):
            sg = s_scr[g]
            for j in range(RWKV_GROUP):
                so_ref[0, g * RWKV_GROUP + j] = sg[j * hd:(j + 1) * hd, j * hd:(j + 1) * hd]


def _rwkv_prompt(proj, lora, nb, l, dr, mu_r, mu_k, mu_v, mu_l, w0, w2p, a0, a2p, k_k, k_a, r_k, ln_w, ln_b):
    assert l % RWKV_CHUNK == 0 and RWKV_CHUNK == RWKV_HEAD_DIM
    c = _tile(l, RWKV_STEP_CHUNKS * RWKV_CHUNK, RWKV_CHUNK)
    nchunk = l // c
    lp = lora.shape[1]
    nh = dr // RWKV_HEAD_DIM
    gw = RWKV_GROUP * RWKV_HEAD_DIM
    vec = lambda: pl.BlockSpec((1, dr), lambda b, n: (0, 0))
    sect = lambda s: pl.BlockSpec((c, dr), lambda b, n, s=s: (b * nchunk + n, 4 + s))
    return pl.pallas_call(
        _rwkv_prompt_body,
        grid=(nb, nchunk),
        in_specs=[sect(0), sect(1), sect(2), sect(3),
                  pl.BlockSpec((c, lp), lambda b, n: (b * nchunk + n, 0)),
                  vec(), vec(), vec(),
                  pl.BlockSpec((1, lp), lambda b, n: (0, 0)),
                  vec(),
                  pl.BlockSpec((lp, dr), lambda b, n: (0, 0)),
                  vec(),
                  pl.BlockSpec((lp, dr), lambda b, n: (0, 0)),
                  vec(), vec(), vec(), vec(), vec()],
        out_specs=[pl.BlockSpec((c, dr), lambda b, n: (b * nchunk + n, 0)),
                   pl.BlockSpec((1, nh, RWKV_HEAD_DIM, RWKV_HEAD_DIM), lambda b, n: (b, 0, 0, 0))],
        out_shape=[jax.ShapeDtypeStruct((nb * l, dr), BF16),
                   jax.ShapeDtypeStruct((nb, nh, RWKV_HEAD_DIM, RWKV_HEAD_DIM), F32)],
        scratch_shapes=[pltpu.VMEM((SUBLANES, dr), F32), pltpu.VMEM((SUBLANES, dr), F32),
                        pltpu.VMEM((SUBLANES, dr), F32), pltpu.VMEM((SUBLANES, lp), F32),
                        pltpu.VMEM((dr // gw, gw, gw), F32)],
        compiler_params=_params(("parallel", "arbitrary")),
        name="rwkv_chunked",
    )(proj, proj, proj, proj, lora, mu_r, mu_k, mu_v, mu_l, w0, w2p, a0, a2p, k_k, k_a, r_k, ln_w, ln_b)


def _rwkv_sample_body(*refs, n_tok):
    ins, s0_ref, y_ref, so_ref = refs[:18], refs[18], refs[19], refs[20]
    hd = RWKV_HEAD_DIM
    c = RWKV_CHUNK
    nseq = c // SAMPLE_SLOTS
    zero = jnp.zeros((hd, hd), F32)

    for sub in range(y_ref.shape[0] // c):
        seq0 = sub * nseq

        def load_state(g, seq0=seq0):
            out = []
            per_head = [jnp.swapaxes(s0_ref[g * RWKV_GROUP + j, :, pl.ds(seq0, nseq), :], 0, 1)
                        for j in range(RWKV_GROUP)]
            for q in range(nseq):
                rows = [jnp.concatenate([per_head[j][q] if i == j else zero
                                         for i in range(RWKV_GROUP)], axis=1) for j in range(RWKV_GROUP)]
                out.append(jnp.concatenate(rows, axis=0))
            return out

        def store_state(g, q, s, seq0=seq0):
            for j in range(RWKV_GROUP):
                so_ref[seq0 + q, g * RWKV_GROUP + j] = s[j * hd:(j + 1) * hd, j * hd:(j + 1) * hd]

        rows = pl.ds(sub * c, c)
        tile_ins = [r.at[rows] for r in ins[:5]] + list(ins[5:])
        _rwkv_tile(*tile_ins, y_ref.at[rows], (None, None, None, None), SAMPLE_SLOTS, (1, n_tok),
                   load_state, store_state)


def _rwkv_sample(proj, lora, row0, s0, n_tok, dr, mu_r, mu_k, mu_v, mu_l, w0, w2p, a0, a2p, k_k, k_a,
                 r_k, ln_w, ln_b):
    nh_all, _, bsz, _ = s0.shape
    rows = bsz * SAMPLE_SLOTS
    assert rows % RWKV_CHUNK == 0 and RWKV_CHUNK == RWKV_HEAD_DIM
    c = RWKV_SAMPLE_STEP_TILES * RWKV_CHUNK
    assert rows % c == 0 and row0 % c == 0
    t0 = row0 // c
    nseq = c // SAMPLE_SLOTS
    lp = lora.shape[1]
    hd = RWKV_HEAD_DIM
    dw = _tile(dr, RWKV_SAMPLE_LANE_TILE, RWKV_GROUP * hd)
    nw = dr // dw
    vec = lambda: pl.BlockSpec((1, dw), lambda i, hf: (0, hf))
    sect = lambda s: pl.BlockSpec((c, dw), lambda i, hf, s=s: (t0 + i, (4 + s) * nw + hf))
    lora_w = lambda: pl.BlockSpec((lp, dw), lambda i, hf: (0, hf))
    st_in = pl.BlockSpec((dw // hd, hd, nseq, hd), lambda i, hf: (hf, 0, i, 0))
    st_out = pl.BlockSpec((nseq, dw // hd, hd, hd), lambda i, hf: (i, hf, 0, 0))
    return pl.pallas_call(
        functools.partial(_rwkv_sample_body, n_tok=n_tok),
        grid=(rows // c, nw),
        in_specs=[sect(0), sect(1), sect(2), sect(3),
                  pl.BlockSpec((c, lp), lambda i, hf: (t0 + i, 0)),
                  vec(), vec(), vec(),
                  pl.BlockSpec((1, lp), lambda i, hf: (0, 0)),
                  vec(), lora_w(), vec(), lora_w(),
                  vec(), vec(), vec(), vec(), vec(), st_in],
        out_specs=[pl.BlockSpec((c, dw), lambda i, hf: (i, hf)), st_out],
        out_shape=[jax.ShapeDtypeStruct((rows, dr), BF16),
                   jax.ShapeDtypeStruct((bsz, nh_all, hd, hd), F32)],
        compiler_params=_params(("parallel", "parallel")),
        name="rwkv_chunked_sample",
    )(proj, proj, proj, proj, lora, mu_r, mu_k, mu_v, mu_l, w0, w2p, a0, a2p, k_k, k_a, r_k, ln_w, ln_b, s0)


def _rotary(x, cos, sin):
    half = x.shape[-1] // 2
    x1 = x[:, :half]
    x2 = x[:, half:]
    return jnp.concatenate([x1 * cos - x2 * sin, x1 * sin + x2 * cos], axis=-1)


def _group_norm_rows(o, eps):
    mean = jnp.mean(o, axis=-1, keepdims=True)
    oc = o - mean
    var = jnp.mean(oc * oc, axis=-1, keepdims=True)
    return oc * lax.rsqrt(var + eps)


def _ret_prompt_body(lg_ref, q_ref, k_ref, v_ref, g_ref, cos_ref, sin_ref, y_ref, s_ref, *, c):
    n = pl.program_id(1)
    hd = RET_HEAD_DIM
    heads = range(q_ref.shape[1] // hd)
    cols = [slice(h * hd, (h + 1) * hd) for h in heads]

    @pl.when(n == 0)
    def _():
        s_ref[...] = jnp.zeros_like(s_ref)

    ii = lax.broadcasted_iota(jnp.int32, (c, c), 0)
    jj = lax.broadcasted_iota(jnp.int32, (c, c), 1)
    diff = (ii - jj).astype(F32)
    causal = diff >= 0.0
    dist = jnp.maximum(diff, 0.0)
    ic = lax.broadcasted_iota(jnp.int32, (c, 1), 0).astype(F32)
    lg = [jnp.full((1, 1), lg_ref[h], F32) for h in heads]
    dmask = [jnp.where(causal, jnp.exp(dist * lg[h]), 0.0) for h in heads]
    cross_decay = [jnp.exp((ic + 1.0) * lg[h]) for h in heads]
    key_decay = [jnp.exp((c - 1.0 - ic) * lg[h]) for h in heads]
    chunk_decay = [jnp.exp(float(c) * lg[h]) for h in heads]

    s = [s_ref[0, h] for h in heads]
    for sub in range(q_ref.shape[0] // c):
        rows = slice(sub * c, (sub + 1) * c)
        cos = cos_ref[rows, :]
        sin = sin_ref[rows, :]
        q = [_rotary(q_ref[rows, cs], cos, sin).astype(BF16) for cs in cols]
        k = [_rotary(k_ref[rows, cs], cos, sin) * (hd ** -0.5) for cs in cols]
        vb = [v_ref[rows, cs].astype(BF16) for cs in cols]
        scores = [_dot_nt(q[h], k[h].astype(BF16)) * dmask[h] for h in heads]
        cross = [_dot(q[h], s[h].astype(BF16)) * cross_decay[h] for h in heads]
        o = [_dot(scores[h].astype(BF16), vb[h]) + cross[h] for h in heads]
        s = [s[h] * chunk_decay[h] + _dot_tn((k[h] * key_decay[h]).astype(BF16), vb[h]) for h in heads]
        for h in heads:
            y_ref[rows, cols[h]] = (_group_norm_rows(o[h], GN_EPS) * _silu(g_ref[rows, cols[h]])).astype(BF16)
    for h in heads:
        s_ref[0, h] = s[h]


def _ret_prompt(proj, lg, cos, sin, nb, l, nh):
    hd = RET_HEAD_DIM
    dr = nh * hd
    chunk = math.gcd(l, RET_CHUNK)
    c = _tile(l, RET_STEP_CHUNKS * chunk, chunk)
    nchunk = l // c
    sect = lambda s: pl.BlockSpec((c, dr), lambda b, n, s=s: (b * nchunk + n, s))
    tab = pl.BlockSpec((c, hd // 2), lambda b, n: (n, 0))
    return pl.pallas_call(
        functools.partial(_ret_prompt_body, c=chunk),
        grid=(nb, nchunk),
        in_specs=[pl.BlockSpec(memory_space=pltpu.SMEM), sect(0), sect(1), sect(2), sect(3), tab, tab],
        out_specs=[pl.BlockSpec((c, dr), lambda b, n: (b * nchunk + n, 0)),
                   pl.BlockSpec((1, nh, hd, hd), lambda b, n: (b, 0, 0, 0))],
        out_shape=[jax.ShapeDtypeStruct((nb * l, dr), BF16),
                   jax.ShapeDtypeStruct((nb, nh, hd, hd), F32)],
        compiler_params=_params(("parallel", "arbitrary")),
        name="retention_prompt",
    )(lg, proj, proj, proj, proj, cos, sin)


def _ret_sample_body(lg_ref, q_ref, k_ref, v_ref, g_ref, cos_ref, sin_ref, s0_ref, y_ref, s_ref, *, nh, n_tok):
    hd = RET_HEAD_DIM
    bs = s0_ref.shape[0]
    sl = SAMPLE_SLOTS
    cos = cos_ref[...]
    sin = sin_ref[...]
    slot_r = lax.broadcasted_iota(jnp.int32, (sl, 1), 0)
    tok_r = (slot_r - 1).astype(F32)
    valid_r = (slot_r >= 1) & (slot_r <= n_tok)
    ii = lax.broadcasted_iota(jnp.int32, (sl, sl), 0)
    jj = lax.broadcasted_iota(jnp.int32, (sl, sl), 1)
    diff = (ii - jj).astype(F32)
    pair_ok = (diff >= 0.0) & (jj >= 1) & (jj <= n_tok)
    lg = [jnp.full((1, 1), lg_ref[h], F32) for h in range(nh)]
    dmask = [jnp.where(pair_ok, jnp.exp(jnp.maximum(diff, 0.0) * lg[h]), 0.0) for h in range(nh)]
    cross_decay = [jnp.exp((tok_r + 1.0) * lg[h]) for h in range(nh)]
    key_decay = [jnp.where(valid_r, jnp.exp((n_tok - 1.0 - tok_r) * lg[h]), 0.0) for h in range(nh)]
    chunk_decay = [jnp.exp(float(n_tok) * lg[h]) for h in range(nh)]

    pairs = [(b, h) for h in range(nh) for b in range(bs)]
    at = [(slice(b * sl, (b + 1) * sl), slice(h * hd, (h + 1) * hd)) for b, h in pairs]
    q = [_rotary(q_ref[r, c], cos, sin).astype(BF16) for r, c in at]
    k = [_rotary(k_ref[r, c], cos, sin) * (hd ** -0.5) for r, c in at]
    vb = [v_ref[r, c].astype(BF16) for r, c in at]
    s = [s0_ref[b, h] for b, h in pairs]
    scores = [_dot_nt(q[i], k[i].astype(BF16)) * dmask[h] for i, (b, h) in enumerate(pairs)]
    cross = [_dot(q[i], s[i].astype(BF16)) * cross_decay[h] for i, (b, h) in enumerate(pairs)]
    o = [_dot(scores[i].astype(BF16), vb[i]) + cross[i] for i in range(len(pairs))]
    for i, (b, h) in enumerate(pairs):
        kd_t = (k[i] * key_decay[h]).T.astype(BF16)
        s_ref[b, h] = s[i] * chunk_decay[h] + _dot(kd_t, vb[i])
    for i, (r, c) in enumerate(at):
        y_ref[r, c] = (_group_norm_rows(o[i], GN_EPS) * _silu(g_ref[r, c])).astype(BF16)


def _ret_sample(proj, row0, lg, cos, sin, s0, nh, n_tok):
    bsz = s0.shape[0]
    hd = RET_HEAD_DIM
    dr = nh * hd
    bs = _tile(bsz, RET_SAMPLE_SEQ_TILE, 2)
    rows = bs * SAMPLE_SLOTS
    assert row0 % rows == 0
    t0 = row0 // rows
    sect = lambda s: pl.BlockSpec((rows, dr), lambda i, s=s: (t0 + i, s))
    tab = pl.BlockSpec((SAMPLE_SLOTS, hd // 2), lambda i: (0, 0))
    st = pl.BlockSpec((bs, nh, hd, hd), lambda i: (i, 0, 0, 0))
    return pl.pallas_call(
        functools.partial(_ret_sample_body, nh=nh, n_tok=n_tok),
        grid=(bsz // bs,),
        in_specs=[pl.BlockSpec(memory_space=pltpu.SMEM), sect(0), sect(1), sect(2), sect(3), tab, tab, st],
        out_specs=[pl.BlockSpec((rows, dr), lambda i: (i, 0)), st],
        out_shape=[jax.ShapeDtypeStruct((bsz * SAMPLE_SLOTS, dr), BF16),
                   jax.ShapeDtypeStruct(s0.shape, F32)],
        compiler_params=_params(("parallel",)),
        name="retention_sample",
    )(lg, proj, proj, proj, proj, cos, sin, s0)


def _out_core(yr_ref, yw_ref, w1_ref, w2_ref, x, gate, fw):
    y = _dot(yr_ref[...], w1_ref[...]) + _dot(yw_ref[...], w2_ref[...])
    xn = x + gate * y
    ms = jnp.mean(xn * xn, axis=-1, keepdims=True)
    return xn * lax.rsqrt(ms + NORM_EPS) * fw


def _out_prompt_body(yr_ref, yw_ref, w1_ref, w2_ref, x_ref, mod_ref, fw_ref, o_ref):
    o_ref[0] = _out_core(yr_ref, yw_ref, w1_ref, w2_ref, x_ref[0], mod_ref[0][2:3], fw_ref[...])


def _out_prompt(y_ret, y_rw, w, x, mod3, fw):
    b, l, d = x.shape
    dr = w.shape[0] // 2
    tl = _tile(l, OUT_PROJ_ROW_TILE, 16)
    nt = l // tl
    wspec = lambda half: pl.BlockSpec((dr, d), lambda bi, i: (half, 0), pipeline_mode=pl.Buffered(1))
    yspec = pl.BlockSpec((tl, dr), lambda bi, i: (bi * nt + i, 0))
    return pl.pallas_call(
        _out_prompt_body,
        grid=(b, nt),
        in_specs=[yspec, yspec, wspec(0), wspec(1),
                  pl.BlockSpec((1, tl, d), lambda bi, i: (bi, i, 0)),
                  pl.BlockSpec((1, 3, d), lambda bi, i: (bi, 0, 0)),
                  pl.BlockSpec((1, d), lambda bi, i: (0, 0))],
        out_specs=pl.BlockSpec((1, tl, d), lambda bi, i: (bi, i, 0)),
        out_shape=jax.ShapeDtypeStruct((b, l, d), F32),
        compiler_params=_params(("parallel", "parallel")),
        name="out_proj_prompt",
    )(y_ret, y_rw, w, w, x, mod3, fw.reshape(1, d))


def _out_sample_body(yr_ref, yw_ref, w1_ref, w2_ref, x_ref, mod_ref, fw_ref, o_ref):
    bs, sl, d = x_ref.shape
    gate = jnp.broadcast_to(mod_ref[...][:, 2:3, :], (bs, sl, d)).reshape(bs * sl, d)
    x = x_ref[...].reshape(bs * sl, d)
    o = _out_core(yr_ref, yw_ref, w1_ref, w2_ref, x, gate, fw_ref[...])
    o_ref[...] = o.reshape(bs, sl, d)


def _out_sample(y_ret, y_rw, w, x_slots, mod3, fw):
    bsz, sl, d = x_slots.shape
    dr = w.shape[0] // 2
    bs = _tile(bsz, OUT_SAMPLE_SEQ_TILE, SUBLANES)
    wspec = lambda half: pl.BlockSpec((dr, d), lambda i: (half, 0), pipeline_mode=pl.Buffered(1))
    yspec = pl.BlockSpec((bs * sl, dr), lambda i: (i, 0))
    xspec = pl.BlockSpec((bs, sl, d), lambda i: (i, 0, 0))
    return pl.pallas_call(
        _out_sample_body,
        grid=(bsz // bs,),
        in_specs=[yspec, yspec, wspec(0), wspec(1), xspec,
                  pl.BlockSpec((bs, 3, d), lambda i: (i, 0, 0)),
                  pl.BlockSpec((1, d), lambda i: (0, 0))],
        out_specs=xspec,
        out_shape=jax.ShapeDtypeStruct((bsz, sl, d), F32),
        compiler_params=_params(("parallel",)),
        name="out_proj_sample",
    )(y_ret, y_rw, w, w, x_slots, mod3, fw.reshape(1, d))


def _rope_tables(pos, half):
    inv_freq = ROPE_THETA ** (-jnp.arange(half, dtype=F32) / half)
    ang = pos[:, None] * inv_freq[None, :]
    return jnp.cos(ang), jnp.sin(ang)


def kernel(x_prompt, x_sample, c_prompt, c_sample, state_ret, state_rwkv, state_shift, norm_w, w_ada,
           b_ada, w_in, mu_shift, w0_decay, w2_decay, a0, a2, k_k, k_a, r_k, ln_x_w, ln_x_b, w_out,
           final_norm_w):
    depth = w_in.shape[0]
    assert depth == 1, "single-layer trunk"
    bp, lp, d = x_prompt.shape
    bsz, ls, _ = x_sample.shape
    assert ls == 4, "sample path packs 4 tokens into slots 1..4"
    dr = d
    nh_ret = dr // RET_HEAD_DIM
    n_main = 8 * dr
    lora = w2_decay.shape[1]
    lora_pad = -(-2 * lora // LANES) * LANES

    w_t = jnp.swapaxes(w_in[0], 0, 1)
    w_t_lora = jnp.pad(w_t[n_main:], ((0, lora_pad - 2 * lora), (0, 0)))
    w_o = w_out[0].astype(BF16)
    mu = mu_shift[0]
    row = lambda p: p.reshape(1, -1)
    mu_r, mu_k, mu_v = row(mu[0:dr]), row(mu[dr:2 * dr]), row(mu[2 * dr:3 * dr])
    mu_l = jnp.pad(mu[3 * dr:], (0, lora_pad - 2 * lora)).reshape(1, lora_pad)
    w2p = jnp.pad(w2_decay[0], ((0, lora_pad - lora), (0, 0))).astype(BF16)
    a2p = jnp.pad(a2[0], ((lora, lora_pad - 2 * lora), (0, 0))).astype(BF16)
    lg = jnp.log1p(-jnp.exp2(-5.0 - jnp.arange(nh_ret, dtype=F32)))
    rw_params = (mu_r, mu_k, mu_v, mu_l, row(w0_decay[0]), w2p, row(a0[0]), a2p, row(k_k[0]), row(k_a[0]),
                 row(r_k[0]), row(ln_x_w[0]), row(ln_x_b[0]))

    n_c = bp + bsz
    n_c_pad = -(-n_c // SUBLANES) * SUBLANES
    c_all = jnp.pad(jnp.concatenate([c_prompt, c_sample], axis=0), ((0, n_c_pad - n_c), (0, 0)))
    mod3 = _adaln(c_all, w_ada[0], b_ada[0]).reshape(n_c_pad, 3, d)
    mod_p, mod_s = mod3[:bp], mod3[bp:bp + bsz]

    sl = SAMPLE_SLOTS
    row_s = bp * lp
    x_slots = jnp.pad(x_sample, ((0, 0), (1, sl - 1 - ls), (0, 0)))
    h_all, last_p, new_shift_s = _modulate(x_prompt, mod_p, x_slots, state_shift[0], mod_s, norm_w[0])
    new_shift_p = last_p[:, SUBLANES - 1, :]
    proj = _in_proj(h_all, w_t, n_main)
    lora_proj = _in_proj(h_all, w_t_lora, lora_pad, name="in_proj_lora")

    cos_p, sin_p = _rope_tables(jnp.arange(lp, dtype=F32), RET_HEAD_DIM // 2)
    y_ret_p, s_ret_p = _ret_prompt(proj, lg, cos_p, sin_p, bp, lp, nh_ret)
    y_rw_p, s_rw_p = _rwkv_prompt(proj, lora_proj, bp, lp, dr, *rw_params)
    y_prompt = _out_prompt(y_ret_p, y_rw_p, w_o, x_prompt, mod_p, final_norm_w)

    slot_pos = jnp.arange(sl, dtype=F32) - 1.0
    pos_s = jnp.where((slot_pos >= 0) & (slot_pos < ls), float(PAST_LEN) + slot_pos, 0.0)
    cos_s, sin_s = _rope_tables(pos_s, RET_HEAD_DIM // 2)
    y_ret_s, s_ret_s = _ret_sample(proj, row_s, lg, cos_s, sin_s, state_ret[0], nh_ret, ls)
    s0_rw = jnp.transpose(state_rwkv[0], (1, 2, 0, 3))
    y_rw_s, s_rw_s = _rwkv_sample(proj, lora_proj, row_s, s0_rw, ls, dr, *rw_params)
    y_slots = _out_sample(y_ret_s, y_rw_s, w_o, x_slots, mod_s, final_norm_w)
    y_sample = y_slots[:, 1:1 + ls, :]

    return (y_prompt, y_sample, s_ret_p[None], s_rw_p[None], new_shift_p[None],
            s_ret_s[None], s_rw_s[None], new_shift_s[None])
```
